```python
import math
import jax, jax.numpy as jnp
from jax import lax
import numpy as np

D_MODEL = 1024
BATCH = 32
SEQ = 2048
DEPTH = 1

MEM_LEN = 256
DIFF_HEADS = 4
DIFF_HEAD_DIM = 64
DIFF_WIDTH = DIFF_HEADS * 2 * DIFF_HEAD_DIM
MOBA_HEADS = 8
MOBA_HEAD_DIM = 64
MOBA_WIDTH = MOBA_HEADS * MOBA_HEAD_DIM
MIX_WIDTH = DIFF_WIDTH + MOBA_WIDTH
MOBA_BLOCK = 256
MOBA_TOPK = 3
Q_BLOCK = 128
ROPE_THETA = 500000.0
ROT_DIM = 64 // 4
MEM_HEADS = 4
MEM_HEAD_DIM = D_MODEL // MEM_HEADS
N_EXPERTS = 32
TOP_K = 4
D_FF = D_MODEL
SWIGLU_ALPHA = 1.702
SWIGLU_LIMIT = 7.0
LN_EPS = 1e-5
RMS_EPS = 1e-5
DEEPNORM_ALPHA = (2.0 * DEPTH) ** 0.25
DEEPNORM_BETA = (8.0 * DEPTH) ** -0.25

kernel_name = "hybrid_diffattn_moba_memxattn_moe_deepnorm"


def layer_norm(x, g, b):
    xf = x.astype(jnp.float32)
    mu = jnp.mean(xf, axis=-1, keepdims=True)
    var = jnp.mean(jnp.square(xf - mu), axis=-1, keepdims=True)
    return ((xf - mu) * lax.rsqrt(var + LN_EPS) * g + b).astype(x.dtype)


def rms_norm(x, g):
    xf = x.astype(jnp.float32)
    return (xf * lax.rsqrt(jnp.mean(xf * xf, axis=-1, keepdims=True) + RMS_EPS) * g).astype(x.dtype)


def rope_tables(seq):
    inv_freq = ROPE_THETA ** (-jnp.arange(0, ROT_DIM, 2, dtype=jnp.float32) / ROT_DIM)
    ang = jnp.arange(seq, dtype=jnp.float32)[:, None] * inv_freq[None, :]
    return jnp.cos(ang), jnp.sin(ang)


def apply_partial_rope(x, cos, sin):
    half = cos.shape[-1]
    bshape = (cos.shape[0],) + (1,) * (x.ndim - 3) + (half,)
    c = cos.reshape(bshape).astype(x.dtype)
    s = sin.reshape(bshape).astype(x.dtype)
    x1, x2, rest = x[..., :half], x[..., half:2 * half], x[..., 2 * half:]
    return jnp.concatenate([x1 * c - x2 * s, x2 * c + x1 * s, rest], axis=-1)


def differential_attention(q, k, v, lam, lambda_init, subln_g):
    B, S = q.shape[0], q.shape[1]
    nqb = S // Q_BLOCK
    scale = DIFF_HEAD_DIM ** -0.5
    q_blocks = q.reshape(B, nqb, Q_BLOCK, DIFF_HEADS, 2, DIFF_HEAD_DIM).swapaxes(0, 1)
    kpos = jnp.arange(S)

    def one_block(args):
        qi, i = args
        qpos = i * Q_BLOCK + jnp.arange(Q_BLOCK)
        s = jnp.einsum('bqhcd,bkhcd->bhcqk', qi, k).astype(jnp.float32) * scale
        s = jnp.where(kpos[None, :] <= qpos[:, None], s, -jnp.inf)
        p = jax.nn.softmax(s, axis=-1)
        a = p[:, :, 0] - lam * p[:, :, 1]
        return jnp.einsum('bhqk,bkhe->bqhe', a.astype(v.dtype), v)

    o = lax.map(one_block, (q_blocks, jnp.arange(nqb)))
    o = o.swapaxes(0, 1).reshape(B, S, DIFF_HEADS, 2 * DIFF_HEAD_DIM)
    o = rms_norm(o, subln_g) * (1.0 - lambda_init)
    return o.reshape(B, S, DIFF_WIDTH)


def moba_attention(q, k, v):
    B, S, H, d = q.shape
    scale = d ** -0.5
    s_pad = -(-S // MOBA_BLOCK) * MOBA_BLOCK
    nb = s_pad // MOBA_BLOCK
    n_sel = min(MOBA_TOPK, nb)
    pad = ((0, 0), (0, s_pad - S), (0, 0), (0, 0))
    kb = jnp.pad(k, pad).reshape(B, nb, MOBA_BLOCK, H, d).transpose(0, 3, 1, 2, 4)
    vb = jnp.pad(v, pad).reshape(B, nb, MOBA_BLOCK, H, d).transpose(0, 3, 1, 2, 4)
    k_mean = jnp.mean(kb.astype(jnp.float32), axis=3)
    qh = q.transpose(0, 2, 1, 3)
    gate = jnp.einsum('bhsd,bhnd->bhsn', qh.astype(jnp.float32), k_mean)
    fully_past = jnp.arange(nb)[None, :] < (jnp.arange(S) // MOBA_BLOCK)[:, None]
    gate = jnp.where(fully_past, gate, -jnp.inf)
    _, sel = lax.top_k(gate, n_sel)
    n_chunks = S // Q_BLOCK

    def per_batch(args):
        q_b, kb_b, vb_b, sel_b = args

        def per_chunk(c):
            q0 = c * Q_BLOCK
            qc = lax.dynamic_slice_in_dim(q_b, q0, Q_BLOCK, axis=1)
            sc = lax.dynamic_slice_in_dim(sel_b, q0, Q_BLOCK, axis=1)
            qpos = q0 + jnp.arange(Q_BLOCK)
            own_blk = q0 // MOBA_BLOCK
            k_sel = jax.vmap(lambda kbh, sh: kbh[sh])(kb_b, sc)
            v_sel = jax.vmap(lambda vbh, sh: vbh[sh])(vb_b, sc)
            s_sel = jnp.einsum('hqd,hqnld->hqnl', qc, k_sel).astype(jnp.float32) * scale
            slot_valid = jnp.arange(n_sel) < own_blk
            s_sel = jnp.where(slot_valid[None, None, :, None], s_sel, -jnp.inf)
            k_own = lax.dynamic_index_in_dim(kb_b, own_blk, axis=1, keepdims=False)
            v_own = lax.dynamic_index_in_dim(vb_b, own_blk, axis=1, keepdims=False)
            s_own = jnp.einsum('hqd,hld->hql', qc, k_own).astype(jnp.float32) * scale
            kpos_own = own_blk * MOBA_BLOCK + jnp.arange(MOBA_BLOCK)
            s_own = jnp.where(kpos_own[None, None, :] <= qpos[None, :, None], s_own, -jnp.inf)
            s_all = jnp.concatenate([s_sel.reshape(H, Q_BLOCK, n_sel * MOBA_BLOCK), s_own], axis=-1)
            p = jax.nn.softmax(s_all, axis=-1).astype(v_b_dtype)
            p_sel = p[..., :n_sel * MOBA_BLOCK].reshape(H, Q_BLOCK, n_sel, MOBA_BLOCK)
            p_own = p[..., n_sel * MOBA_BLOCK:]
            return (jnp.einsum('hqnl,hqnld->hqd', p_sel, v_sel)
                    + jnp.einsum('hql,hld->hqd', p_own, v_own))

        o = lax.map(per_chunk, jnp.arange(n_chunks))
        return o.transpose(1, 0, 2, 3).reshape(H, S, d)

    v_b_dtype = v.dtype
    o = lax.map(per_batch, (qh, kb, vb, sel))
    return o.transpose(0, 2, 1, 3).reshape(B, S, MOBA_WIDTH)


def hybrid_mixer(h, w_in, lq1, lk1, lq2, lk2, subln_g, w_out, lambda_init, cos, sin):
    B, S, _ = h.shape
    proj = h @ w_in
    cuts = [DIFF_WIDTH, 2 * DIFF_WIDTH, 3 * DIFF_WIDTH,
            3 * DIFF_WIDTH + MOBA_WIDTH, 3 * DIFF_WIDTH + 2 * MOBA_WIDTH]
    dq, dk, dv, mq, mk, mv = jnp.split(proj, cuts, axis=-1)
    dq = apply_partial_rope(dq.reshape(B, S, DIFF_HEADS, 2, DIFF_HEAD_DIM), cos, sin)
    dk = apply_partial_rope(dk.reshape(B, S, DIFF_HEADS, 2, DIFF_HEAD_DIM), cos, sin)
    dv = dv.reshape(B, S, DIFF_HEADS, 2 * DIFF_HEAD_DIM)
    lam = (jnp.exp(jnp.sum(lq1.astype(jnp.float32) * lk1.astype(jnp.float32)))
           - jnp.exp(jnp.sum(lq2.astype(jnp.float32) * lk2.astype(jnp.float32))) + lambda_init)
    out_a = differential_attention(dq, dk, dv, lam, lambda_init, subln_g)
    mq = apply_partial_rope(mq.reshape(B, S, MOBA_HEADS, MOBA_HEAD_DIM), cos, sin)
    mk = apply_partial_rope(mk.reshape(B, S, MOBA_HEADS, MOBA_HEAD_DIM), cos, sin)
    mv = mv.reshape(B, S, MOBA_HEADS, MOBA_HEAD_DIM)
    out_b = moba_attention(mq, mk, mv)
    return jnp.concatenate([out_a, out_b], axis=-1) @ w_out


def memory_cross_attention(h, mem_n, w_q, w_kv, w_o):
    B, S, D = h.shape
    M = mem_n.shape[1]
    q = (h @ w_q).reshape(B, S, MEM_HEADS, MEM_HEAD_DIM)
    k, v = jnp.split(mem_n @ w_kv, 2, axis=-1)
    k = k.reshape(B, M, MEM_HEADS, MEM_HEAD_DIM)
    v = v.reshape(B, M, MEM_HEADS, MEM_HEAD_DIM)
    s = jnp.einsum('bshd,bmhd->bhsm', q, k).astype(jnp.float32) * (MEM_HEAD_DIM ** -0.5)
    p = jax.nn.softmax(s, axis=-1).astype(v.dtype)
    o = jnp.einsum('bhsm,bmhd->bshd', p, v).reshape(B, S, D)
    return o @ w_o


def clamped_swiglu(hidden):
    glu, lin = hidden[..., ::2], hidden[..., 1::2]
    glu = jnp.minimum(glu, SWIGLU_LIMIT)
    lin = jnp.clip(lin, -SWIGLU_LIMIT, SWIGLU_LIMIT)
    return glu * jax.nn.sigmoid(SWIGLU_ALPHA * glu) * (lin + 1.0)


def moe_ffn(h, w_router, b_router, w_mlp1, b_mlp1, w_mlp2, b_mlp2):
    B, S, D = h.shape
    xt = h.reshape(B * S, D)
    logits = (xt @ w_router + b_router).astype(jnp.float32)
    top_val, top_idx = lax.top_k(logits, TOP_K)
    top_w = jax.nn.softmax(top_val, axis=-1)
    combine = jnp.einsum('tk,tke->te', top_w,
                         jax.nn.one_hot(top_idx, N_EXPERTS, dtype=jnp.float32)).astype(h.dtype)
    y = jnp.zeros_like(xt)
    for e in range(N_EXPERTS):
        hid = clamped_swiglu(xt @ w_mlp1[e] + b_mlp1[e])
        y = y + combine[:, e:e + 1] * (hid @ w_mlp2[e] + b_mlp2[e])
    return y.reshape(B, S, D)


def setup_inputs(seed: int = 0) -> dict:
    key = jax.random.key(seed)
    ks = jax.random.split(key, 30)
    f32 = jnp.float32
    L, D = DEPTH, D_MODEL

    def nrm(k, shape, scale):
        return jax.random.normal(k, shape, f32) * scale

    x = nrm(ks[0], (BATCH, SEQ, D), 1.0)
    mem = nrm(ks[1], (BATCH, MEM_LEN, D), 1.0)
    w_in = jnp.concatenate([
        nrm(ks[2], (L, D, 2 * DIFF_WIDTH), D ** -0.5),
        nrm(ks[3], (L, D, DIFF_WIDTH), D ** -0.5 * DEEPNORM_BETA),
        nrm(ks[4], (L, D, 2 * MOBA_WIDTH), D ** -0.5),
        nrm(ks[5], (L, D, MOBA_WIDTH), D ** -0.5 * DEEPNORM_BETA)], axis=-1)
    diff_lambda_q1 = nrm(ks[6], (L, DIFF_HEAD_DIM), 0.1)
    diff_lambda_k1 = nrm(ks[7], (L, DIFF_HEAD_DIM), 0.1)
    diff_lambda_q2 = nrm(ks[8], (L, DIFF_HEAD_DIM), 0.1)
    diff_lambda_k2 = nrm(ks[9], (L, DIFF_HEAD_DIM), 0.1)
    diff_subln_g = 1.0 + nrm(ks[10], (L, 2 * DIFF_HEAD_DIM), 0.05)
    w_mix_out = nrm(ks[11], (L, MIX_WIDTH, D), MIX_WIDTH ** -0.5 * DEEPNORM_BETA)
    ln1_g = 1.0 + nrm(ks[12], (L, D), 0.05)
    ln1_b = nrm(ks[13], (L, D), 0.02)
    mem_ln_g = 1.0 + nrm(ks[14], (D,), 0.05)
    mem_ln_b = nrm(ks[15], (D,), 0.02)
    w_mem_q = nrm(ks[16], (L, D, D), D ** -0.5)
    w_mem_kv = jnp.concatenate([nrm(ks[17], (L, D, D), D ** -0.5),
                                nrm(ks[18], (L, D, D), D ** -0.5 * DEEPNORM_BETA)], axis=-1)
    w_mem_o = nrm(ks[19], (L, D, D), D ** -0.5 * DEEPNORM_BETA)
    ln2_g = 1.0 + nrm(ks[20], (L, D), 0.05)
    ln2_b = nrm(ks[21], (L, D), 0.02)
    w_router = nrm(ks[22], (L, D, N_EXPERTS), D ** -0.5)
    b_router = nrm(ks[23], (L, N_EXPERTS), 0.01)
    w_mlp1 = nrm(ks[24], (L, N_EXPERTS, D, 2 * D_FF), D ** -0.5)
    b_mlp1 = nrm(ks[25], (L, N_EXPERTS, 2 * D_FF), 0.01)
    w_mlp2 = nrm(ks[26], (L, N_EXPERTS, D_FF, D), D_FF ** -0.5 * DEEPNORM_BETA)
    b_mlp2 = nrm(ks[27], (L, N_EXPERTS, D), 0.01)
    ln3_g = 1.0 + nrm(ks[28], (L, D), 0.05)
    ln3_b = nrm(ks[29], (L, D), 0.02)
    return {"x": x, "mem": mem, "w_in": w_in,
            "diff_lambda_q1": diff_lambda_q1, "diff_lambda_k1": diff_lambda_k1,
            "diff_lambda_q2": diff_lambda_q2, "diff_lambda_k2": diff_lambda_k2,
            "diff_subln_g": diff_subln_g, "w_mix_out": w_mix_out, "ln1_g": ln1_g, "ln1_b": ln1_b,
            "mem_ln_g": mem_ln_g, "mem_ln_b": mem_ln_b, "w_mem_q": w_mem_q, "w_mem_kv": w_mem_kv,
            "w_mem_o": w_mem_o, "ln2_g": ln2_g, "ln2_b": ln2_b, "w_router": w_router,
            "b_router": b_router, "w_mlp1": w_mlp1, "b_mlp1": b_mlp1, "w_mlp2": w_mlp2,
            "b_mlp2": b_mlp2, "ln3_g": ln3_g, "ln3_b": ln3_b}


def reference(x, mem, w_in, diff_lambda_q1, diff_lambda_k1, diff_lambda_q2, diff_lambda_k2,
              diff_subln_g, w_mix_out, ln1_g, ln1_b, mem_ln_g, mem_ln_b, w_mem_q, w_mem_kv,
              w_mem_o, ln2_g, ln2_b, w_router, b_router, w_mlp1, b_mlp1, w_mlp2, b_mlp2,
              ln3_g, ln3_b):
    S = x.shape[1]
    cos, sin = rope_tables(S)
    mem_n = layer_norm(mem, mem_ln_g, mem_ln_b)
    h = x
    for l in range(DEPTH):
        lambda_init = 0.8 - 0.6 * math.exp(-0.3 * l)
        mix = hybrid_mixer(h, w_in[l], diff_lambda_q1[l], diff_lambda_k1[l], diff_lambda_q2[l],
                           diff_lambda_k2[l], diff_subln_g[l], w_mix_out[l], lambda_init, cos, sin)
        h = layer_norm(DEEPNORM_ALPHA * h + mix, ln1_g[l], ln1_b[l])
        xatt = memory_cross_attention(h, mem_n, w_mem_q[l], w_mem_kv[l], w_mem_o[l])
        h = layer_norm(DEEPNORM_ALPHA * h + xatt, ln2_g[l], ln2_b[l])
        ffn = moe_ffn(h, w_router[l], b_router[l], w_mlp1[l], b_mlp1[l], w_mlp2[l], b_mlp2[l])
        h = layer_norm(DEEPNORM_ALPHA * h + ffn, ln3_g[l], ln3_b[l])
    return h
```

```python
import functools
import math

import jax
import jax.numpy as jnp
from jax import lax
from jax.experimental import pallas as pl
from jax.experimental.pallas import tpu as pltpu

D_MODEL = 1024
DIFF_HEADS = 4
HEAD_DIM = 64
DIFF_WIDTH = DIFF_HEADS * 2 * HEAD_DIM
MOBA_HEADS = 8
MOBA_WIDTH = MOBA_HEADS * HEAD_DIM
MOBA_BLOCK = 256
MOBA_TOPK = 3
ROPE_THETA = 500000.0
ROT_DIM = HEAD_DIM // 4
MEM_HEADS = 4
MEM_HEAD_DIM = D_MODEL // MEM_HEADS
N_EXPERTS = 32
TOP_K = 4
D_FF = D_MODEL
SWIGLU_ALPHA = 1.702
SWIGLU_LIMIT = 7.0
LN_EPS = 1e-5
RMS_EPS = 1e-5
DEEPNORM_ALPHA = 2.0 ** 0.25
LAMBDA_INIT = 0.8 - 0.6 * math.exp(0.0)

LANES = 128
V7X_VMEM_LIMIT = 56 * 1024 * 1024

_CDT = jnp.bfloat16
_F32 = jnp.float32
_NEG = -1e30

_CB_DQ, _CB_DK, _CB_DV = 0, 4, 8
_CB_MQ, _CB_MK, _CB_MV = 12, 16, 20
PROJ_WIDTH = 3 * (DIFF_WIDTH + MOBA_WIDTH)


def _cparams(sem, vmem=V7X_VMEM_LIMIT):
    return pltpu.CompilerParams(dimension_semantics=sem, vmem_limit_bytes=vmem)


def _dot(a, b):
    return jnp.dot(a, b, preferred_element_type=_F32)


def _dot_nt(a, b):
    return lax.dot_general(a, b, (((1,), (1,)), ((), ())), preferred_element_type=_F32)


def _layer_norm(x, g, b):
    mu = jnp.mean(x, axis=-1, keepdims=True)
    xc = x - mu
    var = jnp.mean(xc * xc, axis=-1, keepdims=True)
    return xc * lax.rsqrt(var + LN_EPS) * g + b


_ROPE_BLOCKS = tuple(range(_CB_DQ, _CB_DV)) + tuple(range(_CB_MQ, _CB_MV))


def _inproj_kernel(x_ref, w_ref, cos_ref, sa_ref, sb_ref, o_ref, km_ref, *, tm):
    x = x_ref[...].astype(_CDT)
    cos, sa, sb = cos_ref[...], sa_ref[...], sb_ref[...]
    seg_w = 4 * LANES
    for seg in range(PROJ_WIDTH // seg_w):
        pseg = _dot(x, w_ref[:, seg * seg_w:(seg + 1) * seg_w])
        for c in range(4):
            cb = seg * 4 + c
            cols = slice(cb * LANES, (cb + 1) * LANES)
            p = pseg[:, c * LANES:(c + 1) * LANES]
            if cb in _ROPE_BLOCKS:
                p = p * cos + pltpu.roll(p, 8, 1) * sa + pltpu.roll(p, LANES - 8, 1) * sb
            o_ref[:, cols] = p.astype(o_ref.dtype)
            if _CB_MK <= cb < _CB_MV:
                kc = slice((cb - _CB_MK) * LANES, (cb - _CB_MK + 1) * LANES)
                for blk in range(tm // MOBA_BLOCK):
                    rows = p[blk * MOBA_BLOCK:(blk + 1) * MOBA_BLOCK, :]
                    km_ref[blk, :, kc] = jnp.sum(rows, axis=0, keepdims=True) * (1.0 / MOBA_BLOCK)


def _rope_tables(seq):
    half = ROT_DIM // 2
    inv_freq = ROPE_THETA ** (-jnp.arange(0, ROT_DIM, 2, dtype=_F32) / ROT_DIM)
    ang = jnp.arange(seq, dtype=_F32)[:, None] * inv_freq[None, :]
    cos, sin = jnp.cos(ang), jnp.sin(ang)
    lane = jnp.arange(LANES) % HEAD_DIM
    first, second = lane < half, (lane >= half) & (lane < ROT_DIM)
    idx = jnp.where(first, lane, jnp.where(second, lane - half, 0))
    cos_t = jnp.where((first | second)[None, :], cos[:, idx], 1.0)
    sa_t = jnp.where(second[None, :], sin[:, idx], 0.0)
    sb_t = jnp.where(first[None, :], -sin[:, idx], 0.0)
    return cos_t, sa_t, sb_t


def _inproj(x2d, w_in, seq, *, tm):
    t = x2d.shape[0]
    cos_t, sa_t, sb_t = _rope_tables(seq)
    n_pos = seq // tm
    tab_spec = pl.BlockSpec((tm, LANES), lambda i: (i % n_pos, 0))
    return pl.pallas_call(
        functools.partial(_inproj_kernel, tm=tm),
        out_shape=(jax.ShapeDtypeStruct((t, PROJ_WIDTH), _CDT),
                   jax.ShapeDtypeStruct((t // MOBA_BLOCK, 1, MOBA_WIDTH), _F32)),
        grid=(t // tm,),
        in_specs=[pl.BlockSpec((tm, D_MODEL), lambda i: (i, 0)),
                  pl.BlockSpec((D_MODEL, PROJ_WIDTH), lambda i: (0, 0)),
                  tab_spec, tab_spec, tab_spec],
        out_specs=(pl.BlockSpec((tm, PROJ_WIDTH), lambda i: (i, 0)),
                   pl.BlockSpec((tm // MOBA_BLOCK, 1, MOBA_WIDTH), lambda i: (i, 0, 0))),
        compiler_params=_cparams(("parallel",)),
        name="inproj",
    )(x2d, w_in, cos_t, sa_t, sb_t)


def _online_update(s, v, m, l, acc):
    m_new = jnp.maximum(m, jnp.max(s, axis=1, keepdims=True))
    alpha = jnp.exp(m - m_new)
    p = jnp.exp(s - m_new)
    l_new = alpha * l + jnp.sum(p, axis=1, keepdims=True)
    acc_new = alpha * acc + _dot(p.astype(v.dtype), v)
    return m_new, l_new, acc_new


def _causal_mask(tq, tk):
    row = lax.broadcasted_iota(jnp.int32, (tq, tk), 0)
    col = lax.broadcasted_iota(jnp.int32, (tq, tk), 1)
    return col <= row


def _diff_attn_kernel(q_ref, k_ref, v_ref, lam_ref, g_ref, o_ref, *, tq):
    qi = pl.program_id(2)
    q = q_ref[...]
    lane = lax.broadcasted_iota(jnp.int32, (tq, LANES), 1)
    zero = jnp.zeros_like(q)
    q1 = jnp.where(lane < HEAD_DIM, q, zero)
    q2 = jnp.where(lane >= HEAD_DIM, q, zero)

    def tile(j):
        start = pl.multiple_of(j * tq, tq)
        return k_ref[pl.ds(start, tq), :], v_ref[pl.ds(start, tq), :]

    kd, vd = tile(qi)
    mask = _causal_mask(tq, tq)
    init = (jnp.full((tq, 1), _NEG, _F32), jnp.zeros((tq, 1), _F32), jnp.zeros((tq, LANES), _F32))
    st1 = _online_update(jnp.where(mask, _dot_nt(q1, kd), _NEG), vd, *init)
    st2 = _online_update(jnp.where(mask, _dot_nt(q2, kd), _NEG), vd, *init)

    def body(j, carry):
        c1, c2 = carry
        kt, vt = tile(j)
        return (_online_update(_dot_nt(q1, kt), vt, *c1), _online_update(_dot_nt(q2, kt), vt, *c2))

    (m1, l1, a1), (m2, l2, a2) = lax.fori_loop(0, qi, body, (st1, st2))

    lv = lam_ref[...]
    lam = (jnp.exp(jnp.sum(lv[0:1] * lv[1:2], axis=1, keepdims=True))
           - jnp.exp(jnp.sum(lv[2:3] * lv[3:4], axis=1, keepdims=True)) + LAMBDA_INIT)
    o = a1 / l1 - lam * (a2 / l2)
    o = o * lax.rsqrt(jnp.mean(o * o, axis=1, keepdims=True) + RMS_EPS) * g_ref[...] * (1.0 - LAMBDA_INIT)
    o_ref[...] = o.astype(o_ref.dtype)


def _diff_attn(proj, lam_vecs, subln_g, batch, seq, *, tq):
    nq = seq // tq
    t = batch * seq
    kv_spec = lambda cb: pl.BlockSpec((seq, LANES), lambda b, h, i: (b, cb + h))
    return pl.pallas_call(
        functools.partial(_diff_attn_kernel, tq=tq),
        out_shape=jax.ShapeDtypeStruct((t, DIFF_WIDTH), _CDT),
        grid=(batch, DIFF_HEADS, nq),
        in_specs=[pl.BlockSpec((tq, LANES), lambda b, h, i: (b * nq + i, _CB_DQ + h)),
                  kv_spec(_CB_DK), kv_spec(_CB_DV),
                  pl.BlockSpec((4, HEAD_DIM), lambda b, h, i: (0, 0)),
                  pl.BlockSpec((1, 2 * HEAD_DIM), lambda b, h, i: (0, 0))],
        out_specs=pl.BlockSpec((tq, LANES), lambda b, h, i: (b * nq + i, h)),
        compiler_params=_cparams(("parallel", "parallel", "parallel")),
        name="diff_attn",
    )(proj, proj, proj, lam_vecs, subln_g)


def _moba_kernel(q_ref, k_ref, v_ref, km_ref, o_ref, *, nb):
    tq = MOBA_BLOCK
    qi = pl.program_id(2)
    q = q_ref[...]
    lane = lax.broadcasted_iota(jnp.int32, (tq, LANES), 1)
    zero = jnp.zeros_like(q)
    km = km_ref[...].astype(q.dtype)
    mask = _causal_mask(tq, tq)
    blk = lax.broadcasted_iota(jnp.int32, (nb, tq), 0)

    def tile(j):
        start = pl.multiple_of(j * tq, tq)
        return k_ref[pl.ds(start, tq), :], v_ref[pl.ds(start, tq), :]

    outs = []
    for hh in range(2):
        in_head = (lane >= hh * HEAD_DIM) & (lane < (hh + 1) * HEAD_DIM)
        qh = jnp.where(in_head, q, zero)
        gate = _dot_nt(km, qh)
        rank = jnp.zeros((nb, tq), _F32)
        for m_blk in range(nb):
            gm = gate[m_blk:m_blk + 1, :]
            beats = (gm > gate) | ((gm == gate) & (m_blk < blk))
            rank = rank + jnp.where(beats & (m_blk < qi), 1.0, 0.0)
        sel = jnp.where((blk < qi) & (rank < MOBA_TOPK), 1.0, 0.0)
        sel_pad = jnp.concatenate([sel, jnp.zeros((LANES - nb, tq), _F32)], axis=0)
        sel_q = sel_pad.T

        kd, vd = tile(qi)
        init = (jnp.full((tq, 1), _NEG, _F32), jnp.zeros((tq, 1), _F32), jnp.zeros((tq, LANES), _F32))
        st = _online_update(jnp.where(mask, _dot_nt(qh, kd), _NEG), vd, *init)

        def body(j, carry, qh=qh, sel_q=sel_q):
            kt, vt = tile(j)
            chosen = jnp.sum(jnp.where(lane == j, sel_q, 0.0), axis=1, keepdims=True) > 0.5
            return _online_update(jnp.where(chosen, _dot_nt(qh, kt), _NEG), vt, *carry)

        _, l, acc = lax.fori_loop(0, qi, body, st)
        outs.append(acc / l)
    o_ref[...] = jnp.where(lane < HEAD_DIM, outs[0], outs[1]).astype(o_ref.dtype)


def _moba_attn(proj, kmean, batch, seq):
    nb = seq // MOBA_BLOCK
    t = batch * seq
    kv_spec = lambda cb: pl.BlockSpec((seq, LANES), lambda b, p, i: (b, cb + p))
    return pl.pallas_call(
        functools.partial(_moba_kernel, nb=nb),
        out_shape=jax.ShapeDtypeStruct((t, MOBA_WIDTH), _CDT),
        grid=(batch, MOBA_HEADS // 2, nb),
        in_specs=[pl.BlockSpec((MOBA_BLOCK, LANES), lambda b, p, i: (b * nb + i, _CB_MQ + p)),
                  kv_spec(_CB_MK), kv_spec(_CB_MV),
                  pl.BlockSpec((None, nb, LANES), lambda b, p, i: (b, 0, p))],
        out_specs=pl.BlockSpec((MOBA_BLOCK, LANES), lambda b, p, i: (b * nb + i, p)),
        compiler_params=_cparams(("parallel", "parallel", "parallel")),
        name="moba_attn",
    )(proj, proj, proj, kmean)


def _mem_kv_kernel(mem_ref, g_ref, b_ref, w_ref, o_ref):
    mn = _layer_norm(mem_ref[...], g_ref[...], b_ref[...])
    o_ref[...] = _dot(mn.astype(_CDT), w_ref[...]).astype(o_ref.dtype)


def _mem_kv(mem2d, g, b, w_kv, *, tm):
    rows = mem2d.shape[0]
    vec = pl.BlockSpec((1, D_MODEL), lambda i: (0, 0))
    return pl.pallas_call(
        _mem_kv_kernel,
        out_shape=jax.ShapeDtypeStruct((rows, 2 * D_MODEL), _CDT),
        grid=(rows // tm,),
        in_specs=[pl.BlockSpec((tm, D_MODEL), lambda i: (i, 0)), vec, vec,
                  pl.BlockSpec((D_MODEL, 2 * D_MODEL), lambda i: (0, 0))],
        out_specs=pl.BlockSpec((tm, 2 * D_MODEL), lambda i: (i, 0)),
        compiler_params=_cparams(("parallel",)),
        name="mem_kv",
    )(mem2d, g, b, w_kv)


def _pack_bf16_pairs(h):
    half = h.shape[1] // 2
    hb = h.astype(jnp.bfloat16).astype(_F32)
    lo = lax.bitcast_convert_type(hb[:, :half], jnp.uint32) >> 16
    hi = lax.bitcast_convert_type(hb[:, half:], jnp.uint32) & jnp.uint32(0xFFFF0000)
    return hi | lo


def _unpack_bf16_pairs(w):
    lo = lax.bitcast_convert_type(w << 16, _F32)
    hi = lax.bitcast_convert_type(w & jnp.uint32(0xFFFF0000), _F32)
    return jnp.concatenate([lo, hi], axis=1)


def _post_attn_kernel(x_ref, oa_ref, ob_ref, wout_ref, g1_ref, b1_ref, kv_ref, wq_ref, wo_ref,
                      g2_ref, b2_ref, wr_ref, br_ref, h2_ref, h2p_ref, idx_ref, wt_ref, *, tm):
    mix = _dot(oa_ref[...], wout_ref[:DIFF_WIDTH, :]) + _dot(ob_ref[...], wout_ref[DIFF_WIDTH:, :])
    h1 = _layer_norm(DEEPNORM_ALPHA * x_ref[...] + mix, g1_ref[...], b1_ref[...])

    q = (_dot(h1.astype(_CDT), wq_ref[...]) * (MEM_HEAD_DIM ** -0.5)).astype(_CDT)
    heads = []
    for h in range(MEM_HEADS):
        c0 = h * MEM_HEAD_DIM
        kh = kv_ref[:, c0:c0 + MEM_HEAD_DIM]
        vh = kv_ref[:, D_MODEL + c0:D_MODEL + c0 + MEM_HEAD_DIM]
        s = _dot_nt(q[:, c0:c0 + MEM_HEAD_DIM], kh)
        p = jnp.exp(s - jnp.max(s, axis=1, keepdims=True))
        o = _dot(p.astype(_CDT), vh) / jnp.sum(p, axis=1, keepdims=True)
        heads.append(o.astype(_CDT))
    xatt = _dot(jnp.concatenate(heads, axis=1), wo_ref[...])
    h2 = _layer_norm(DEEPNORM_ALPHA * h1 + xatt, g2_ref[...], b2_ref[...])
    h2_ref[...] = h2
    h2p_ref[...] = _pack_bf16_pairs(h2)

    logits = _dot(h2.astype(_CDT), wr_ref[...]) + br_ref[...]
    lane = lax.broadcasted_iota(jnp.int32, (tm, LANES), 1)
    lane_f = lane.astype(_F32)
    vals, idxs = [], []
    work = logits
    for _ in range(TOP_K):
        m = jnp.max(work, axis=1, keepdims=True)
        ik = jnp.min(jnp.where(work == m, lane_f, float(LANES)), axis=1, keepdims=True)
        vals.append(m)
        idxs.append(ik)
        work = jnp.where(lane_f == ik, -jnp.inf, work)
    exps = [jnp.exp(v - vals[0]) for v in vals]
    denom = exps[0] + exps[1] + exps[2] + exps[3]
    idx_w = jnp.zeros((tm, LANES), _F32)
    wt_w = jnp.zeros((tm, LANES), _F32)
    for k in range(TOP_K):
        idx_w = jnp.where(lane == k, idxs[k], idx_w)
        wt_w = jnp.where(lane == k, exps[k] / denom, wt_w)
    idx_ref[...] = idx_w[:, :TOP_K].astype(jnp.int32)
    wt_ref[...] = wt_w[:, :TOP_K]


def _post_attn(x2d, out_a, out_b, w_out, g1, b1, kv, w_q, w_o, g2, b2, w_r, b_r, seq, mem_len, *, tm):
    t = x2d.shape[0]
    per_b = seq // tm
    full = lambda shape: pl.BlockSpec(shape, lambda i: (0,) * len(shape))
    vec = full((1, D_MODEL))
    return pl.pallas_call(
        functools.partial(_post_attn_kernel, tm=tm),
        out_shape=(jax.ShapeDtypeStruct((t, D_MODEL), _F32),
                   jax.ShapeDtypeStruct((t, D_MODEL // 2), jnp.uint32),
                   jax.ShapeDtypeStruct((t, TOP_K), jnp.int32),
                   jax.ShapeDtypeStruct((t, TOP_K), _F32)),
        grid=(t // tm,),
        in_specs=[pl.BlockSpec((tm, D_MODEL), lambda i: (i, 0)),
                  pl.BlockSpec((tm, DIFF_WIDTH), lambda i: (i, 0)),
                  pl.BlockSpec((tm, MOBA_WIDTH), lambda i: (i, 0)),
                  full((D_MODEL, D_MODEL)), vec, vec,
                  pl.BlockSpec((mem_len, 2 * D_MODEL), lambda i: (i // per_b, 0)),
                  full((D_MODEL, D_MODEL)), full((D_MODEL, D_MODEL)), vec, vec,
                  full((D_MODEL, LANES)), full((1, LANES))],
        out_specs=(pl.BlockSpec((tm, D_MODEL), lambda i: (i, 0)),
                   pl.BlockSpec((tm, D_MODEL // 2), lambda i: (i, 0)),
                   pl.BlockSpec((tm, TOP_K), lambda i: (i, 0)),
                   pl.BlockSpec((tm, TOP_K), lambda i: (i, 0))),
        compiler_params=_cparams(("parallel",)),
        name="post_attn",
    )(x2d, out_a, out_b, w_out, g1, b1, kv, w_q, w_o, g2, b2, w_r, b_r)


def _rank_kernel(idx_ref, rank_ref, cnt_ref, carry_ref, *, tm):
    @pl.when(pl.program_id(0) == 0)
    def _init():
        carry_ref[...] = jnp.zeros_like(carry_ref)

    idx = idx_ref[...]
    lane = lax.broadcasted_iota(jnp.int32, (tm, LANES), 1)
    hits = [lane == idx[:, k:k + 1] for k in range(TOP_K)]
    sel = jnp.zeros((tm, LANES), _F32)
    for hit in hits:
        sel = sel + jnp.where(hit, 1.0, 0.0)
    row = lax.broadcasted_iota(jnp.int32, (tm, tm), 0)
    col = lax.broadcasted_iota(jnp.int32, (tm, tm), 1)
    earlier = jnp.where(col < row, 1.0, 0.0).astype(jnp.bfloat16)
    dense = _dot(earlier, sel.astype(jnp.bfloat16)) + carry_ref[...]
    rank_w = jnp.zeros((tm, LANES), _F32)
    for k, hit in enumerate(hits):
        rk = jnp.sum(jnp.where(hit, dense, 0.0), axis=1, keepdims=True)
        rank_w = jnp.where(lane == k, rk, rank_w)
    rank_ref[...] = rank_w[:, :TOP_K].astype(jnp.int32)
    carry_ref[...] = carry_ref[...] + jnp.sum(sel, axis=0, keepdims=True)
    cnt_ref[...] = carry_ref[...].astype(jnp.int32)


def _rank(idx, *, tm):
    t = idx.shape[0]
    return pl.pallas_call(
        functools.partial(_rank_kernel, tm=tm),
        out_shape=(jax.ShapeDtypeStruct((t, TOP_K), jnp.int32),
                   jax.ShapeDtypeStruct((1, LANES), jnp.int32)),
        grid=(t // tm,),
        in_specs=[pl.BlockSpec((tm, TOP_K), lambda i: (i, 0))],
        out_specs=(pl.BlockSpec((tm, TOP_K), lambda i: (i, 0)),
                   pl.BlockSpec((1, LANES), lambda i: (0, 0))),
        scratch_shapes=[pltpu.VMEM((1, LANES), _F32)],
        compiler_params=_cparams(("arbitrary",)),
        name="rank",
    )(idx)


def _dispatch_kernel(meta_ref, h_ref, idx_ref, rank_ref, xs_ref, pos_ref, zero_ref, sem, zsem, *, tm):
    n_rows = tm * TOP_K

    def zero_copy(r):
        return pltpu.make_async_copy(zero_ref.at[pl.ds(0, 1)], xs_ref.at[pl.ds(r, 1)], zsem)

    @pl.when(pl.program_id(0) == 0)
    def _zero_fill():
        zero_ref[...] = jnp.zeros_like(zero_ref)

        def per_expert(e, c):
            lo = meta_ref[e] + meta_ref[N_EXPERTS + e]
            hi = meta_ref[e] + meta_ref[2 * N_EXPERTS + e]
            lax.fori_loop(lo, hi, lambda r, c2: (zero_copy(r).start(), c2)[1], 0)
            lax.fori_loop(lo, hi, lambda r, c2: (zero_copy(r).wait(), c2)[1], 0)
            return c

        lax.fori_loop(0, N_EXPERTS, per_expert, 0)

    def row_copy(r):
        return pltpu.make_async_copy(h_ref.at[pl.ds(lax.shift_right_logical(r, 2), 1)],
                                     xs_ref.at[pl.ds(pos_ref[r], 1)], sem)

    def issue(r, c):
        pos_ref[r] = meta_ref[idx_ref[r]] + rank_ref[r]
        row_copy(r).start()
        return c

    lax.fori_loop(0, n_rows, issue, 0)
    lax.fori_loop(0, n_rows, lambda r, c: (row_copy(r).wait(), c)[1], 0)


def _dispatch(meta, h2p, idx_flat, rank_flat, n_sorted_rows, *, tm):
    t = h2p.shape[0]
    n_rows = tm * TOP_K
    smem_blk = pl.BlockSpec((n_rows,), lambda i, meta: (i,), memory_space=pltpu.SMEM)
    return pl.pallas_call(
        functools.partial(_dispatch_kernel, tm=tm),
        out_shape=(jax.ShapeDtypeStruct((n_sorted_rows, D_MODEL // 2), jnp.uint32),
                   jax.ShapeDtypeStruct((t * TOP_K,), jnp.int32)),
        grid_spec=pltpu.PrefetchScalarGridSpec(
            num_scalar_prefetch=1,
            grid=(t // tm,),
            in_specs=[pl.BlockSpec((tm, D_MODEL // 2), lambda i, meta: (i, 0)), smem_blk, smem_blk],
            out_specs=(pl.BlockSpec(memory_space=pl.ANY), smem_blk),
            scratch_shapes=[pltpu.VMEM((8, D_MODEL // 2), jnp.uint32),
                            pltpu.SemaphoreType.DMA, pltpu.SemaphoreType.DMA]),
        compiler_params=_cparams(("arbitrary",)),
        name="dispatch",
    )(meta, h2p, idx_flat, rank_flat)


def _experts_kernel(te_ref, nv_ref, xs_ref, w1g_ref, w1l_ref, b1g_ref, b1l_ref, w2_ref, b2_ref, y_ref):
    live = pl.program_id(0) < nv_ref[0]

    @pl.when(live)
    def _mlp():
        x = _unpack_bf16_pairs(xs_ref[...]).astype(_CDT)
        glu = jnp.minimum(_dot(x, w1g_ref[0]) + b1g_ref[0], SWIGLU_LIMIT)
        lin = jnp.clip(_dot(x, w1l_ref[0]) + b1l_ref[0], -SWIGLU_LIMIT, SWIGLU_LIMIT)
        act = glu * jax.nn.sigmoid(SWIGLU_ALPHA * glu) * (lin + 1.0)
        y_ref[...] = _dot(act.astype(_CDT), w2_ref[0]) + b2_ref[0]

    @pl.when(jnp.logical_not(live))
    def _unused_tile():
        y_ref[...] = jnp.zeros_like(y_ref)


def _experts(tile_expert, n_valid, xs, w1g, w1l, b1g, b1l, w2, b2, *, tm):
    n_tiles = xs.shape[0] // tm
    by_expert = lambda shape: pl.BlockSpec((1,) + shape, lambda i, te, nv: (te[i], 0, 0))
    return pl.pallas_call(
        _experts_kernel,
        out_shape=jax.ShapeDtypeStruct((xs.shape[0], D_MODEL), _F32),
        grid_spec=pltpu.PrefetchScalarGridSpec(
            num_scalar_prefetch=2,
            grid=(n_tiles,),
            in_specs=[pl.BlockSpec((tm, D_MODEL // 2), lambda i, te, nv: (jnp.minimum(i, nv[0] - 1), 0)),
                      by_expert((D_MODEL, D_FF)), by_expert((D_MODEL, D_FF)),
                      by_expert((1, D_FF)), by_expert((1, D_FF)),
                      by_expert((D_FF, D_MODEL)), by_expert((1, D_MODEL))],
            out_specs=pl.BlockSpec((tm, D_MODEL), lambda i, te, nv: (i, 0))),
        compiler_params=_cparams(("arbitrary",)),
        name="experts",
    )(tile_expert, n_valid, xs, w1g, w1l, b1g, b1l, w2, b2)


def _combine_kernel(pos_ref, wt_ref, h2_ref, g_ref, b_ref, ys_ref, o_ref, buf_ref, sem, *, tm):
    n_rows = tm * TOP_K

    def row_copy(r):
        return pltpu.make_async_copy(ys_ref.at[pl.ds(pos_ref[r], 1)],
                                     buf_ref.at[r & (TOP_K - 1), pl.ds(lax.shift_right_logical(r, 2), 1)], sem)

    lax.fori_loop(0, n_rows, lambda r, c: (row_copy(r).start(), c)[1], 0)
    lax.fori_loop(0, n_rows, lambda r, c: (row_copy(r).wait(), c)[1], 0)
    wt = wt_ref[...]
    ffn = wt[:, 0:1] * buf_ref[0]
    for k in range(1, TOP_K):
        ffn = ffn + wt[:, k:k + 1] * buf_ref[k]
    o_ref[...] = _layer_norm(DEEPNORM_ALPHA * h2_ref[...] + ffn, g_ref[...], b_ref[...])


def _combine(pos_flat, wts, h2, g3, b3, ys, *, tm):
    t = h2.shape[0]
    vec = pl.BlockSpec((1, D_MODEL), lambda i: (0, 0))
    return pl.pallas_call(
        functools.partial(_combine_kernel, tm=tm),
        out_shape=jax.ShapeDtypeStruct((t, D_MODEL), _F32),
        grid=(t // tm,),
        in_specs=[pl.BlockSpec((tm * TOP_K,), lambda i: (i,), memory_space=pltpu.SMEM),
                  pl.BlockSpec((tm, TOP_K), lambda i: (i, 0)),
                  pl.BlockSpec((tm, D_MODEL), lambda i: (i, 0)), vec, vec,
                  pl.BlockSpec(memory_space=pl.ANY)],
        out_specs=pl.BlockSpec((tm, D_MODEL), lambda i: (i, 0)),
        scratch_shapes=[pltpu.VMEM((TOP_K, tm, D_MODEL), _F32), pltpu.SemaphoreType.DMA],
        compiler_params=_cparams(("arbitrary",)),
        name="combine",
    )(pos_flat, wts, h2, g3, b3, ys)


def _tiles(seq):
    big = 512 if seq % 512 == 0 else MOBA_BLOCK
    return dict(inproj=big, attn_q=MOBA_BLOCK, post=big if big < 512 else 256, rank=big, dispatch=big,
                expert=512, combine=MOBA_BLOCK)


def kernel(x, mem, w_in, diff_lambda_q1, diff_lambda_k1, diff_lambda_q2, diff_lambda_k2, diff_subln_g,
           w_mix_out, ln1_g, ln1_b, mem_ln_g, mem_ln_b, w_mem_q, w_mem_kv, w_mem_o, ln2_g, ln2_b, w_router,
           b_router, w_mlp1, b_mlp1, w_mlp2, b_mlp2, ln3_g, ln3_b):
    batch, seq, d = x.shape
    mem_len = mem.shape[1]
    assert d == D_MODEL and seq % MOBA_BLOCK == 0 and w_in.shape[0] == 1
    t = batch * seq
    ts = _tiles(seq)
    row = lambda v: v.reshape(1, -1).astype(_F32)

    col = jnp.arange(PROJ_WIDTH)
    is_q = (col < DIFF_WIDTH) | ((col >= 3 * DIFF_WIDTH) & (col < 3 * DIFF_WIDTH + MOBA_WIDTH))
    w_in_c = (w_in[0] * jnp.where(is_q, HEAD_DIM ** -0.5, 1.0)[None, :]).astype(_CDT)
    lam_vecs = jnp.stack([diff_lambda_q1[0], diff_lambda_k1[0], diff_lambda_q2[0], diff_lambda_k2[0]]).astype(_F32)
    w_r = jnp.zeros((D_MODEL, LANES), _F32).at[:, :N_EXPERTS].set(w_router[0]).astype(_CDT)
    b_r = jnp.full((1, LANES), _NEG, _F32).at[0, :N_EXPERTS].set(b_router[0])
    w1g = w_mlp1[0][:, :, 0::2].astype(_CDT)
    w1l = w_mlp1[0][:, :, 1::2].astype(_CDT)
    b1g = b_mlp1[0][:, None, 0::2].astype(_F32)
    b1l = b_mlp1[0][:, None, 1::2].astype(_F32)
    w2 = w_mlp2[0].astype(_CDT)
    b2 = b_mlp2[0][:, None, :].astype(_F32)

    x2d = x.reshape(t, d)
    proj, kmean = _inproj(x2d, w_in_c, seq, tm=ts["inproj"])
    kmean = kmean.reshape(batch, seq // MOBA_BLOCK, MOBA_WIDTH)
    out_a = _diff_attn(proj, lam_vecs, row(diff_subln_g[0]), batch, seq, tq=ts["attn_q"])
    out_b = _moba_attn(proj, kmean, batch, seq)
    kv = _mem_kv(mem.reshape(batch * mem_len, d), row(mem_ln_g), row(mem_ln_b), w_mem_kv[0].astype(_CDT),
                 tm=mem_len)
    h2, h2p, top_idx, top_w = _post_attn(
        x2d, out_a, out_b, w_mix_out[0].astype(_CDT), row(ln1_g[0]), row(ln1_b[0]), kv,
        w_mem_q[0].astype(_CDT), w_mem_o[0].astype(_CDT), row(ln2_g[0]), row(ln2_b[0]), w_r, b_r,
        seq, mem_len, tm=ts["post"])

    rank4, counts = _rank(top_idx, tm=ts["rank"])
    tmx = ts["expert"]
    cnt = counts[0, :N_EXPERTS]
    padded = (cnt + tmx - 1) // tmx * tmx
    ends = jnp.cumsum(padded)
    meta = jnp.concatenate([ends - padded, cnt, padded]).astype(jnp.int32)
    n_sorted_rows = t * TOP_K + N_EXPERTS * tmx
    n_tiles = n_sorted_rows // tmx
    tile_expert = jnp.minimum(jnp.searchsorted(ends, jnp.arange(n_tiles, dtype=jnp.int32) * tmx, side="right"),
                              N_EXPERTS - 1).astype(jnp.int32)
    n_valid = (ends[-1:] // tmx).astype(jnp.int32)

    xs, pos = _dispatch(meta, h2p, top_idx.reshape(-1), rank4.reshape(-1), n_sorted_rows, tm=ts["dispatch"])
    ys = _experts(tile_expert, n_valid, xs, w1g, w1l, b1g, b1l, w2, b2, tm=tmx)
    out = _combine(pos, top_w, h2, row(ln3_g[0]), row(ln3_b[0]), ys, tm=ts["combine"])
    return out.reshape(batch, seq, d)
```

```python
import functools
import math

import jax
import jax.numpy as jnp
from jax import lax
from jax.experimental import pallas as pl
from jax.experimental.pallas import tpu as pltpu

D_MODEL = 1024
DIFF_HEADS = 4
HEAD_DIM = 64
DIFF_WIDTH = DIFF_HEADS * 2 * HEAD_DIM
MOBA_HEADS = 8
MOBA_WIDTH = MOBA_HEADS * HEAD_DIM
MOBA_BLOCK = 256
MOBA_TOPK = 3
ROPE_THETA = 500000.0
ROT_DIM = HEAD_DIM // 4
MEM_HEADS = 4
MEM_HEAD_DIM = D_MODEL // MEM_HEADS
N_EXPERTS = 32
TOP_K = 4
D_FF = D_MODEL
SWIGLU_ALPHA = 1.702
SWIGLU_LIMIT = 7.0
LN_EPS = 1e-5
RMS_EPS = 1e-5
DEEPNORM_ALPHA = 2.0 ** 0.25
LAMBDA_INIT = 0.8 - 0.6 * math.exp(0.0)

LANES = 128
V7X_VMEM_LIMIT = 56 * 1024 * 1024

_CDT = jnp.bfloat16
_F32 = jnp.float32
_NEG = -1e30

_CB_DQ, _CB_DK, _CB_DV = 0, 4, 8
_CB_MQ, _CB_MK, _CB_MV = 12, 16, 20
PROJ_WIDTH = 3 * (DIFF_WIDTH + MOBA_WIDTH)


def _cparams(sem, vmem=V7X_VMEM_LIMIT):
    return pltpu.CompilerParams(dimension_semantics=sem, vmem_limit_bytes=vmem)


def _dot(a, b):
    return jnp.dot(a, b, preferred_element_type=_F32)


def _dot_nt(a, b):
    return lax.dot_general(a, b, (((1,), (1,)), ((), ())), preferred_element_type=_F32)


def _layer_norm(x, g, b):
    mu = jnp.mean(x, axis=-1, keepdims=True)
    xc = x - mu
    var = jnp.mean(xc * xc, axis=-1, keepdims=True)
    return xc * lax.rsqrt(var + LN_EPS) * g + b


_ROPE_BLOCKS = tuple(range(_CB_DQ, _CB_DV)) + tuple(range(_CB_MQ, _CB_MV))


def _inproj_kernel(x_ref, w_ref, cos_ref, sa_ref, sb_ref, o_ref, km_ref, *, tm):
    x = x_ref[...].astype(_CDT)
    cos, sa, sb = cos_ref[...], sa_ref[...], sb_ref[...]
    seg_w = 4 * LANES
    for seg in range(PROJ_WIDTH // seg_w):
        pseg = _dot(x, w_ref[:, seg * seg_w:(seg + 1) * seg_w])
        for c in range(4):
            cb = seg * 4 + c
            cols = slice(cb * LANES, (cb + 1) * LANES)
            p = pseg[:, c * LANES:(c + 1) * LANES]
            if cb in _ROPE_BLOCKS:
                p = p * cos + pltpu.roll(p, 8, 1) * sa + pltpu.roll(p, LANES - 8, 1) * sb
            o_ref[:, cols] = p.astype(o_ref.dtype)
            if _CB_MK <= cb < _CB_MV:
                kc = slice((cb - _CB_MK) * LANES, (cb - _CB_MK + 1) * LANES)
                for blk in range(tm // MOBA_BLOCK):
                    rows = p[blk * MOBA_BLOCK:(blk + 1) * MOBA_BLOCK, :]
                    km_ref[blk, :, kc] = jnp.sum(rows, axis=0, keepdims=True) * (1.0 / MOBA_BLOCK)


def _rope_tables(seq):
    half = ROT_DIM // 2
    inv_freq = ROPE_THETA ** (-jnp.arange(0, ROT_DIM, 2, dtype=_F32) / ROT_DIM)
    ang = jnp.arange(seq, dtype=_F32)[:, None] * inv_freq[None, :]
    cos, sin = jnp.cos(ang), jnp.sin(ang)
    lane = jnp.arange(LANES) % HEAD_DIM
    first, second = lane < half, (lane >= half) & (lane < ROT_DIM)
    idx = jnp.where(first, lane, jnp.where(second, lane - half, 0))
    cos_t = jnp.where((first | second)[None, :], cos[:, idx], 1.0)
    sa_t = jnp.where(second[None, :], sin[:, idx], 0.0)
    sb_t = jnp.where(first[None, :], -sin[:, idx], 0.0)
    return cos_t, sa_t, sb_t


def _inproj(x2d, w_in, seq, *, tm):
    t = x2d.shape[0]
    cos_t, sa_t, sb_t = _rope_tables(seq)
    n_pos = seq // tm
    tab_spec = pl.BlockSpec((tm, LANES), lambda i: (i % n_pos, 0))
    return pl.pallas_call(
        functools.partial(_inproj_kernel, tm=tm),
        out_shape=(jax.ShapeDtypeStruct((t, PROJ_WIDTH), _CDT),
                   jax.ShapeDtypeStruct((t // MOBA_BLOCK, 1, MOBA_WIDTH), _F32)),
        grid=(t // tm,),
        in_specs=[pl.BlockSpec((tm, D_MODEL), lambda i: (i, 0)),
                  pl.BlockSpec((D_MODEL, PROJ_WIDTH), lambda i: (0, 0)),
                  tab_spec, tab_spec, tab_spec],
        out_specs=(pl.BlockSpec((tm, PROJ_WIDTH), lambda i: (i, 0)),
                   pl.BlockSpec((tm // MOBA_BLOCK, 1, MOBA_WIDTH), lambda i: (i, 0, 0))),
        compiler_params=_cparams(("parallel",)),
        name="inproj",
    )(x2d, w_in, cos_t, sa_t, sb_t)


def _softmax_parts(parts):
    m = functools.reduce(jnp.maximum, [jnp.max(s, axis=1, keepdims=True) for s in parts])
    es = [jnp.exp(s - m) for s in parts]
    l = functools.reduce(jnp.add, [jnp.sum(e, axis=1, keepdims=True) for e in es])
    return es, l


def _causal_mask(tq):
    row = lax.broadcasted_iota(jnp.int32, (tq, tq), 0)
    col = lax.broadcasted_iota(jnp.int32, (tq, tq), 1)
    return col <= row


def _diff_attn_kernel(q_ref, k_ref, v_ref, lam_ref, g_ref, o_ref, *, seq, tq):
    lane = lax.broadcasted_iota(jnp.int32, (tq, LANES), 1)
    mask = _causal_mask(tq)
    lv = lam_ref[...]
    lam = (jnp.exp(jnp.sum(lv[0:1] * lv[1:2], axis=1, keepdims=True))
           - jnp.exp(jnp.sum(lv[2:3] * lv[3:4], axis=1, keepdims=True)) + LAMBDA_INIT)
    gain = g_ref[...] * (1.0 - LAMBDA_INIT)
    for i in range(seq // tq):
        rows = slice(i * tq, (i + 1) * tq)
        q = q_ref[rows, :]
        zero = jnp.zeros_like(q)
        q1 = jnp.where(lane < HEAD_DIM, q, zero)
        q2 = jnp.where(lane >= HEAD_DIM, q, zero)
        kd, vd = k_ref[rows, :], v_ref[rows, :]
        s1 = [jnp.where(mask, _dot_nt(q1, kd), _NEG)]
        s2 = [jnp.where(mask, _dot_nt(q2, kd), _NEG)]
        if i:
            kp = k_ref[:i * tq, :]
            s1.append(_dot_nt(q1, kp))
            s2.append(_dot_nt(q2, kp))
        e1, l1 = _softmax_parts(s1)
        e2, l2 = _softmax_parts(s2)
        c1, c2 = 1.0 / l1, lam / l2
        a = [(x * c1 - y * c2).astype(_CDT) for x, y in zip(e1, e2)]
        o = _dot(a[0], vd)
        if i:
            o = o + _dot(a[1], v_ref[:i * tq, :])
        o = o * lax.rsqrt(jnp.mean(o * o, axis=1, keepdims=True) + RMS_EPS) * gain
        o_ref[rows, :] = o.astype(o_ref.dtype)


def _diff_attn(proj, lam_vecs, subln_g, batch, seq, *, tq):
    t = batch * seq
    blk = lambda cb: pl.BlockSpec((seq, LANES), lambda b, h: (b, cb + h))
    return pl.pallas_call(
        functools.partial(_diff_attn_kernel, seq=seq, tq=tq),
        out_shape=jax.ShapeDtypeStruct((t, DIFF_WIDTH), _CDT),
        grid=(batch, DIFF_HEADS),
        in_specs=[blk(_CB_DQ), blk(_CB_DK), blk(_CB_DV),
                  pl.BlockSpec((4, HEAD_DIM), lambda b, h: (0, 0)),
                  pl.BlockSpec((1, 2 * HEAD_DIM), lambda b, h: (0, 0))],
        out_specs=pl.BlockSpec((seq, LANES), lambda b, h: (b, h)),
        compiler_params=_cparams(("parallel", "parallel")),
        name="diff_attn",
    )(proj, proj, proj, lam_vecs, subln_g)


def _moba_selection(gate, n_past):
    blk = lax.broadcasted_iota(jnp.int32, gate.shape, 0)
    rank = jnp.zeros(gate.shape, _F32)
    for m_blk in range(n_past):
        gm = gate[m_blk:m_blk + 1, :]
        beats = (gm > gate) | ((gm == gate) & (m_blk < blk))
        rank = rank + jnp.where(beats, 1.0, 0.0)
    return jnp.where((blk < n_past) & (rank < MOBA_TOPK), 1.0, 0.0)


def _moba_kernel(q_ref, k_ref, v_ref, km_ref, o_ref, *, nb):
    tq = MOBA_BLOCK
    lane = lax.broadcasted_iota(jnp.int32, (tq, LANES), 1)
    mask = _causal_mask(tq)
    km = km_ref[...].astype(_CDT)
    for i in range(nb):
        rows = slice(i * tq, (i + 1) * tq)
        q = q_ref[rows, :]
        zero = jnp.zeros_like(q)
        kd, vd = k_ref[rows, :], v_ref[rows, :]
        gated = i > MOBA_TOPK
        outs = []
        for hh in range(2):
            in_head = (lane >= hh * HEAD_DIM) & (lane < (hh + 1) * HEAD_DIM)
            qh = jnp.where(in_head, q, zero)
            parts = [jnp.where(mask, _dot_nt(qh, kd), _NEG)]
            if gated:
                sel = _moba_selection(_dot_nt(km, qh), i)
                sel_q = jnp.concatenate([sel, jnp.zeros((LANES - nb, tq), _F32)], axis=0).T
            for j in range(i):
                s = _dot_nt(qh, k_ref[j * tq:(j + 1) * tq, :])
                parts.append(jnp.where(sel_q[:, j:j + 1] > 0.5, s, _NEG) if gated else s)
            es, l = _softmax_parts(parts)
            acc = _dot(es[0].astype(_CDT), vd)
            for j in range(i):
                acc = acc + _dot(es[j + 1].astype(_CDT), v_ref[j * tq:(j + 1) * tq, :])
            outs.append(acc / l)
        o_ref[rows, :] = jnp.where(lane < HEAD_DIM, outs[0], outs[1]).astype(o_ref.dtype)


def _moba_attn(proj, kmean, batch, seq):
    nb = seq // MOBA_BLOCK
    t = batch * seq
    blk = lambda cb: pl.BlockSpec((seq, LANES), lambda b, p: (b, cb + p))
    return pl.pallas_call(
        functools.partial(_moba_kernel, nb=nb),
        out_shape=jax.ShapeDtypeStruct((t, MOBA_WIDTH), _CDT),
        grid=(batch, MOBA_HEADS // 2),
        in_specs=[blk(_CB_MQ), blk(_CB_MK), blk(_CB_MV),
                  pl.BlockSpec((None, nb, LANES), lambda b, p: (b, 0, p))],
        out_specs=pl.BlockSpec((seq, LANES), lambda b, p: (b, p)),
        compiler_params=_cparams(("parallel", "parallel")),
        name="moba_attn",
    )(proj, proj, proj, kmean)


def _mem_kv_kernel(mem_ref, g_ref, b_ref, w_ref, o_ref):
    mn = _layer_norm(mem_ref[...], g_ref[...], b_ref[...])
    o_ref[...] = _dot(mn.astype(_CDT), w_ref[...]).astype(o_ref.dtype)


def _mem_kv(mem2d, g, b, w_kv, *, tm):
    rows = mem2d.shape[0]
    vec = pl.BlockSpec((1, D_MODEL), lambda i: (0, 0))
    return pl.pallas_call(
        _mem_kv_kernel,
        out_shape=jax.ShapeDtypeStruct((rows, 2 * D_MODEL), _CDT),
        grid=(rows // tm,),
        in_specs=[pl.BlockSpec((tm, D_MODEL), lambda i: (i, 0)), vec, vec,
                  pl.BlockSpec((D_MODEL, 2 * D_MODEL), lambda i: (0, 0))],
        out_specs=pl.BlockSpec((tm, 2 * D_MODEL), lambda i: (i, 0)),
        compiler_params=_cparams(("parallel",)),
        name="mem_kv",
    )(mem2d, g, b, w_kv)


def _pack_bf16_pairs(h):
    half = h.shape[1] // 2
    hb = h.astype(jnp.bfloat16).astype(_F32)
    lo = lax.bitcast_convert_type(hb[:, :half], jnp.uint32) >> 16
    hi = lax.bitcast_convert_type(hb[:, half:], jnp.uint32) & jnp.uint32(0xFFFF0000)
    return hi | lo


def _unpack_bf16_pairs(w):
    lo = lax.bitcast_convert_type(w << 16, _F32)
    hi = lax.bitcast_convert_type(w & jnp.uint32(0xFFFF0000), _F32)
    return jnp.concatenate([lo, hi], axis=1)


def _post_attn_kernel(x_ref, oa_ref, ob_ref, wout_ref, g1_ref, b1_ref, kv_ref, wq_ref, wo_ref,
                      g2_ref, b2_ref, wr_ref, br_ref, h2_ref, h2p_ref, idx_ref, wt_ref, *, tm):
    mix = _dot(oa_ref[...], wout_ref[:DIFF_WIDTH, :]) + _dot(ob_ref[...], wout_ref[DIFF_WIDTH:, :])
    h1 = _layer_norm(DEEPNORM_ALPHA * x_ref[...] + mix, g1_ref[...], b1_ref[...])

    q = (_dot(h1.astype(_CDT), wq_ref[...]) * (MEM_HEAD_DIM ** -0.5)).astype(_CDT)
    heads = []
    for h in range(MEM_HEADS):
        c0 = h * MEM_HEAD_DIM
        kh = kv_ref[:, c0:c0 + MEM_HEAD_DIM]
        vh = kv_ref[:, D_MODEL + c0:D_MODEL + c0 + MEM_HEAD_DIM]
        s = _dot_nt(q[:, c0:c0 + MEM_HEAD_DIM], kh)
        p = jnp.exp(s - jnp.max(s, axis=1, keepdims=True))
        o = _dot(p.astype(_CDT), vh) / jnp.sum(p, axis=1, keepdims=True)
        heads.append(o.astype(_CDT))
    xatt = _dot(jnp.concatenate(heads, axis=1), wo_ref[...])
    h2 = _layer_norm(DEEPNORM_ALPHA * h1 + xatt, g2_ref[...], b2_ref[...])
    h2_ref[...] = h2
    h2p_ref[...] = _pack_bf16_pairs(h2)

    logits = _dot(h2.astype(_CDT), wr_ref[...]) + br_ref[...]
    lane = lax.broadcasted_iota(jnp.int32, (tm, LANES), 1)
    lane_f = lane.astype(_F32)
    vals, idxs = [], []
    work = logits
    for _ in range(TOP_K):
        m = jnp.max(work, axis=1, keepdims=True)
        ik = jnp.min(jnp.where(work == m, lane_f, float(LANES)), axis=1, keepdims=True)
        vals.append(m)
        idxs.append(ik)
        work = jnp.where(lane_f == ik, -jnp.inf, work)
    exps = [jnp.exp(v - vals[0]) for v in vals]
    denom = exps[0] + exps[1] + exps[2] + exps[3]
    idx_w = jnp.zeros((tm, LANES), _F32)
    wt_w = jnp.zeros((tm, LANES), _F32)
    for k in range(TOP_K):
        idx_w = jnp.where(lane == k, idxs[k], idx_w)
        wt_w = jnp.where(lane == k, exps[k] / denom, wt_w)
    idx_ref[...] = idx_w[:, :TOP_K].astype(jnp.int32)
    wt_ref[...] = wt_w[:, :TOP_K]


def _post_attn(x2d, out_a, out_b, w_out, g1, b1, kv, w_q, w_o, g2, b2, w_r, b_r, seq, mem_len, *, tm):
    t = x2d.shape[0]
    per_b = seq // tm
    full = lambda shape: pl.BlockSpec(shape, lambda i: (0,) * len(shape))
    vec = full((1, D_MODEL))
    return pl.pallas_call(
        functools.partial(_post_attn_kernel, tm=tm),
        out_shape=(jax.ShapeDtypeStruct((t, D_MODEL), _F32),
                   jax.ShapeDtypeStruct((t, D_MODEL // 2), jnp.uint32),
                   jax.ShapeDtypeStruct((t, TOP_K), jnp.int32),
                   jax.ShapeDtypeStruct((t, TOP_K), _F32)),
        grid=(t // tm,),
        in_specs=[pl.BlockSpec((tm, D_MODEL), lambda i: (i, 0)),
                  pl.BlockSpec((tm, DIFF_WIDTH), lambda i: (i, 0)),
                  pl.BlockSpec((tm, MOBA_WIDTH), lambda i: (i, 0)),
                  full((D_MODEL, D_MODEL)), vec, vec,
                  pl.BlockSpec((mem_len, 2 * D_MODEL), lambda i: (i // per_b, 0)),
                  full((D_MODEL, D_MODEL)), full((D_MODEL, D_MODEL)), vec, vec,
                  full((D_MODEL, LANES)), full((1, LANES))],
        out_specs=(pl.BlockSpec((tm, D_MODEL), lambda i: (i, 0)),
                   pl.BlockSpec((tm, D_MODEL // 2), lambda i: (i, 0)),
                   pl.BlockSpec((tm, TOP_K), lambda i: (i, 0)),
                   pl.BlockSpec((tm, TOP_K), lambda i: (i, 0))),
        compiler_params=_cparams(("parallel",)),
        name="post_attn",
    )(x2d, out_a, out_b, w_out, g1, b1, kv, w_q, w_o, g2, b2, w_r, b_r)


def _rank_kernel(idx_ref, rank_ref, cnt_ref, carry_ref, *, tm):
    @pl.when(pl.program_id(0) == 0)
    def _init():
        carry_ref[...] = jnp.zeros_like(carry_ref)

    idx = idx_ref[...]
    lane = lax.broadcasted_iota(jnp.int32, (tm, LANES), 1)
    hits = [lane == idx[:, k:k + 1] for k in range(TOP_K)]
    sel = jnp.zeros((tm, LANES), _F32)
    for hit in hits:
        sel = sel + jnp.where(hit, 1.0, 0.0)
    row = lax.broadcasted_iota(jnp.int32, (tm, tm), 0)
    col = lax.broadcasted_iota(jnp.int32, (tm, tm), 1)
    earlier = jnp.where(col < row, 1.0, 0.0).astype(jnp.bfloat16)
    dense = _dot(earlier, sel.astype(jnp.bfloat16)) + carry_ref[...]
    rank_w = jnp.zeros((tm, LANES), _F32)
    for k, hit in enumerate(hits):
        rk = jnp.sum(jnp.where(hit, dense, 0.0), axis=1, keepdims=True)
        rank_w = jnp.where(lane == k, rk, rank_w)
    rank_ref[...] = rank_w[:, :TOP_K].astype(jnp.int32)
    carry_ref[...] = carry_ref[...] + jnp.sum(sel, axis=0, keepdims=True)
    cnt_ref[...] = carry_ref[...].astype(jnp.int32)


def _rank(idx, *, tm):
    t = idx.shape[0]
    return pl.pallas_call(
        functools.partial(_rank_kernel, tm=tm),
        out_shape=(jax.ShapeDtypeStruct((t, TOP_K), jnp.int32),
                   jax.ShapeDtypeStruct((1, LANES), jnp.int32)),
        grid=(t // tm,),
        in_specs=[pl.BlockSpec((tm, TOP_K), lambda i: (i, 0))],
        out_specs=(pl.BlockSpec((tm, TOP_K), lambda i: (i, 0)),
                   pl.BlockSpec((1, LANES), lambda i: (0, 0))),
        scratch_shapes=[pltpu.VMEM((1, LANES), _F32)],
        compiler_params=_cparams(("arbitrary",)),
        name="rank",
    )(idx)


def _dispatch_kernel(meta_ref, h_ref, idx_ref, rank_ref, xs_ref, pos_ref, zero_ref, sem, zsem, *, tm):
    n_rows = tm * TOP_K

    def zero_copy(r):
        return pltpu.make_async_copy(zero_ref.at[pl.ds(0, 1)], xs_ref.at[pl.ds(r, 1)], zsem)

    @pl.when(pl.program_id(0) == 0)
    def _zero_fill():
        zero_ref[...] = jnp.zeros_like(zero_ref)

        def per_expert(e, c):
            lo = meta_ref[e] + meta_ref[N_EXPERTS + e]
            hi = meta_ref[e] + meta_ref[2 * N_EXPERTS + e]
            lax.fori_loop(lo, hi, lambda r, c2: (zero_copy(r).start(), c2)[1], 0)
            lax.fori_loop(lo, hi, lambda r, c2: (zero_copy(r).wait(), c2)[1], 0)
            return c

        lax.fori_loop(0, N_EXPERTS, per_expert, 0)

    def row_copy(r):
        return pltpu.make_async_copy(h_ref.at[pl.ds(lax.shift_right_logical(r, 2), 1)],
                                     xs_ref.at[pl.ds(pos_ref[r], 1)], sem)

    def issue(r, c):
        pos_ref[r] = meta_ref[idx_ref[r]] + rank_ref[r]
        row_copy(r).start()
        return c

    lax.fori_loop(0, n_rows, issue, 0)
    lax.fori_loop(0, n_rows, lambda r, c: (row_copy(r).wait(), c)[1], 0)


def _dispatch(meta, h2p, idx_flat, rank_flat, n_sorted_rows, *, tm):
    t = h2p.shape[0]
    n_rows = tm * TOP_K
    smem_blk = pl.BlockSpec((n_rows,), lambda i, meta: (i,), memory_space=pltpu.SMEM)
    return pl.pallas_call(
        functools.partial(_dispatch_kernel, tm=tm),
        out_shape=(jax.ShapeDtypeStruct((n_sorted_rows, D_MODEL // 2), jnp.uint32),
                   jax.ShapeDtypeStruct((t * TOP_K,), jnp.int32)),
        grid_spec=pltpu.PrefetchScalarGridSpec(
            num_scalar_prefetch=1,
            grid=(t // tm,),
            in_specs=[pl.BlockSpec((tm, D_MODEL // 2), lambda i, meta: (i, 0)), smem_blk, smem_blk],
            out_specs=(pl.BlockSpec(memory_space=pl.ANY), smem_blk),
            scratch_shapes=[pltpu.VMEM((8, D_MODEL // 2), jnp.uint32),
                            pltpu.SemaphoreType.DMA, pltpu.SemaphoreType.DMA]),
        compiler_params=_cparams(("arbitrary",)),
        name="dispatch",
    )(meta, h2p, idx_flat, rank_flat)


_DEINT = 2 * LANES


def _deinterleave_matrix():
    k = jnp.arange(_DEINT)[:, None]
    n = jnp.arange(_DEINT)[None, :]
    src = jnp.where(n < LANES, 2 * n, 2 * (n - LANES) + 1)
    return (k == src).astype(_CDT)


def _experts_kernel(te_ref, nv_ref, xs_ref, perm_ref, w1_ref, b1g_ref, b1l_ref, w2_ref, b2_ref, y_ref,
                    wg_ref, wl_ref, w2c_ref):
    i = pl.program_id(0)
    live = i < nv_ref[0]
    new_expert = (i == 0) | (te_ref[i] != te_ref[jnp.maximum(i - 1, 0)])

    @pl.when(live & new_expert)
    def _stage_weights():
        for c in range(2 * D_FF // _DEINT):
            both = _dot(w1_ref[0, :, c * _DEINT:(c + 1) * _DEINT].astype(_CDT), perm_ref[...])
            wg_ref[:, c * LANES:(c + 1) * LANES] = both[:, :LANES].astype(_CDT)
            wl_ref[:, c * LANES:(c + 1) * LANES] = both[:, LANES:].astype(_CDT)
        w2c_ref[...] = w2_ref[0].astype(_CDT)

    @pl.when(live)
    def _mlp():
        x = _unpack_bf16_pairs(xs_ref[...]).astype(_CDT)
        glu = jnp.minimum(_dot(x, wg_ref[...]) + b1g_ref[0], SWIGLU_LIMIT)
        lin = jnp.clip(_dot(x, wl_ref[...]) + b1l_ref[0], -SWIGLU_LIMIT, SWIGLU_LIMIT)
        act = glu * jax.nn.sigmoid(SWIGLU_ALPHA * glu) * (lin + 1.0)
        y_ref[...] = _dot(act.astype(_CDT), w2c_ref[...]) + b2_ref[0]

    @pl.when(jnp.logical_not(live))
    def _unused_tile():
        y_ref[...] = jnp.zeros_like(y_ref)


def _experts(tile_expert, n_valid, xs, w1, b1g, b1l, w2, b2, *, tm):
    n_tiles = xs.shape[0] // tm
    by_expert = lambda shape: pl.BlockSpec((1,) + shape, lambda i, te, nv: (te[i], 0, 0))
    return pl.pallas_call(
        _experts_kernel,
        out_shape=jax.ShapeDtypeStruct((xs.shape[0], D_MODEL), _F32),
        grid_spec=pltpu.PrefetchScalarGridSpec(
            num_scalar_prefetch=2,
            grid=(n_tiles,),
            in_specs=[pl.BlockSpec((tm, D_MODEL // 2), lambda i, te, nv: (jnp.minimum(i, nv[0] - 1), 0)),
                      pl.BlockSpec((_DEINT, _DEINT), lambda i, te, nv: (0, 0)),
                      by_expert((D_MODEL, 2 * D_FF)),
                      by_expert((1, D_FF)), by_expert((1, D_FF)),
                      by_expert((D_FF, D_MODEL)), by_expert((1, D_MODEL))],
            out_specs=pl.BlockSpec((tm, D_MODEL), lambda i, te, nv: (i, 0)),
            scratch_shapes=[pltpu.VMEM((D_MODEL, D_FF), _CDT), pltpu.VMEM((D_MODEL, D_FF), _CDT),
                            pltpu.VMEM((D_FF, D_MODEL), _CDT)]),
        compiler_params=_cparams(("arbitrary",)),
        name="experts",
    )(tile_expert, n_valid, xs, _deinterleave_matrix(), w1, b1g, b1l, w2, b2)


def _combine_kernel(pos_ref, wt_ref, h2_ref, g_ref, b_ref, ys_ref, o_ref, buf_ref, sem, *, tm):
    n_rows = tm * TOP_K

    def row_copy(r):
        return pltpu.make_async_copy(ys_ref.at[pl.ds(pos_ref[r], 1)],
                                     buf_ref.at[r & (TOP_K - 1), pl.ds(lax.shift_right_logical(r, 2), 1)], sem)

    lax.fori_loop(0, n_rows, lambda r, c: (row_copy(r).start(), c)[1], 0)
    lax.fori_loop(0, n_rows, lambda r, c: (row_copy(r).wait(), c)[1], 0)
    wt = wt_ref[...]
    ffn = wt[:, 0:1] * buf_ref[0]
    for k in range(1, TOP_K):
        ffn = ffn + wt[:, k:k + 1] * buf_ref[k]
    o_ref[...] = _layer_norm(DEEPNORM_ALPHA * h2_ref[...] + ffn, g_ref[...], b_ref[...])


def _combine(pos_flat, wts, h2, g3, b3, ys, *, tm):
    t = h2.shape[0]
    vec = pl.BlockSpec((1, D_MODEL), lambda i: (0, 0))
    return pl.pallas_call(
        functools.partial(_combine_kernel, tm=tm),
        out_shape=jax.ShapeDtypeStruct((t, D_MODEL), _F32),
        grid=(t // tm,),
        in_specs=[pl.BlockSpec((tm * TOP_K,), lambda i: (i,), memory_space=pltpu.SMEM),
                  pl.BlockSpec((tm, TOP_K), lambda i: (i, 0)),
                  pl.BlockSpec((tm, D_MODEL), lambda i: (i, 0)), vec, vec,
                  pl.BlockSpec(memory_space=pl.ANY)],
        out_specs=pl.BlockSpec((tm, D_MODEL), lambda i: (i, 0)),
        scratch_shapes=[pltpu.VMEM((TOP_K, tm, D_MODEL), _F32), pltpu.SemaphoreType.DMA],
        compiler_params=_cparams(("arbitrary",)),
        name="combine",
    )(pos_flat, wts, h2, g3, b3, ys)


def _tiles(seq):
    big = 512 if seq % 512 == 0 else MOBA_BLOCK
    return dict(inproj=big, attn_q=MOBA_BLOCK, post=big if big < 512 else 256, rank=big, dispatch=big,
                expert=512, combine=MOBA_BLOCK)


def kernel(x, mem, w_in, diff_lambda_q1, diff_lambda_k1, diff_lambda_q2, diff_lambda_k2, diff_subln_g,
           w_mix_out, ln1_g, ln1_b, mem_ln_g, mem_ln_b, w_mem_q, w_mem_kv, w_mem_o, ln2_g, ln2_b, w_router,
           b_router, w_mlp1, b_mlp1, w_mlp2, b_mlp2, ln3_g, ln3_b):
    batch, seq, d = x.shape
    mem_len = mem.shape[1]
    assert d == D_MODEL and seq % MOBA_BLOCK == 0 and w_in.shape[0] == 1
    t = batch * seq
    ts = _tiles(seq)
    row = lambda v: v.reshape(1, -1).astype(_F32)

    col = jnp.arange(PROJ_WIDTH)
    is_q = (col < DIFF_WIDTH) | ((col >= 3 * DIFF_WIDTH) & (col < 3 * DIFF_WIDTH + MOBA_WIDTH))
    w_in_c = (w_in[0] * jnp.where(is_q, HEAD_DIM ** -0.5, 1.0)[None, :]).astype(_CDT)
    lam_vecs = jnp.stack([diff_lambda_q1[0], diff_lambda_k1[0], diff_lambda_q2[0], diff_lambda_k2[0]]).astype(_F32)
    w_r = jnp.zeros((D_MODEL, LANES), _F32).at[:, :N_EXPERTS].set(w_router[0]).astype(_CDT)
    b_r = jnp.full((1, LANES), _NEG, _F32).at[0, :N_EXPERTS].set(b_router[0])
    b1g = b_mlp1[0][:, None, 0::2].astype(_F32)
    b1l = b_mlp1[0][:, None, 1::2].astype(_F32)
    b2 = b_mlp2[0][:, None, :].astype(_F32)

    x2d = x.reshape(t, d)
    proj, kmean = _inproj(x2d, w_in_c, seq, tm=ts["inproj"])
    kmean = kmean.reshape(batch, seq // MOBA_BLOCK, MOBA_WIDTH)
    out_a = _diff_attn(proj, lam_vecs, row(diff_subln_g[0]), batch, seq, tq=ts["attn_q"])
    out_b = _moba_attn(proj, kmean, batch, seq)
    kv = _mem_kv(mem.reshape(batch * mem_len, d), row(mem_ln_g), row(mem_ln_b), w_mem_kv[0].astype(_CDT),
                 tm=mem_len)
    h2, h2p, top_idx, top_w = _post_attn(
        x2d, out_a, out_b, w_mix_out[0].astype(_CDT), row(ln1_g[0]), row(ln1_b[0]), kv,
        w_mem_q[0].astype(_CDT), w_mem_o[0].astype(_CDT), row(ln2_g[0]), row(ln2_b[0]), w_r, b_r,
        seq, mem_len, tm=ts["post"])

    rank4, counts = _rank(top_idx, tm=ts["rank"])
    tmx = ts["expert"]
    cnt = counts[0, :N_EXPERTS]
    padded = (cnt + tmx - 1) // tmx * tmx
    ends = jnp.cumsum(padded)
    meta = jnp.concatenate([ends - padded, cnt, padded]).astype(jnp.int32)
    n_sorted_rows = t * TOP_K + N_EXPERTS * tmx
    n_tiles = n_sorted_rows // tmx
    tile_start = jnp.arange(n_tiles, dtype=jnp.int32) * tmx
    tile_expert = jnp.minimum(jnp.sum(tile_start[:, None] >= ends[None, :], axis=1), N_EXPERTS - 1).astype(jnp.int32)
    n_valid = (ends[-1:] // tmx).astype(jnp.int32)

    xs, pos = _dispatch(meta, h2p, top_idx.reshape(-1), rank4.reshape(-1), n_sorted_rows, tm=ts["dispatch"])
    ys = _experts(tile_expert, n_valid, xs, w_mlp1[0], b1g, b1l, w_mlp2[0], b2, tm=tmx)
    out = _combine(pos, top_w, h2, row(ln3_g[0]), row(ln3_b[0]), ys, tm=ts["combine"])
    return out.reshape(batch, seq, d)
```

```python
import functools
import math

import jax
import jax.numpy as jnp
from jax import lax
from jax.experimental import pallas as pl
from jax.experimental.pallas import tpu as pltpu

D_MODEL = 1024
DIFF_HEADS = 4
HEAD_DIM = 64
DIFF_WIDTH = DIFF_HEADS * 2 * HEAD_DIM
MOBA_HEADS = 8
MOBA_WIDTH = MOBA_HEADS * HEAD_DIM
MOBA_BLOCK = 256
MOBA_TOPK = 3
ROPE_THETA = 500000.0
ROT_DIM = HEAD_DIM // 4
MEM_HEADS = 4
MEM_HEAD_DIM = D_MODEL // MEM_HEADS
N_EXPERTS = 32
TOP_K = 4
D_FF = D_MODEL
SWIGLU_ALPHA = 1.702
SWIGLU_LIMIT = 7.0
LN_EPS = 1e-5
RMS_EPS = 1e-5
DEEPNORM_ALPHA = 2.0 ** 0.25
LAMBDA_INIT = 0.8 - 0.6 * math.exp(0.0)

LANES = 128
ROW_TILE = 8
V7X_VMEM_LIMIT = 56 * 1024 * 1024

_CDT = jnp.bfloat16
_F32 = jnp.float32
_NEG = -1e30

_CB_DQ, _CB_DK, _CB_DV = 0, 4, 8
_CB_MQ, _CB_MK, _CB_MV = 12, 16, 20
PROJ_WIDTH = 3 * (DIFF_WIDTH + MOBA_WIDTH)


def _cparams(sem, vmem=V7X_VMEM_LIMIT):
    return pltpu.CompilerParams(dimension_semantics=sem, vmem_limit_bytes=vmem)


def _dot(a, b):
    return jnp.dot(a, b, preferred_element_type=_F32)


def _dot_nt(a, b):
    return lax.dot_general(a, b, (((1,), (1,)), ((), ())), preferred_element_type=_F32)


def _layer_norm(x, g, b):
    mu = jnp.mean(x, axis=-1, keepdims=True)
    xc = x - mu
    var = jnp.mean(xc * xc, axis=-1, keepdims=True)
    return xc * lax.rsqrt(var + LN_EPS) * g + b


_ROPE_BLOCKS = tuple(range(_CB_DQ, _CB_DV)) + tuple(range(_CB_MQ, _CB_MV))


def _inproj_kernel(x_ref, w_ref, cos_ref, sa_ref, sb_ref, o_ref, km_ref, *, tm):
    x = x_ref[...].astype(_CDT)
    cos, sa, sb = cos_ref[...], sa_ref[...], sb_ref[...]
    seg_w = 4 * LANES
    for seg in range(PROJ_WIDTH // seg_w):
        pseg = _dot(x, w_ref[:, seg * seg_w:(seg + 1) * seg_w])
        for c in range(4):
            cb = seg * 4 + c
            cols = slice(cb * LANES, (cb + 1) * LANES)
            p = pseg[:, c * LANES:(c + 1) * LANES]
            if cb in _ROPE_BLOCKS:
                p = p * cos + pltpu.roll(p, 8, 1) * sa + pltpu.roll(p, LANES - 8, 1) * sb
            o_ref[:, cols] = p.astype(o_ref.dtype)
            if _CB_MK <= cb < _CB_MV:
                kc = slice((cb - _CB_MK) * LANES, (cb - _CB_MK + 1) * LANES)
                for blk in range(tm // MOBA_BLOCK):
                    rows = p[blk * MOBA_BLOCK:(blk + 1) * MOBA_BLOCK, :]
                    km_ref[blk, :, kc] = jnp.sum(rows, axis=0, keepdims=True) * (1.0 / MOBA_BLOCK)


def _rope_tables(seq):
    half = ROT_DIM // 2
    inv_freq = ROPE_THETA ** (-jnp.arange(0, ROT_DIM, 2, dtype=_F32) / ROT_DIM)
    ang = jnp.arange(seq, dtype=_F32)[:, None] * inv_freq[None, :]
    cos, sin = jnp.cos(ang), jnp.sin(ang)
    lane = jnp.arange(LANES) % HEAD_DIM
    first, second = lane < half, (lane >= half) & (lane < ROT_DIM)
    idx = jnp.where(first, lane, jnp.where(second, lane - half, 0))
    cos_t = jnp.where((first | second)[None, :], cos[:, idx], 1.0)
    sa_t = jnp.where(second[None, :], sin[:, idx], 0.0)
    sb_t = jnp.where(first[None, :], -sin[:, idx], 0.0)
    return cos_t, sa_t, sb_t


def _inproj(x2d, w_in, seq, *, tm):
    t = x2d.shape[0]
    cos_t, sa_t, sb_t = _rope_tables(seq)
    n_pos = seq // tm
    tab_spec = pl.BlockSpec((tm, LANES), lambda i: (i % n_pos, 0))
    return pl.pallas_call(
        functools.partial(_inproj_kernel, tm=tm),
        out_shape=(jax.ShapeDtypeStruct((t, PROJ_WIDTH), _CDT),
                   jax.ShapeDtypeStruct((t // MOBA_BLOCK, 1, MOBA_WIDTH), _F32)),
        grid=(t // tm,),
        in_specs=[pl.BlockSpec((tm, D_MODEL), lambda i: (i, 0)),
                  pl.BlockSpec((D_MODEL, PROJ_WIDTH), lambda i: (0, 0)),
                  tab_spec, tab_spec, tab_spec],
        out_specs=(pl.BlockSpec((tm, PROJ_WIDTH), lambda i: (i, 0)),
                   pl.BlockSpec((tm // MOBA_BLOCK, 1, MOBA_WIDTH), lambda i: (i, 0, 0))),
        compiler_params=_cparams(("parallel",)),
        name="inproj",
    )(x2d, w_in, cos_t, sa_t, sb_t)


def _softmax_parts(parts):
    m = functools.reduce(jnp.maximum, [jnp.max(s, axis=1, keepdims=True) for s in parts])
    es = [jnp.exp(s - m) for s in parts]
    l = functools.reduce(jnp.add, [jnp.sum(e, axis=1, keepdims=True) for e in es])
    return es, l


def _causal_mask(tq):
    row = lax.broadcasted_iota(jnp.int32, (tq, tq), 0)
    col = lax.broadcasted_iota(jnp.int32, (tq, tq), 1)
    return col <= row


def _diff_attn_kernel(q_ref, k_ref, v_ref, lam_ref, g_ref, o_ref, *, seq, tq):
    lane = lax.broadcasted_iota(jnp.int32, (tq, LANES), 1)
    mask = _causal_mask(tq)
    lv = lam_ref[...]
    lam = (jnp.exp(jnp.sum(lv[0:1] * lv[1:2], axis=1, keepdims=True))
           - jnp.exp(jnp.sum(lv[2:3] * lv[3:4], axis=1, keepdims=True)) + LAMBDA_INIT)
    gain = g_ref[...] * (1.0 - LAMBDA_INIT)
    for i in range(seq // tq):
        rows = slice(i * tq, (i + 1) * tq)
        q = q_ref[rows, :]
        zero = jnp.zeros_like(q)
        q1 = jnp.where(lane < HEAD_DIM, q, zero)
        q2 = jnp.where(lane >= HEAD_DIM, q, zero)
        kd, vd = k_ref[rows, :], v_ref[rows, :]
        s1 = [jnp.where(mask, _dot_nt(q1, kd), _NEG)]
        s2 = [jnp.where(mask, _dot_nt(q2, kd), _NEG)]
        if i:
            kp = k_ref[:i * tq, :]
            s1.append(_dot_nt(q1, kp))
            s2.append(_dot_nt(q2, kp))
        e1, l1 = _softmax_parts(s1)
        e2, l2 = _softmax_parts(s2)
        c1, c2 = 1.0 / l1, lam / l2
        a = [(x * c1 - y * c2).astype(_CDT) for x, y in zip(e1, e2)]
        o = _dot(a[0], vd)
        if i:
            o = o + _dot(a[1], v_ref[:i * tq, :])
        o = o * lax.rsqrt(jnp.mean(o * o, axis=1, keepdims=True) + RMS_EPS) * gain
        o_ref[rows, :] = o.astype(o_ref.dtype)


def _diff_attn(proj, lam_vecs, subln_g, batch, seq, *, tq):
    t = batch * seq
    blk = lambda cb: pl.BlockSpec((seq, LANES), lambda b, h: (b, cb + h))
    return pl.pallas_call(
        functools.partial(_diff_attn_kernel, seq=seq, tq=tq),
        out_shape=jax.ShapeDtypeStruct((t, DIFF_WIDTH), _CDT),
        grid=(batch, DIFF_HEADS),
        in_specs=[blk(_CB_DQ), blk(_CB_DK), blk(_CB_DV),
                  pl.BlockSpec((4, HEAD_DIM), lambda b, h: (0, 0)),
                  pl.BlockSpec((1, 2 * HEAD_DIM), lambda b, h: (0, 0))],
        out_specs=pl.BlockSpec((seq, LANES), lambda b, h: (b, h)),
        compiler_params=_cparams(("parallel", "parallel")),
        name="diff_attn",
    )(proj, proj, proj, lam_vecs, subln_g)


def _moba_selection(gate, n_past):
    blk = lax.broadcasted_iota(jnp.int32, gate.shape, 0)
    rank = jnp.zeros(gate.shape, _F32)
    for m_blk in range(n_past):
        gm = gate[m_blk:m_blk + 1, :]
        beats = (gm > gate) | ((gm == gate) & (m_blk < blk))
        rank = rank + jnp.where(beats, 1.0, 0.0)
    return jnp.where((blk < n_past) & (rank < MOBA_TOPK), 1.0, 0.0)


def _moba_kernel(q_ref, k_ref, v_ref, km_ref, o_ref, *, nb):
    tq = MOBA_BLOCK
    lane = lax.broadcasted_iota(jnp.int32, (tq, LANES), 1)
    mask = _causal_mask(tq)
    km = km_ref[...].astype(_CDT)
    for i in range(nb):
        rows = slice(i * tq, (i + 1) * tq)
        q = q_ref[rows, :]
        zero = jnp.zeros_like(q)
        kd, vd = k_ref[rows, :], v_ref[rows, :]
        gated = i > MOBA_TOPK
        outs = []
        for hh in range(2):
            in_head = (lane >= hh * HEAD_DIM) & (lane < (hh + 1) * HEAD_DIM)
            qh = jnp.where(in_head, q, zero)
            parts = [jnp.where(mask, _dot_nt(qh, kd), _NEG)]
            if gated:
                sel = _moba_selection(_dot_nt(km, qh), i)
                sel_q = jnp.concatenate([sel, jnp.zeros((LANES - nb, tq), _F32)], axis=0).T
            for j in range(i):
                s = _dot_nt(qh, k_ref[j * tq:(j + 1) * tq, :])
                parts.append(jnp.where(sel_q[:, j:j + 1] > 0.5, s, _NEG) if gated else s)
            es, l = _softmax_parts(parts)
            acc = _dot(es[0].astype(_CDT), vd)
            for j in range(i):
                acc = acc + _dot(es[j + 1].astype(_CDT), v_ref[j * tq:(j + 1) * tq, :])
            outs.append(acc / l)
        o_ref[rows, :] = jnp.where(lane < HEAD_DIM, outs[0], outs[1]).astype(o_ref.dtype)


def _moba_attn(proj, kmean, batch, seq):
    nb = seq // MOBA_BLOCK
    t = batch * seq
    blk = lambda cb: pl.BlockSpec((seq, LANES), lambda b, p: (b, cb + p))
    return pl.pallas_call(
        functools.partial(_moba_kernel, nb=nb),
        out_shape=jax.ShapeDtypeStruct((t, MOBA_WIDTH), _CDT),
        grid=(batch, MOBA_HEADS // 2),
        in_specs=[blk(_CB_MQ), blk(_CB_MK), blk(_CB_MV),
                  pl.BlockSpec((None, nb, LANES), lambda b, p: (b, 0, p))],
        out_specs=pl.BlockSpec((seq, LANES), lambda b, p: (b, p)),
        compiler_params=_cparams(("parallel", "parallel")),
        name="moba_attn",
    )(proj, proj, proj, kmean)


def _mem_kv_kernel(mem_ref, g_ref, b_ref, w_ref, o_ref):
    mn = _layer_norm(mem_ref[...], g_ref[...], b_ref[...])
    o_ref[...] = _dot(mn.astype(_CDT), w_ref[...]).astype(o_ref.dtype)


def _mem_kv(mem2d, g, b, w_kv, *, tm):
    rows = mem2d.shape[0]
    vec = pl.BlockSpec((1, D_MODEL), lambda i: (0, 0))
    return pl.pallas_call(
        _mem_kv_kernel,
        out_shape=jax.ShapeDtypeStruct((rows, 2 * D_MODEL), _CDT),
        grid=(rows // tm,),
        in_specs=[pl.BlockSpec((tm, D_MODEL), lambda i: (i, 0)), vec, vec,
                  pl.BlockSpec((D_MODEL, 2 * D_MODEL), lambda i: (0, 0))],
        out_specs=pl.BlockSpec((tm, 2 * D_MODEL), lambda i: (i, 0)),
        compiler_params=_cparams(("parallel",)),
        name="mem_kv",
    )(mem2d, g, b, w_kv)


def _store_row_tiles(ref, val):
    n = val.shape[0]
    for c in range(ROW_TILE):
        ref[pl.ds(c, n, stride=ROW_TILE), :] = val[:, c * LANES:(c + 1) * LANES]


def _load_row_tiles(ref, n):
    return jnp.concatenate([ref[pl.ds(c, n, stride=ROW_TILE), :] for c in range(ROW_TILE)], axis=1)


def _post_attn_kernel(x_ref, oa_ref, ob_ref, wout_ref, g1_ref, b1_ref, kv_ref, wq_ref, wo_ref,
                      g2_ref, b2_ref, wr_ref, br_ref, h2t_ref, idx_ref, wt_ref, *, tm):
    mix = _dot(oa_ref[...], wout_ref[:DIFF_WIDTH, :]) + _dot(ob_ref[...], wout_ref[DIFF_WIDTH:, :])
    h1 = _layer_norm(DEEPNORM_ALPHA * x_ref[...] + mix, g1_ref[...], b1_ref[...])

    q = (_dot(h1.astype(_CDT), wq_ref[...]) * (MEM_HEAD_DIM ** -0.5)).astype(_CDT)
    heads = []
    for h in range(MEM_HEADS):
        c0 = h * MEM_HEAD_DIM
        kh = kv_ref[:, c0:c0 + MEM_HEAD_DIM]
        vh = kv_ref[:, D_MODEL + c0:D_MODEL + c0 + MEM_HEAD_DIM]
        s = _dot_nt(q[:, c0:c0 + MEM_HEAD_DIM], kh)
        p = jnp.exp(s - jnp.max(s, axis=1, keepdims=True))
        o = _dot(p.astype(_CDT), vh) / jnp.sum(p, axis=1, keepdims=True)
        heads.append(o.astype(_CDT))
    xatt = _dot(jnp.concatenate(heads, axis=1), wo_ref[...])
    h2 = _layer_norm(DEEPNORM_ALPHA * h1 + xatt, g2_ref[...], b2_ref[...])
    _store_row_tiles(h2t_ref, h2)

    logits = _dot(h2.astype(_CDT), wr_ref[...]) + br_ref[...]
    lane = lax.broadcasted_iota(jnp.int32, (tm, LANES), 1)
    lane_f = lane.astype(_F32)
    vals, idxs = [], []
    work = logits
    for _ in range(TOP_K):
        m = jnp.max(work, axis=1, keepdims=True)
        ik = jnp.min(jnp.where(work == m, lane_f, float(LANES)), axis=1, keepdims=True)
        vals.append(m)
        idxs.append(ik)
        work = jnp.where(lane_f == ik, -jnp.inf, work)
    exps = [jnp.exp(v - vals[0]) for v in vals]
    denom = exps[0] + exps[1] + exps[2] + exps[3]
    idx_w = jnp.zeros((tm, LANES), _F32)
    wt_w = jnp.zeros((tm, LANES), _F32)
    for k in range(TOP_K):
        idx_w = jnp.where(lane == k, idxs[k], idx_w)
        wt_w = jnp.where(lane == k, exps[k] / denom, wt_w)
    idx_ref[...] = idx_w[:, :TOP_K].astype(jnp.int32)
    wt_ref[...] = wt_w[:, :TOP_K]


def _post_attn(x2d, out_a, out_b, w_out, g1, b1, kv, w_q, w_o, g2, b2, w_r, b_r, seq, mem_len, *, tm):
    t = x2d.shape[0]
    per_b = seq // tm
    full = lambda shape: pl.BlockSpec(shape, lambda i: (0,) * len(shape))
    vec = full((1, D_MODEL))
    return pl.pallas_call(
        functools.partial(_post_attn_kernel, tm=tm),
        out_shape=(jax.ShapeDtypeStruct((t * ROW_TILE, LANES), _F32),
                   jax.ShapeDtypeStruct((t, TOP_K), jnp.int32),
                   jax.ShapeDtypeStruct((t, TOP_K), _F32)),
        grid=(t // tm,),
        in_specs=[pl.BlockSpec((tm, D_MODEL), lambda i: (i, 0)),
                  pl.BlockSpec((tm, DIFF_WIDTH), lambda i: (i, 0)),
                  pl.BlockSpec((tm, MOBA_WIDTH), lambda i: (i, 0)),
                  full((D_MODEL, D_MODEL)), vec, vec,
                  pl.BlockSpec((mem_len, 2 * D_MODEL), lambda i: (i // per_b, 0)),
                  full((D_MODEL, D_MODEL)), full((D_MODEL, D_MODEL)), vec, vec,
                  full((D_MODEL, LANES)), full((1, LANES))],
        out_specs=(pl.BlockSpec((tm * ROW_TILE, LANES), lambda i: (i, 0)),
                   pl.BlockSpec((tm, TOP_K), lambda i: (i, 0)),
                   pl.BlockSpec((tm, TOP_K), lambda i: (i, 0))),
        compiler_params=_cparams(("parallel",)),
        name="post_attn",
    )(x2d, out_a, out_b, w_out, g1, b1, kv, w_q, w_o, g2, b2, w_r, b_r)


def _rank_kernel(idx_ref, rank_ref, cnt_ref, carry_ref, *, tm):
    @pl.when(pl.program_id(0) == 0)
    def _init():
        carry_ref[...] = jnp.zeros_like(carry_ref)

    idx = idx_ref[...]
    lane = lax.broadcasted_iota(jnp.int32, (tm, LANES), 1)
    hits = [lane == idx[:, k:k + 1] for k in range(TOP_K)]
    sel = jnp.zeros((tm, LANES), _F32)
    for hit in hits:
        sel = sel + jnp.where(hit, 1.0, 0.0)
    row = lax.broadcasted_iota(jnp.int32, (tm, tm), 0)
    col = lax.broadcasted_iota(jnp.int32, (tm, tm), 1)
    earlier = jnp.where(col < row, 1.0, 0.0).astype(jnp.bfloat16)
    dense = _dot(earlier, sel.astype(jnp.bfloat16)) + carry_ref[...]
    rank_w = jnp.zeros((tm, LANES), _F32)
    for k, hit in enumerate(hits):
        rk = jnp.sum(jnp.where(hit, dense, 0.0), axis=1, keepdims=True)
        rank_w = jnp.where(lane == k, rk, rank_w)
    rank_ref[...] = rank_w[:, :TOP_K].astype(jnp.int32)
    carry_ref[...] = carry_ref[...] + jnp.sum(sel, axis=0, keepdims=True)
    cnt_ref[...] = carry_ref[...].astype(jnp.int32)


def _rank(idx, *, tm):
    t = idx.shape[0]
    return pl.pallas_call(
        functools.partial(_rank_kernel, tm=tm),
        out_shape=(jax.ShapeDtypeStruct((t, TOP_K), jnp.int32),
                   jax.ShapeDtypeStruct((1, LANES), jnp.int32)),
        grid=(t // tm,),
        in_specs=[pl.BlockSpec((tm, TOP_K), lambda i: (i, 0))],
        out_specs=(pl.BlockSpec((tm, TOP_K), lambda i: (i, 0)),
                   pl.BlockSpec((1, LANES), lambda i: (0, 0))),
        scratch_shapes=[pltpu.VMEM((1, LANES), _F32)],
        compiler_params=_cparams(("arbitrary",)),
        name="rank",
    )(idx)


_DMA_UNROLL = 8


def _dispatch_kernel(meta_ref, h_ref, idx_ref, rank_ref, xs_ref, pos_ref, zero_ref, sem, zsem, *, tm):
    n_rows = tm * TOP_K

    def zero_copy(r):
        return pltpu.make_async_copy(zero_ref, xs_ref.at[r], zsem)

    @pl.when(pl.program_id(0) == 0)
    def _zero_fill():
        zero_ref[...] = jnp.zeros_like(zero_ref)

        def per_expert(e, c):
            lo = meta_ref[e] + meta_ref[N_EXPERTS + e]
            hi = meta_ref[e] + meta_ref[2 * N_EXPERTS + e]
            lax.fori_loop(lo, hi, lambda r, c2: (zero_copy(r).start(), c2)[1], 0)
            lax.fori_loop(lo, hi, lambda r, c2: (zero_copy(r).wait(), c2)[1], 0)
            return c

        lax.fori_loop(0, N_EXPERTS, per_expert, 0)

    def row_copy(r, p):
        return pltpu.make_async_copy(h_ref.at[lax.shift_right_logical(r, 2)], xs_ref.at[p], sem)

    def issue(r, c):
        p = meta_ref[idx_ref[r]] + rank_ref[r]
        pos_ref[r] = p
        row_copy(r, p).start()
        return c

    lax.fori_loop(0, n_rows, issue, 0, unroll=_DMA_UNROLL)
    lax.fori_loop(0, n_rows, lambda r, c: (row_copy(r, pos_ref[r]).wait(), c)[1], 0, unroll=_DMA_UNROLL)


def _dispatch(meta, h2t, idx_flat, rank_flat, n_sorted_rows, *, tm):
    t = h2t.shape[0]
    n_rows = tm * TOP_K
    smem_blk = pl.BlockSpec((n_rows,), lambda i, meta: (i,), memory_space=pltpu.SMEM)
    return pl.pallas_call(
        functools.partial(_dispatch_kernel, tm=tm),
        out_shape=(jax.ShapeDtypeStruct((n_sorted_rows, ROW_TILE, LANES), _F32),
                   jax.ShapeDtypeStruct((t * TOP_K,), jnp.int32)),
        grid_spec=pltpu.PrefetchScalarGridSpec(
            num_scalar_prefetch=1,
            grid=(t // tm,),
            in_specs=[pl.BlockSpec((tm, ROW_TILE, LANES), lambda i, meta: (i, 0, 0)), smem_blk, smem_blk],
            out_specs=(pl.BlockSpec(memory_space=pl.ANY), smem_blk),
            scratch_shapes=[pltpu.VMEM((ROW_TILE, LANES), _F32),
                            pltpu.SemaphoreType.DMA, pltpu.SemaphoreType.DMA]),
        compiler_params=_cparams(("arbitrary",)),
        name="dispatch",
    )(meta, h2t, idx_flat, rank_flat)


_DEINT = 2 * LANES


def _deinterleave_matrix():
    k = jnp.arange(_DEINT)[:, None]
    n = jnp.arange(_DEINT)[None, :]
    src = jnp.where(n < LANES, 2 * n, 2 * (n - LANES) + 1)
    return (k == src).astype(_CDT)


def _experts_kernel(te_ref, nv_ref, xs_ref, perm_ref, w1_ref, b1g_ref, b1l_ref, w2_ref, b2_ref, y_ref,
                    wg_ref, wl_ref, w2c_ref, *, tm):
    i = pl.program_id(0)
    live = i < nv_ref[0]
    new_expert = (i == 0) | (te_ref[i] != te_ref[jnp.maximum(i - 1, 0)])

    @pl.when(live & new_expert)
    def _stage_weights():
        for c in range(2 * D_FF // _DEINT):
            both = _dot(w1_ref[0, :, c * _DEINT:(c + 1) * _DEINT].astype(_CDT), perm_ref[...])
            wg_ref[:, c * LANES:(c + 1) * LANES] = both[:, :LANES].astype(_CDT)
            wl_ref[:, c * LANES:(c + 1) * LANES] = both[:, LANES:].astype(_CDT)
        w2c_ref[...] = w2_ref[0].astype(_CDT)

    @pl.when(live)
    def _mlp():
        x = _load_row_tiles(xs_ref, tm).astype(_CDT)
        glu = jnp.minimum(_dot(x, wg_ref[...]) + b1g_ref[0], SWIGLU_LIMIT)
        lin = jnp.clip(_dot(x, wl_ref[...]) + b1l_ref[0], -SWIGLU_LIMIT, SWIGLU_LIMIT)
        act = glu * jax.nn.sigmoid(SWIGLU_ALPHA * glu) * (lin + 1.0)
        _store_row_tiles(y_ref, _dot(act.astype(_CDT), w2c_ref[...]) + b2_ref[0])

    @pl.when(jnp.logical_not(live))
    def _unused_tile():
        y_ref[...] = jnp.zeros_like(y_ref)


def _experts(tile_expert, n_valid, xs, w1, b1g, b1l, w2, b2, *, tm):
    n_tiles = xs.shape[0] // (tm * ROW_TILE)
    by_expert = lambda shape: pl.BlockSpec((1,) + shape, lambda i, te, nv: (te[i], 0, 0))
    return pl.pallas_call(
        functools.partial(_experts_kernel, tm=tm),
        out_shape=jax.ShapeDtypeStruct(xs.shape, _F32),
        grid_spec=pltpu.PrefetchScalarGridSpec(
            num_scalar_prefetch=2,
            grid=(n_tiles,),
            in_specs=[pl.BlockSpec((tm * ROW_TILE, LANES), lambda i, te, nv: (jnp.minimum(i, nv[0] - 1), 0)),
                      pl.BlockSpec((_DEINT, _DEINT), lambda i, te, nv: (0, 0)),
                      by_expert((D_MODEL, 2 * D_FF)),
                      by_expert((1, D_FF)), by_expert((1, D_FF)),
                      by_expert((D_FF, D_MODEL)), by_expert((1, D_MODEL))],
            out_specs=pl.BlockSpec((tm * ROW_TILE, LANES), lambda i, te, nv: (i, 0)),
            scratch_shapes=[pltpu.VMEM((D_MODEL, D_FF), _CDT), pltpu.VMEM((D_MODEL, D_FF), _CDT),
                            pltpu.VMEM((D_FF, D_MODEL), _CDT)]),
        compiler_params=_cparams(("arbitrary",)),
        name="experts",
    )(tile_expert, n_valid, xs, _deinterleave_matrix(), w1, b1g, b1l, w2, b2)


def _combine_kernel(pos_ref, nxt_ref, wt_ref, h2t_ref, g_ref, b_ref, ys_ref, o_ref, buf_ref, sem, *, tm):
    n_rows = tm * TOP_K
    i = pl.program_id(0)
    slot = i & 1

    def row_copy(p_ref, r, s):
        tile_rows = pl.ds(pl.multiple_of(lax.shift_right_logical(r, 2) * ROW_TILE, ROW_TILE), ROW_TILE)
        return pltpu.make_async_copy(ys_ref.at[p_ref[r]], buf_ref.at[s, r & (TOP_K - 1), tile_rows], sem.at[s])

    def start_gather(p_ref, s):
        lax.fori_loop(0, n_rows, lambda r, c: (row_copy(p_ref, r, s).start(), c)[1], 0, unroll=_DMA_UNROLL)

    @pl.when(i == 0)
    def _first():
        start_gather(pos_ref, 0)

    @pl.when(i + 1 < pl.num_programs(0))
    def _prefetch():
        start_gather(nxt_ref, 1 - slot)

    lax.fori_loop(0, n_rows, lambda r, c: (row_copy(pos_ref, r, slot).wait(), c)[1], 0, unroll=_DMA_UNROLL)
    wt = wt_ref[...]
    ffn = wt[:, 0:1] * _load_row_tiles(buf_ref.at[slot, 0], tm)
    for k in range(1, TOP_K):
        ffn = ffn + wt[:, k:k + 1] * _load_row_tiles(buf_ref.at[slot, k], tm)
    o_ref[...] = _layer_norm(DEEPNORM_ALPHA * _load_row_tiles(h2t_ref, tm) + ffn, g_ref[...], b_ref[...])


def _combine(pos_flat, wts, h2t, g3, b3, ys, *, tm):
    t = h2t.shape[0] // ROW_TILE
    n_steps = t // tm
    vec = pl.BlockSpec((1, D_MODEL), lambda i: (0, 0))
    return pl.pallas_call(
        functools.partial(_combine_kernel, tm=tm),
        out_shape=jax.ShapeDtypeStruct((t, D_MODEL), _F32),
        grid=(n_steps,),
        in_specs=[pl.BlockSpec((tm * TOP_K,), lambda i: (i,), memory_space=pltpu.SMEM),
                  pl.BlockSpec((tm * TOP_K,), lambda i: (jnp.minimum(i + 1, n_steps - 1),), memory_space=pltpu.SMEM),
                  pl.BlockSpec((tm, TOP_K), lambda i: (i, 0)),
                  pl.BlockSpec((tm * ROW_TILE, LANES), lambda i: (i, 0)), vec, vec,
                  pl.BlockSpec(memory_space=pl.ANY)],
        out_specs=pl.BlockSpec((tm, D_MODEL), lambda i: (i, 0)),
        scratch_shapes=[pltpu.VMEM((2, TOP_K, tm * ROW_TILE, LANES), _F32), pltpu.SemaphoreType.DMA((2,))],
        compiler_params=_cparams(("arbitrary",)),
        name="combine",
    )(pos_flat, pos_flat, wts, h2t, g3, b3, ys)


def _tiles(seq):
    big = 512 if seq % 512 == 0 else MOBA_BLOCK
    return dict(inproj=big, attn_q=MOBA_BLOCK, post=big if big < 512 else 256, rank=big, dispatch=big,
                expert=512, combine=MOBA_BLOCK)


def kernel(x, mem, w_in, diff_lambda_q1, diff_lambda_k1, diff_lambda_q2, diff_lambda_k2, diff_subln_g,
           w_mix_out, ln1_g, ln1_b, mem_ln_g, mem_ln_b, w_mem_q, w_mem_kv, w_mem_o, ln2_g, ln2_b, w_router,
           b_router, w_mlp1, b_mlp1, w_mlp2, b_mlp2, ln3_g, ln3_b):
    batch, seq, d = x.shape
    mem_len = mem.shape[1]
    assert d == D_MODEL and seq % MOBA_BLOCK == 0 and w_in.shape[0] == 1
    t = batch * seq
    ts = _tiles(seq)
    row = lambda v: v.reshape(1, -1).astype(_F32)

    col = jnp.arange(PROJ_WIDTH)
    is_q = (col < DIFF_WIDTH) | ((col >= 3 * DIFF_WIDTH) & (col < 3 * DIFF_WIDTH + MOBA_WIDTH))
    w_in_c = (w_in[0] * jnp.where(is_q, HEAD_DIM ** -0.5, 1.0)[None, :]).astype(_CDT)
    lam_vecs = jnp.stack([diff_lambda_q1[0], diff_lambda_k1[0], diff_lambda_q2[0], diff_lambda_k2[0]]).astype(_F32)
    w_r = jnp.zeros((D_MODEL, LANES), _F32).at[:, :N_EXPERTS].set(w_router[0]).astype(_CDT)
    b_r = jnp.full((1, LANES), _NEG, _F32).at[0, :N_EXPERTS].set(b_router[0])
    b1g = b_mlp1[0][:, None, 0::2].astype(_F32)
    b1l = b_mlp1[0][:, None, 1::2].astype(_F32)
    b2 = b_mlp2[0][:, None, :].astype(_F32)

    x2d = x.reshape(t, d)
    proj, kmean = _inproj(x2d, w_in_c, seq, tm=ts["inproj"])
    kmean = kmean.reshape(batch, seq // MOBA_BLOCK, MOBA_WIDTH)
    out_a = _diff_attn(proj, lam_vecs, row(diff_subln_g[0]), batch, seq, tq=ts["attn_q"])
    out_b = _moba_attn(proj, kmean, batch, seq)
    kv = _mem_kv(mem.reshape(batch * mem_len, d), row(mem_ln_g), row(mem_ln_b), w_mem_kv[0].astype(_CDT),
                 tm=mem_len)
    h2t, top_idx, top_w = _post_attn(
        x2d, out_a, out_b, w_mix_out[0].astype(_CDT), row(ln1_g[0]), row(ln1_b[0]), kv,
        w_mem_q[0].astype(_CDT), w_mem_o[0].astype(_CDT), row(ln2_g[0]), row(ln2_b[0]), w_r, b_r,
        seq, mem_len, tm=ts["post"])

    rank4, counts = _rank(top_idx, tm=ts["rank"])
    tmx = ts["expert"]
    cnt = counts[0, :N_EXPERTS]
    padded = (cnt + tmx - 1) // tmx * tmx
    ends = jnp.cumsum(padded)
    meta = jnp.concatenate([ends - padded, cnt, padded]).astype(jnp.int32)
    n_sorted_rows = t * TOP_K + N_EXPERTS * tmx
    n_tiles = n_sorted_rows // tmx
    tile_start = jnp.arange(n_tiles, dtype=jnp.int32) * tmx
    tile_expert = jnp.minimum(jnp.sum(tile_start[:, None] >= ends[None, :], axis=1), N_EXPERTS - 1).astype(jnp.int32)
    n_valid = (ends[-1:] // tmx).astype(jnp.int32)

    xs, pos = _dispatch(meta, h2t.reshape(t, ROW_TILE, LANES), top_idx.reshape(-1), rank4.reshape(-1),
                        n_sorted_rows, tm=ts["dispatch"])
    ys = _experts(tile_expert, n_valid, xs.reshape(n_sorted_rows * ROW_TILE, LANES), w_mlp1[0], b1g, b1l,
                  w_mlp2[0], b2, tm=tmx)
    out = _combine(pos, top_w, h2t, row(ln3_g[0]), row(ln3_b[0]),
                   ys.reshape(n_sorted_rows, ROW_TILE, LANES), tm=ts["combine"])
    return out.reshape(batch, seq, d)
```

```python
import functools
import math

import jax
import jax.numpy as jnp
from jax import lax
from jax.experimental import pallas as pl
from jax.experimental.pallas import tpu as pltpu

D_MODEL = 1024
DIFF_HEADS = 4
HEAD_DIM = 64
DIFF_WIDTH = DIFF_HEADS * 2 * HEAD_DIM
MOBA_HEADS = 8
MOBA_WIDTH = MOBA_HEADS * HEAD_DIM
MOBA_BLOCK = 256
MOBA_TOPK = 3
ROPE_THETA = 500000.0
ROT_DIM = HEAD_DIM // 4
MEM_HEADS = 4
MEM_HEAD_DIM = D_MODEL // MEM_HEADS
N_EXPERTS = 32
TOP_K = 4
D_FF = D_MODEL
SWIGLU_ALPHA = 1.702
SWIGLU_LIMIT = 7.0
LN_EPS = 1e-5
RMS_EPS = 1e-5
DEEPNORM_ALPHA = 2.0 ** 0.25
LAMBDA_INIT = 0.8 - 0.6 * math.exp(0.0)

LANES = 128
ROW_TILE = 8
V7X_VMEM_LIMIT = 56 * 1024 * 1024

_CDT = jnp.bfloat16
_F32 = jnp.float32
_NEG = -1e30

_CB_DQ, _CB_DK, _CB_DV = 0, 4, 8
_CB_MQ, _CB_MK, _CB_MV = 12, 16, 20
PROJ_WIDTH = 3 * (DIFF_WIDTH + MOBA_WIDTH)


def _cparams(sem, vmem=V7X_VMEM_LIMIT):
    return pltpu.CompilerParams(dimension_semantics=sem, vmem_limit_bytes=vmem)


def _dot(a, b):
    return jnp.dot(a, b, preferred_element_type=_F32)


def _dot_nt(a, b):
    return lax.dot_general(a, b, (((1,), (1,)), ((), ())), preferred_element_type=_F32)


def _layer_norm(x, g, b):
    mu = jnp.mean(x, axis=-1, keepdims=True)
    xc = x - mu
    var = jnp.mean(xc * xc, axis=-1, keepdims=True)
    return xc * lax.rsqrt(var + LN_EPS) * g + b


_ROPE_BLOCKS = tuple(range(_CB_DQ, _CB_DV)) + tuple(range(_CB_MQ, _CB_MV))


def _inproj_kernel(x_ref, w_ref, cos_ref, sa_ref, sb_ref, o_ref, km_ref, *, tm):
    x = x_ref[...].astype(_CDT)
    cos, sa, sb = cos_ref[...], sa_ref[...], sb_ref[...]
    seg_w = 4 * LANES
    for seg in range(PROJ_WIDTH // seg_w):
        pseg = _dot(x, w_ref[:, seg * seg_w:(seg + 1) * seg_w])
        for c in range(4):
            cb = seg * 4 + c
            cols = slice(cb * LANES, (cb + 1) * LANES)
            p = pseg[:, c * LANES:(c + 1) * LANES]
            if cb in _ROPE_BLOCKS:
                p = p * cos + pltpu.roll(p, 8, 1) * sa + pltpu.roll(p, LANES - 8, 1) * sb
            o_ref[:, cols] = p.astype(o_ref.dtype)
            if _CB_MK <= cb < _CB_MV:
                kc = slice((cb - _CB_MK) * LANES, (cb - _CB_MK + 1) * LANES)
                for blk in range(tm // MOBA_BLOCK):
                    rows = p[blk * MOBA_BLOCK:(blk + 1) * MOBA_BLOCK, :]
                    km_ref[blk, :, kc] = jnp.sum(rows, axis=0, keepdims=True) * (1.0 / MOBA_BLOCK)


def _rope_tables(seq):
    half = ROT_DIM // 2
    inv_freq = ROPE_THETA ** (-jnp.arange(0, ROT_DIM, 2, dtype=_F32) / ROT_DIM)
    ang = jnp.arange(seq, dtype=_F32)[:, None] * inv_freq[None, :]
    cos, sin = jnp.cos(ang), jnp.sin(ang)
    lane = jnp.arange(LANES) % HEAD_DIM
    first, second = lane < half, (lane >= half) & (lane < ROT_DIM)
    idx = jnp.where(first, lane, jnp.where(second, lane - half, 0))
    cos_t = jnp.where((first | second)[None, :], cos[:, idx], 1.0)
    sa_t = jnp.where(second[None, :], sin[:, idx], 0.0)
    sb_t = jnp.where(first[None, :], -sin[:, idx], 0.0)
    return cos_t, sa_t, sb_t


def _inproj(x2d, w_in, seq, *, tm):
    t = x2d.shape[0]
    cos_t, sa_t, sb_t = _rope_tables(seq)
    n_pos = seq // tm
    tab_spec = pl.BlockSpec((tm, LANES), lambda i: (i % n_pos, 0))
    return pl.pallas_call(
        functools.partial(_inproj_kernel, tm=tm),
        out_shape=(jax.ShapeDtypeStruct((t, PROJ_WIDTH), _CDT),
                   jax.ShapeDtypeStruct((t // MOBA_BLOCK, 1, MOBA_WIDTH), _F32)),
        grid=(t // tm,),
        in_specs=[pl.BlockSpec((tm, D_MODEL), lambda i: (i, 0)),
                  pl.BlockSpec((D_MODEL, PROJ_WIDTH), lambda i: (0, 0)),
                  tab_spec, tab_spec, tab_spec],
        out_specs=(pl.BlockSpec((tm, PROJ_WIDTH), lambda i: (i, 0)),
                   pl.BlockSpec((tm // MOBA_BLOCK, 1, MOBA_WIDTH), lambda i: (i, 0, 0))),
        compiler_params=_cparams(("parallel",)),
        name="inproj",
    )(x2d, w_in, cos_t, sa_t, sb_t)


def _exp2_parts(parts):
    chunks = [s[:, c:c + LANES] for s in parts for c in range(0, s.shape[1], LANES)]
    m = jnp.max(functools.reduce(jnp.maximum, chunks), axis=1, keepdims=True)
    return [jnp.exp2(s - m).astype(_CDT) for s in parts]


def _causal_mask(tq):
    row = lax.broadcasted_iota(jnp.int32, (tq, tq), 0)
    col = lax.broadcasted_iota(jnp.int32, (tq, tq), 1)
    return col <= row


def _diff_attn_kernel(q_ref, k_ref, v_ref, lam_ref, g_ref, o_ref, *, seq, tq):
    lane = lax.broadcasted_iota(jnp.int32, (tq, LANES), 1)
    mask = _causal_mask(tq)
    lv = lam_ref[...]
    lam = (jnp.exp(jnp.sum(lv[0:1] * lv[1:2], axis=1, keepdims=True))
           - jnp.exp(jnp.sum(lv[2:3] * lv[3:4], axis=1, keepdims=True)) + LAMBDA_INIT)
    gain = g_ref[...] * (1.0 - LAMBDA_INIT)

    def with_ones(v):
        return jnp.concatenate([v, jnp.ones_like(v)], axis=1)

    for i in range(seq // tq):
        rows = slice(i * tq, (i + 1) * tq)
        q = q_ref[rows, :]
        zero = jnp.zeros_like(q)
        q1 = jnp.where(lane < HEAD_DIM, q, zero)
        q2 = jnp.where(lane >= HEAD_DIM, q, zero)
        kd = k_ref[rows, :]
        vs = [with_ones(v_ref[rows, :])]
        s1 = [jnp.where(mask, _dot_nt(q1, kd), _NEG)]
        s2 = [jnp.where(mask, _dot_nt(q2, kd), _NEG)]
        if i:
            kp = k_ref[:i * tq, :]
            vs.append(with_ones(v_ref[:i * tq, :]))
            s1.append(_dot_nt(q1, kp))
            s2.append(_dot_nt(q2, kp))
        nums = []
        for parts in (s1, s2):
            nl = functools.reduce(jnp.add, [_dot(e, v) for e, v in zip(_exp2_parts(parts), vs)])
            nums.append(nl[:, :LANES] / nl[:, LANES:LANES + 1])
        o = nums[0] - lam * nums[1]
        o = o * lax.rsqrt(jnp.mean(o * o, axis=1, keepdims=True) + RMS_EPS) * gain
        o_ref[rows, :] = o.astype(o_ref.dtype)


def _diff_attn(proj, lam_vecs, subln_g, batch, seq, *, tq):
    t = batch * seq
    blk = lambda cb: pl.BlockSpec((seq, LANES), lambda b, h: (b, cb + h))
    return pl.pallas_call(
        functools.partial(_diff_attn_kernel, seq=seq, tq=tq),
        out_shape=jax.ShapeDtypeStruct((t, DIFF_WIDTH), _CDT),
        grid=(batch, DIFF_HEADS),
        in_specs=[blk(_CB_DQ), blk(_CB_DK), blk(_CB_DV),
                  pl.BlockSpec((4, HEAD_DIM), lambda b, h: (0, 0)),
                  pl.BlockSpec((1, 2 * HEAD_DIM), lambda b, h: (0, 0))],
        out_specs=pl.BlockSpec((seq, LANES), lambda b, h: (b, h)),
        compiler_params=_cparams(("parallel", "parallel")),
        name="diff_attn",
    )(proj, proj, proj, lam_vecs, subln_g)


def _moba_selection(gate, n_past):
    blk = lax.broadcasted_iota(jnp.int32, gate.shape, 0)
    rank = jnp.zeros(gate.shape, _F32)
    for m_blk in range(n_past):
        gm = gate[m_blk:m_blk + 1, :]
        beats = (gm > gate) | ((gm == gate) & (m_blk < blk))
        rank = rank + jnp.where(beats, 1.0, 0.0)
    return jnp.where((blk < n_past) & (rank < MOBA_TOPK), 1.0, 0.0)


def _moba_kernel(q_ref, k_ref, v_ref, km_ref, o_ref, *, nb):
    tq = MOBA_BLOCK
    lane = lax.broadcasted_iota(jnp.int32, (tq, LANES), 1)
    mask = _causal_mask(tq)
    km = km_ref[...].astype(_CDT)
    in_head = [(lane >= hh * HEAD_DIM) & (lane < (hh + 1) * HEAD_DIM) for hh in range(2)]
    v_ones = [[jnp.where(in_head[hh], v_ref[j * tq:(j + 1) * tq, :], jnp.ones((tq, LANES), _CDT))
               for j in range(nb)] for hh in range(2)]
    for i in range(nb):
        rows = slice(i * tq, (i + 1) * tq)
        q = q_ref[rows, :]
        zero = jnp.zeros_like(q)
        kd = k_ref[rows, :]
        gated = i > MOBA_TOPK
        outs = []
        for hh in range(2):
            qh = jnp.where(in_head[hh], q, zero)
            parts = [jnp.where(mask, _dot_nt(qh, kd), _NEG)]
            if gated:
                sel = _moba_selection(_dot_nt(km, qh), i)
                sel_q = jnp.concatenate([sel, jnp.zeros((LANES - nb, tq), _F32)], axis=0).T
            for j in range(i):
                s = _dot_nt(qh, k_ref[j * tq:(j + 1) * tq, :])
                parts.append(jnp.where(sel_q[:, j:j + 1] > 0.5, s, _NEG) if gated else s)
            es = _exp2_parts(parts)
            acc = _dot(es[0], v_ones[hh][i])
            for j in range(i):
                acc = acc + _dot(es[j + 1], v_ones[hh][j])
            sum_lane = (1 - hh) * HEAD_DIM
            outs.append(acc / acc[:, sum_lane:sum_lane + 1])
        o_ref[rows, :] = jnp.where(in_head[0], outs[0], outs[1]).astype(o_ref.dtype)


def _moba_attn(proj, kmean, batch, seq):
    nb = seq // MOBA_BLOCK
    t = batch * seq
    blk = lambda cb: pl.BlockSpec((seq, LANES), lambda b, p: (b, cb + p))
    return pl.pallas_call(
        functools.partial(_moba_kernel, nb=nb),
        out_shape=jax.ShapeDtypeStruct((t, MOBA_WIDTH), _CDT),
        grid=(batch, MOBA_HEADS // 2),
        in_specs=[blk(_CB_MQ), blk(_CB_MK), blk(_CB_MV),
                  pl.BlockSpec((None, nb, LANES), lambda b, p: (b, 0, p))],
        out_specs=pl.BlockSpec((seq, LANES), lambda b, p: (b, p)),
        compiler_params=_cparams(("parallel", "parallel")),
        name="moba_attn",
    )(proj, proj, proj, kmean)


def _mem_kv_kernel(mem_ref, g_ref, b_ref, w_ref, o_ref):
    mn = _layer_norm(mem_ref[...], g_ref[...], b_ref[...])
    o_ref[...] = _dot(mn.astype(_CDT), w_ref[...]).astype(o_ref.dtype)


def _mem_kv(mem2d, g, b, w_kv, *, tm):
    rows = mem2d.shape[0]
    vec = pl.BlockSpec((1, D_MODEL), lambda i: (0, 0))
    return pl.pallas_call(
        _mem_kv_kernel,
        out_shape=jax.ShapeDtypeStruct((rows, 2 * D_MODEL), _CDT),
        grid=(rows // tm,),
        in_specs=[pl.BlockSpec((tm, D_MODEL), lambda i: (i, 0)), vec, vec,
                  pl.BlockSpec((D_MODEL, 2 * D_MODEL), lambda i: (0, 0))],
        out_specs=pl.BlockSpec((tm, 2 * D_MODEL), lambda i: (i, 0)),
        compiler_params=_cparams(("parallel",)),
        name="mem_kv",
    )(mem2d, g, b, w_kv)


def _store_row_tiles(ref, val):
    n = val.shape[0]
    for c in range(ROW_TILE):
        ref[pl.ds(c, n, stride=ROW_TILE), :] = val[:, c * LANES:(c + 1) * LANES]


def _load_row_tiles(ref, n):
    return jnp.concatenate([ref[pl.ds(c, n, stride=ROW_TILE), :] for c in range(ROW_TILE)], axis=1)


def _post_attn_kernel(x_ref, oa_ref, ob_ref, wout_ref, g1_ref, b1_ref, kv_ref, wq_ref, wo_ref,
                      g2_ref, b2_ref, wr_ref, br_ref, h2t_ref, idx_ref, wt_ref, *, tm):
    mix = _dot(oa_ref[...], wout_ref[:DIFF_WIDTH, :]) + _dot(ob_ref[...], wout_ref[DIFF_WIDTH:, :])
    h1 = _layer_norm(DEEPNORM_ALPHA * x_ref[...] + mix, g1_ref[...], b1_ref[...])

    q = (_dot(h1.astype(_CDT), wq_ref[...]) * (MEM_HEAD_DIM ** -0.5)).astype(_CDT)
    heads = []
    for h in range(MEM_HEADS):
        c0 = h * MEM_HEAD_DIM
        kh = kv_ref[:, c0:c0 + MEM_HEAD_DIM]
        vh = kv_ref[:, D_MODEL + c0:D_MODEL + c0 + MEM_HEAD_DIM]
        s = _dot_nt(q[:, c0:c0 + MEM_HEAD_DIM], kh)
        p = jnp.exp(s - jnp.max(s, axis=1, keepdims=True))
        o = _dot(p.astype(_CDT), vh) / jnp.sum(p, axis=1, keepdims=True)
        heads.append(o.astype(_CDT))
    xatt = _dot(jnp.concatenate(heads, axis=1), wo_ref[...])
    h2 = _layer_norm(DEEPNORM_ALPHA * h1 + xatt, g2_ref[...], b2_ref[...])
    _store_row_tiles(h2t_ref, h2)

    logits = _dot(h2.astype(_CDT), wr_ref[...]) + br_ref[...]
    lane = lax.broadcasted_iota(jnp.int32, (tm, LANES), 1)
    lane_f = lane.astype(_F32)
    vals, idxs = [], []
    work = logits
    for _ in range(TOP_K):
        m = jnp.max(work, axis=1, keepdims=True)
        ik = jnp.min(jnp.where(work == m, lane_f, float(LANES)), axis=1, keepdims=True)
        vals.append(m)
        idxs.append(ik)
        work = jnp.where(lane_f == ik, -jnp.inf, work)
    exps = [jnp.exp(v - vals[0]) for v in vals]
    denom = exps[0] + exps[1] + exps[2] + exps[3]
    idx_w = jnp.zeros((tm, LANES), _F32)
    wt_w = jnp.zeros((tm, LANES), _F32)
    for k in range(TOP_K):
        idx_w = jnp.where(lane == k, idxs[k], idx_w)
        wt_w = jnp.where(lane == k, exps[k] / denom, wt_w)
    idx_ref[...] = idx_w[:, :TOP_K].astype(jnp.int32)
    wt_ref[...] = wt_w[:, :TOP_K]


def _post_attn(x2d, out_a, out_b, w_out, g1, b1, kv, w_q, w_o, g2, b2, w_r, b_r, seq, mem_len, *, tm):
    t = x2d.shape[0]
    per_b = seq // tm
    full = lambda shape: pl.BlockSpec(shape, lambda i: (0,) * len(shape))
    vec = full((1, D_MODEL))
    return pl.pallas_call(
        functools.partial(_post_attn_kernel, tm=tm),
        out_shape=(jax.ShapeDtypeStruct((t * ROW_TILE, LANES), _F32),
                   jax.ShapeDtypeStruct((t, TOP_K), jnp.int32),
                   jax.ShapeDtypeStruct((t, TOP_K), _F32)),
        grid=(t // tm,),
        in_specs=[pl.BlockSpec((tm, D_MODEL), lambda i: (i, 0)),
                  pl.BlockSpec((tm, DIFF_WIDTH), lambda i: (i, 0)),
                  pl.BlockSpec((tm, MOBA_WIDTH), lambda i: (i, 0)),
                  full((D_MODEL, D_MODEL)), vec, vec,
                  pl.BlockSpec((mem_len, 2 * D_MODEL), lambda i: (i // per_b, 0)),
                  full((D_MODEL, D_MODEL)), full((D_MODEL, D_MODEL)), vec, vec,
                  full((D_MODEL, LANES)), full((1, LANES))],
        out_specs=(pl.BlockSpec((tm * ROW_TILE, LANES), lambda i: (i, 0)),
                   pl.BlockSpec((tm, TOP_K), lambda i: (i, 0)),
                   pl.BlockSpec((tm, TOP_K), lambda i: (i, 0))),
        compiler_params=_cparams(("parallel",)),
        name="post_attn",
    )(x2d, out_a, out_b, w_out, g1, b1, kv, w_q, w_o, g2, b2, w_r, b_r)


def _rank_kernel(idx_ref, rank_ref, cnt_ref, carry_ref, *, tm):
    @pl.when(pl.program_id(0) == 0)
    def _init():
        carry_ref[...] = jnp.zeros_like(carry_ref)

    idx = idx_ref[...]
    lane = lax.broadcasted_iota(jnp.int32, (tm, LANES), 1)
    hits = [lane == idx[:, k:k + 1] for k in range(TOP_K)]
    sel = jnp.zeros((tm, LANES), _F32)
    for hit in hits:
        sel = sel + jnp.where(hit, 1.0, 0.0)
    row = lax.broadcasted_iota(jnp.int32, (tm, tm), 0)
    col = lax.broadcasted_iota(jnp.int32, (tm, tm), 1)
    earlier = jnp.where(col < row, 1.0, 0.0).astype(jnp.bfloat16)
    dense = _dot(earlier, sel.astype(jnp.bfloat16)) + carry_ref[...]
    rank_w = jnp.zeros((tm, LANES), _F32)
    for k, hit in enumerate(hits):
        rk = jnp.sum(jnp.where(hit, dense, 0.0), axis=1, keepdims=True)
        rank_w = jnp.where(lane == k, rk, rank_w)
    rank_ref[...] = rank_w[:, :TOP_K].astype(jnp.int32)
    carry_ref[...] = carry_ref[...] + jnp.sum(sel, axis=0, keepdims=True)
    cnt_ref[...] = carry_ref[...].astype(jnp.int32)


def _rank(idx, *, tm):
    t = idx.shape[0]
    return pl.pallas_call(
        functools.partial(_rank_kernel, tm=tm),
        out_shape=(jax.ShapeDtypeStruct((t, TOP_K), jnp.int32),
                   jax.ShapeDtypeStruct((1, LANES), jnp.int32)),
        grid=(t // tm,),
        in_specs=[pl.BlockSpec((tm, TOP_K), lambda i: (i, 0))],
        out_specs=(pl.BlockSpec((tm, TOP_K), lambda i: (i, 0)),
                   pl.BlockSpec((1, LANES), lambda i: (0, 0))),
        scratch_shapes=[pltpu.VMEM((1, LANES), _F32)],
        compiler_params=_cparams(("arbitrary",)),
        name="rank",
    )(idx)


_DMA_UNROLL = 8
_DMA_THREADS = 2


def _for_each_row(n_rows, fn):
    def group(g, c):
        for u in range(_DMA_UNROLL):
            fn(g * _DMA_UNROLL + u, u)
        return c

    lax.fori_loop(0, n_rows // _DMA_UNROLL, group, 0)


def _dispatch_kernel(meta_ref, h_ref, idx_ref, rank_ref, xs_ref, pos_ref, zero_ref, sem, zsem, *, tm):
    n_rows = tm * TOP_K

    def zero_copy(r):
        return pltpu.make_async_copy(zero_ref, xs_ref.at[r], zsem)

    @pl.when(pl.program_id(0) == 0)
    def _zero_fill():
        zero_ref[...] = jnp.zeros_like(zero_ref)

        def per_expert(e, c):
            lo = meta_ref[e] + meta_ref[N_EXPERTS + e]
            hi = meta_ref[e] + meta_ref[2 * N_EXPERTS + e]
            lax.fori_loop(lo, hi, lambda r, c2: (zero_copy(r).start(), c2)[1], 0)
            lax.fori_loop(lo, hi, lambda r, c2: (zero_copy(r).wait(), c2)[1], 0)
            return c

        lax.fori_loop(0, N_EXPERTS, per_expert, 0)

    def row_copy(r, p):
        return pltpu.make_async_copy(h_ref.at[lax.shift_right_logical(r, 2)], xs_ref.at[p], sem)

    def issue(r, u):
        p = meta_ref[idx_ref[r]] + rank_ref[r]
        pos_ref[r] = p
        row_copy(r, p).start(priority=u % _DMA_THREADS)

    _for_each_row(n_rows, issue)
    _for_each_row(n_rows, lambda r, u: row_copy(r, pos_ref[r]).wait())


def _dispatch(meta, h2t, idx_flat, rank_flat, n_sorted_rows, *, tm):
    t = h2t.shape[0]
    n_rows = tm * TOP_K
    smem_blk = pl.BlockSpec((n_rows,), lambda i, meta: (i,), memory_space=pltpu.SMEM)
    return pl.pallas_call(
        functools.partial(_dispatch_kernel, tm=tm),
        out_shape=(jax.ShapeDtypeStruct((n_sorted_rows, ROW_TILE, LANES), _F32),
                   jax.ShapeDtypeStruct((t * TOP_K,), jnp.int32)),
        grid_spec=pltpu.PrefetchScalarGridSpec(
            num_scalar_prefetch=1,
            grid=(t // tm,),
            in_specs=[pl.BlockSpec((tm, ROW_TILE, LANES), lambda i, meta: (i, 0, 0)), smem_blk, smem_blk],
            out_specs=(pl.BlockSpec(memory_space=pl.ANY), smem_blk),
            scratch_shapes=[pltpu.VMEM((ROW_TILE, LANES), _F32),
                            pltpu.SemaphoreType.DMA, pltpu.SemaphoreType.DMA]),
        compiler_params=_cparams(("arbitrary",)),
        name="dispatch",
    )(meta, h2t, idx_flat, rank_flat)


_DEINT = 2 * LANES


def _deinterleave_matrix():
    k = jnp.arange(_DEINT)[:, None]
    n = jnp.arange(_DEINT)[None, :]
    src = jnp.where(n < LANES, 2 * n, 2 * (n - LANES) + 1)
    return (k == src).astype(_CDT)


def _experts_kernel(te_ref, nv_ref, xs_ref, perm_ref, w1_ref, b1g_ref, b1l_ref, w2_ref, b2_ref, y_ref,
                    wg_ref, wl_ref, w2c_ref, *, tm):
    i = pl.program_id(0)
    live = i < nv_ref[0]
    new_expert = (i == 0) | (te_ref[i] != te_ref[jnp.maximum(i - 1, 0)])

    @pl.when(live & new_expert)
    def _stage_weights():
        for c in range(2 * D_FF // _DEINT):
            both = _dot(w1_ref[0, :, c * _DEINT:(c + 1) * _DEINT].astype(_CDT), perm_ref[...])
            wg_ref[:, c * LANES:(c + 1) * LANES] = both[:, :LANES].astype(_CDT)
            wl_ref[:, c * LANES:(c + 1) * LANES] = both[:, LANES:].astype(_CDT)
        w2c_ref[...] = w2_ref[0].astype(_CDT)

    @pl.when(live)
    def _mlp():
        x = _load_row_tiles(xs_ref, tm).astype(_CDT)
        glu = jnp.minimum(_dot(x, wg_ref[...]) + b1g_ref[0], SWIGLU_LIMIT)
        lin = jnp.clip(_dot(x, wl_ref[...]) + b1l_ref[0], -SWIGLU_LIMIT, SWIGLU_LIMIT)
        act = glu * jax.nn.sigmoid(SWIGLU_ALPHA * glu) * (lin + 1.0)
        _store_row_tiles(y_ref, _dot(act.astype(_CDT), w2c_ref[...]) + b2_ref[0])

    @pl.when(jnp.logical_not(live))
    def _unused_tile():
        y_ref[...] = jnp.zeros_like(y_ref)


def _experts(tile_expert, n_valid, xs, w1, b1g, b1l, w2, b2, *, tm):
    n_tiles = xs.shape[0] // (tm * ROW_TILE)
    by_expert = lambda shape: pl.BlockSpec((1,) + shape, lambda i, te, nv: (te[i], 0, 0))
    return pl.pallas_call(
        functools.partial(_experts_kernel, tm=tm),
        out_shape=jax.ShapeDtypeStruct(xs.shape, _F32),
        grid_spec=pltpu.PrefetchScalarGridSpec(
            num_scalar_prefetch=2,
            grid=(n_tiles,),
            in_specs=[pl.BlockSpec((tm * ROW_TILE, LANES), lambda i, te, nv: (jnp.minimum(i, nv[0] - 1), 0)),
                      pl.BlockSpec((_DEINT, _DEINT), lambda i, te, nv: (0, 0)),
                      by_expert((D_MODEL, 2 * D_FF)),
                      by_expert((1, D_FF)), by_expert((1, D_FF)),
                      by_expert((D_FF, D_MODEL)), by_expert((1, D_MODEL))],
            out_specs=pl.BlockSpec((tm * ROW_TILE, LANES), lambda i, te, nv: (i, 0)),
            scratch_shapes=[pltpu.VMEM((D_MODEL, D_FF), _CDT), pltpu.VMEM((D_MODEL, D_FF), _CDT),
                            pltpu.VMEM((D_FF, D_MODEL), _CDT)]),
        compiler_params=_cparams(("arbitrary",)),
        name="experts",
    )(tile_expert, n_valid, xs, _deinterleave_matrix(), w1, b1g, b1l, w2, b2)


def _combine_kernel(pos_ref, nxt_ref, wt_ref, h2t_ref, g_ref, b_ref, ys_ref, o_ref, buf_ref, sem, *, tm):
    n_rows = tm * TOP_K
    i = pl.program_id(0)
    slot = i & 1

    def row_copy(p_ref, r, s):
        tile_rows = pl.ds(pl.multiple_of(lax.shift_right_logical(r, 2) * ROW_TILE, ROW_TILE), ROW_TILE)
        return pltpu.make_async_copy(ys_ref.at[p_ref[r]], buf_ref.at[s, r & (TOP_K - 1), tile_rows], sem.at[s])

    def start_gather(p_ref, s):
        _for_each_row(n_rows, lambda r, u: row_copy(p_ref, r, s).start(priority=u % _DMA_THREADS))

    @pl.when(i == 0)
    def _first():
        start_gather(pos_ref, 0)

    @pl.when(i + 1 < pl.num_programs(0))
    def _prefetch():
        start_gather(nxt_ref, 1 - slot)

    _for_each_row(n_rows, lambda r, u: row_copy(pos_ref, r, slot).wait())
    wt = wt_ref[...]
    ffn = wt[:, 0:1] * _load_row_tiles(buf_ref.at[slot, 0], tm)
    for k in range(1, TOP_K):
        ffn = ffn + wt[:, k:k + 1] * _load_row_tiles(buf_ref.at[slot, k], tm)
    o_ref[...] = _layer_norm(DEEPNORM_ALPHA * _load_row_tiles(h2t_ref, tm) + ffn, g_ref[...], b_ref[...])


def _combine(pos_flat, wts, h2t, g3, b3, ys, *, tm):
    t = h2t.shape[0] // ROW_TILE
    n_steps = t // tm
    vec = pl.BlockSpec((1, D_MODEL), lambda i: (0, 0))
    return pl.pallas_call(
        functools.partial(_combine_kernel, tm=tm),
        out_shape=jax.ShapeDtypeStruct((t, D_MODEL), _F32),
        grid=(n_steps,),
        in_specs=[pl.BlockSpec((tm * TOP_K,), lambda i: (i,), memory_space=pltpu.SMEM),
                  pl.BlockSpec((tm * TOP_K,), lambda i: (jnp.minimum(i + 1, n_steps - 1),), memory_space=pltpu.SMEM),
                  pl.BlockSpec((tm, TOP_K), lambda i: (i, 0)),
                  pl.BlockSpec((tm * ROW_TILE, LANES), lambda i: (i, 0)), vec, vec,
                  pl.BlockSpec(memory_space=pl.ANY)],
        out_specs=pl.BlockSpec((tm, D_MODEL), lambda i: (i, 0)),
        scratch_shapes=[pltpu.VMEM((2, TOP_K, tm * ROW_TILE, LANES), _F32), pltpu.SemaphoreType.DMA((2,))],
        compiler_params=_cparams(("arbitrary",)),
        name="combine",
    )(pos_flat, pos_flat, wts, h2t, g3, b3, ys)


def _tiles(seq):
    big = 512 if seq % 512 == 0 else MOBA_BLOCK
    return dict(inproj=big, attn_q=MOBA_BLOCK, post=big, rank=big, dispatch=big, expert=512, combine=MOBA_BLOCK)


def kernel(x, mem, w_in, diff_lambda_q1, diff_lambda_k1, diff_lambda_q2, diff_lambda_k2, diff_subln_g,
           w_mix_out, ln1_g, ln1_b, mem_ln_g, mem_ln_b, w_mem_q, w_mem_kv, w_mem_o, ln2_g, ln2_b, w_router,
           b_router, w_mlp1, b_mlp1, w_mlp2, b_mlp2, ln3_g, ln3_b):
    batch, seq, d = x.shape
    mem_len = mem.shape[1]
    assert d == D_MODEL and seq % MOBA_BLOCK == 0 and w_in.shape[0] == 1
    t = batch * seq
    ts = _tiles(seq)
    row = lambda v: v.reshape(1, -1).astype(_F32)

    col = jnp.arange(PROJ_WIDTH)
    is_q = (col < DIFF_WIDTH) | ((col >= 3 * DIFF_WIDTH) & (col < 3 * DIFF_WIDTH + MOBA_WIDTH))
    w_in_c = (w_in[0] * jnp.where(is_q, HEAD_DIM ** -0.5 * math.log2(math.e), 1.0)[None, :]).astype(_CDT)
    lam_vecs = jnp.stack([diff_lambda_q1[0], diff_lambda_k1[0], diff_lambda_q2[0], diff_lambda_k2[0]]).astype(_F32)
    w_r = jnp.zeros((D_MODEL, LANES), _F32).at[:, :N_EXPERTS].set(w_router[0]).astype(_CDT)
    b_r = jnp.full((1, LANES), _NEG, _F32).at[0, :N_EXPERTS].set(b_router[0])
    b1g = b_mlp1[0][:, None, 0::2].astype(_F32)
    b1l = b_mlp1[0][:, None, 1::2].astype(_F32)
    b2 = b_mlp2[0][:, None, :].astype(_F32)

    x2d = x.reshape(t, d)
    proj, kmean = _inproj(x2d, w_in_c, seq, tm=ts["inproj"])
    kmean = kmean.reshape(batch, seq // MOBA_BLOCK, MOBA_WIDTH)
    out_a = _diff_attn(proj, lam_vecs, row(diff_subln_g[0]), batch, seq, tq=ts["attn_q"])
    out_b = _moba_attn(proj, kmean, batch, seq)
    kv = _mem_kv(mem.reshape(batch * mem_len, d), row(mem_ln_g), row(mem_ln_b), w_mem_kv[0].astype(_CDT),
                 tm=mem_len)
    h2t, top_idx, top_w = _post_attn(
        x2d, out_a, out_b, w_mix_out[0].astype(_CDT), row(ln1_g[0]), row(ln1_b[0]), kv,
        w_mem_q[0].astype(_CDT), w_mem_o[0].astype(_CDT), row(ln2_g[0]), row(ln2_b[0]), w_r, b_r,
        seq, mem_len, tm=ts["post"])

    rank4, counts = _rank(top_idx, tm=ts["rank"])
    tmx = ts["expert"]
    cnt = counts[0, :N_EXPERTS]
    padded = (cnt + tmx - 1) // tmx * tmx
    ends = jnp.cumsum(padded)
    meta = jnp.concatenate([ends - padded, cnt, padded]).astype(jnp.int32)
    n_sorted_rows = t * TOP_K + N_EXPERTS * tmx
    n_tiles = n_sorted_rows // tmx
    tile_start = jnp.arange(n_tiles, dtype=jnp.int32) * tmx
    tile_expert = jnp.minimum(jnp.sum(tile_start[:, None] >= ends[None, :], axis=1), N_EXPERTS - 1).astype(jnp.int32)
    n_valid = (ends[-1:] // tmx).astype(jnp.int32)

    xs, pos = _dispatch(meta, h2t.reshape(t, ROW_TILE, LANES), top_idx.reshape(-1), rank4.reshape(-1),
                        n_sorted_rows, tm=ts["dispatch"])
    ys = _experts(tile_expert, n_valid, xs.reshape(n_sorted_rows * ROW_TILE, LANES), w_mlp1[0], b1g, b1l,
                  w_mlp2[0], b2, tm=tmx)
    out = _combine(pos, top_w, h2t, row(ln3_g[0]), row(ln3_b[0]),
                   ys.reshape(n_sorted_rows, ROW_TILE, LANES), tm=ts["combine"])
    return out.reshape(batch, seq, d)
```

```python
import functools
import math

import jax
import jax.numpy as jnp
from jax import lax
from jax.experimental import pallas as pl
from jax.experimental.pallas import tpu as pltpu

D_MODEL = 1024
DIFF_HEADS = 4
HEAD_DIM = 64
DIFF_WIDTH = DIFF_HEADS * 2 * HEAD_DIM
MOBA_HEADS = 8
MOBA_WIDTH = MOBA_HEADS * HEAD_DIM
MOBA_BLOCK = 256
MOBA_TOPK = 3
ROPE_THETA = 500000.0
ROT_DIM = HEAD_DIM // 4
MEM_HEADS = 4
MEM_HEAD_DIM = D_MODEL // MEM_HEADS
N_EXPERTS = 32
TOP_K = 4
D_FF = D_MODEL
SWIGLU_ALPHA = 1.702
SWIGLU_LIMIT = 7.0
LN_EPS = 1e-5
RMS_EPS = 1e-5
DEEPNORM_ALPHA = 2.0 ** 0.25
LAMBDA_INIT = 0.8 - 0.6 * math.exp(0.0)

LANES = 128
ROW_TILE = 8
V7X_VMEM_LIMIT = 56 * 1024 * 1024

_CDT = jnp.bfloat16
_F32 = jnp.float32
_NEG = -1e30

_CB_DQ, _CB_DK, _CB_DV = 0, 4, 8
_CB_MQ, _CB_MK, _CB_MV = 12, 16, 20
PROJ_WIDTH = 3 * (DIFF_WIDTH + MOBA_WIDTH)


def _cparams(sem, vmem=V7X_VMEM_LIMIT):
    return pltpu.CompilerParams(dimension_semantics=sem, vmem_limit_bytes=vmem)


def _dot(a, b):
    return jnp.dot(a, b, preferred_element_type=_F32)


def _dot_nt(a, b):
    return lax.dot_general(a, b, (((1,), (1,)), ((), ())), preferred_element_type=_F32)


def _layer_norm(x, g, b):
    mu = jnp.mean(x, axis=-1, keepdims=True)
    xc = x - mu
    var = jnp.mean(xc * xc, axis=-1, keepdims=True)
    return xc * lax.rsqrt(var + LN_EPS) * g + b


_ROPE_BLOCKS = tuple(range(_CB_DQ, _CB_DV)) + tuple(range(_CB_MQ, _CB_MV))


def _inproj_kernel(x_ref, w_ref, cos_ref, sa_ref, sb_ref, o_ref, km_ref, *, tm):
    x = x_ref[...].astype(_CDT)
    cos, sa, sb = cos_ref[...], sa_ref[...], sb_ref[...]
    seg_w = 4 * LANES
    for seg in range(PROJ_WIDTH // seg_w):
        pseg = _dot(x, w_ref[:, seg * seg_w:(seg + 1) * seg_w])
        for c in range(4):
            cb = seg * 4 + c
            cols = slice(cb * LANES, (cb + 1) * LANES)
            p = pseg[:, c * LANES:(c + 1) * LANES]
            if cb in _ROPE_BLOCKS:
                p = p * cos + pltpu.roll(p, 8, 1) * sa + pltpu.roll(p, LANES - 8, 1) * sb
            o_ref[:, cols] = p.astype(o_ref.dtype)
            if _CB_MK <= cb < _CB_MV:
                kc = slice((cb - _CB_MK) * LANES, (cb - _CB_MK + 1) * LANES)
                for blk in range(tm // MOBA_BLOCK):
                    rows = p[blk * MOBA_BLOCK:(blk + 1) * MOBA_BLOCK, :]
                    km_ref[blk, :, kc] = jnp.sum(rows, axis=0, keepdims=True) * (1.0 / MOBA_BLOCK)


def _rope_tables(seq):
    half = ROT_DIM // 2
    inv_freq = ROPE_THETA ** (-jnp.arange(0, ROT_DIM, 2, dtype=_F32) / ROT_DIM)
    ang = jnp.arange(seq, dtype=_F32)[:, None] * inv_freq[None, :]
    cos, sin = jnp.cos(ang), jnp.sin(ang)
    lane = jnp.arange(LANES) % HEAD_DIM
    first, second = lane < half, (lane >= half) & (lane < ROT_DIM)
    idx = jnp.where(first, lane, jnp.where(second, lane - half, 0))
    cos_t = jnp.where((first | second)[None, :], cos[:, idx], 1.0)
    sa_t = jnp.where(second[None, :], sin[:, idx], 0.0)
    sb_t = jnp.where(first[None, :], -sin[:, idx], 0.0)
    return cos_t, sa_t, sb_t


def _inproj(x2d, w_in, seq, *, tm):
    t = x2d.shape[0]
    cos_t, sa_t, sb_t = _rope_tables(seq)
    n_pos = seq // tm
    tab_spec = pl.BlockSpec((tm, LANES), lambda i: (i % n_pos, 0))
    return pl.pallas_call(
        functools.partial(_inproj_kernel, tm=tm),
        out_shape=(jax.ShapeDtypeStruct((t, PROJ_WIDTH), _CDT),
                   jax.ShapeDtypeStruct((t // MOBA_BLOCK, 1, MOBA_WIDTH), _F32)),
        grid=(t // tm,),
        in_specs=[pl.BlockSpec((tm, D_MODEL), lambda i: (i, 0)),
                  pl.BlockSpec((D_MODEL, PROJ_WIDTH), lambda i: (0, 0)),
                  tab_spec, tab_spec, tab_spec],
        out_specs=(pl.BlockSpec((tm, PROJ_WIDTH), lambda i: (i, 0)),
                   pl.BlockSpec((tm // MOBA_BLOCK, 1, MOBA_WIDTH), lambda i: (i, 0, 0))),
        compiler_params=_cparams(("parallel",)),
        name="inproj",
    )(x2d, w_in, cos_t, sa_t, sb_t)


def _exp2_parts(parts):
    chunks = [s[:, c:c + LANES] for s in parts for c in range(0, s.shape[1], LANES)]
    m = jnp.max(functools.reduce(jnp.maximum, chunks), axis=1, keepdims=True)
    return [jnp.exp2(s - m).astype(_CDT) for s in parts]


def _causal_mask(tq):
    row = lax.broadcasted_iota(jnp.int32, (tq, tq), 0)
    col = lax.broadcasted_iota(jnp.int32, (tq, tq), 1)
    return col <= row


def _diff_attn_kernel(q_ref, k_ref, v_ref, lam_ref, g_ref, o_ref, *, seq, tq):
    lane = lax.broadcasted_iota(jnp.int32, (tq, LANES), 1)
    mask = _causal_mask(tq)
    lv = lam_ref[...]
    lam = (jnp.exp(jnp.sum(lv[0:1] * lv[1:2], axis=1, keepdims=True))
           - jnp.exp(jnp.sum(lv[2:3] * lv[3:4], axis=1, keepdims=True)) + LAMBDA_INIT)
    gain = g_ref[...] * (1.0 - LAMBDA_INIT)

    def with_ones(v):
        return jnp.concatenate([v, jnp.ones_like(v)], axis=1)

    for i in range(seq // tq):
        rows = slice(i * tq, (i + 1) * tq)
        q = q_ref[rows, :]
        zero = jnp.zeros_like(q)
        q1 = jnp.where(lane < HEAD_DIM, q, zero)
        q2 = jnp.where(lane >= HEAD_DIM, q, zero)
        kd = k_ref[rows, :]
        vs = [with_ones(v_ref[rows, :])]
        s1 = [jnp.where(mask, _dot_nt(q1, kd), _NEG)]
        s2 = [jnp.where(mask, _dot_nt(q2, kd), _NEG)]
        if i:
            kp = k_ref[:i * tq, :]
            vs.append(with_ones(v_ref[:i * tq, :]))
            s1.append(_dot_nt(q1, kp))
            s2.append(_dot_nt(q2, kp))
        nums = []
        for parts in (s1, s2):
            nl = functools.reduce(jnp.add, [_dot(e, v) for e, v in zip(_exp2_parts(parts), vs)])
            nums.append(nl[:, :LANES] / nl[:, LANES:LANES + 1])
        o = nums[0] - lam * nums[1]
        o = o * lax.rsqrt(jnp.mean(o * o, axis=1, keepdims=True) + RMS_EPS) * gain
        o_ref[rows, :] = o.astype(o_ref.dtype)


def _diff_attn(proj, lam_vecs, subln_g, batch, seq, *, tq):
    t = batch * seq
    blk = lambda cb: pl.BlockSpec((seq, LANES), lambda b, h: (b, cb + h))
    return pl.pallas_call(
        functools.partial(_diff_attn_kernel, seq=seq, tq=tq),
        out_shape=jax.ShapeDtypeStruct((t, DIFF_WIDTH), _CDT),
        grid=(batch, DIFF_HEADS),
        in_specs=[blk(_CB_DQ), blk(_CB_DK), blk(_CB_DV),
                  pl.BlockSpec((4, HEAD_DIM), lambda b, h: (0, 0)),
                  pl.BlockSpec((1, 2 * HEAD_DIM), lambda b, h: (0, 0))],
        out_specs=pl.BlockSpec((seq, LANES), lambda b, h: (b, h)),
        compiler_params=_cparams(("parallel", "parallel")),
        name="diff_attn",
    )(proj, proj, proj, lam_vecs, subln_g)


def _moba_selection(gate, n_past):
    blk = lax.broadcasted_iota(jnp.int32, gate.shape, 0)
    rank = jnp.zeros(gate.shape, _F32)
    for m_blk in range(n_past):
        gm = gate[m_blk:m_blk + 1, :]
        beats = (gm > gate) | ((gm == gate) & (m_blk < blk))
        rank = rank + jnp.where(beats, 1.0, 0.0)
    return jnp.where((blk < n_past) & (rank < MOBA_TOPK), 1.0, 0.0)


def _moba_kernel(q_ref, k_ref, v_ref, km_ref, o_ref, *, nb):
    tq = MOBA_BLOCK
    lane = lax.broadcasted_iota(jnp.int32, (tq, LANES), 1)
    mask = _causal_mask(tq)
    km = km_ref[...].astype(_CDT)
    in_head = [(lane >= hh * HEAD_DIM) & (lane < (hh + 1) * HEAD_DIM) for hh in range(2)]
    v_ones = [[jnp.where(in_head[hh], v_ref[j * tq:(j + 1) * tq, :], jnp.ones((tq, LANES), _CDT))
               for j in range(nb)] for hh in range(2)]
    for i in range(nb):
        rows = slice(i * tq, (i + 1) * tq)
        q = q_ref[rows, :]
        zero = jnp.zeros_like(q)
        kd = k_ref[rows, :]
        gated = i > MOBA_TOPK
        outs = []
        for hh in range(2):
            qh = jnp.where(in_head[hh], q, zero)
            parts = [jnp.where(mask, _dot_nt(qh, kd), _NEG)]
            if gated:
                sel = _moba_selection(_dot_nt(km, qh), i)
                sel_q = jnp.concatenate([sel, jnp.zeros((LANES - nb, tq), _F32)], axis=0).T
            for j in range(i):
                s = _dot_nt(qh, k_ref[j * tq:(j + 1) * tq, :])
                parts.append(jnp.where(sel_q[:, j:j + 1] > 0.5, s, _NEG) if gated else s)
            es = _exp2_parts(parts)
            acc = _dot(es[0], v_ones[hh][i])
            for j in range(i):
                acc = acc + _dot(es[j + 1], v_ones[hh][j])
            sum_lane = (1 - hh) * HEAD_DIM
            outs.append(acc / acc[:, sum_lane:sum_lane + 1])
        o_ref[rows, :] = jnp.where(in_head[0], outs[0], outs[1]).astype(o_ref.dtype)


def _moba_attn(proj, kmean, batch, seq):
    nb = seq // MOBA_BLOCK
    t = batch * seq
    blk = lambda cb: pl.BlockSpec((seq, LANES), lambda b, p: (b, cb + p))
    return pl.pallas_call(
        functools.partial(_moba_kernel, nb=nb),
        out_shape=jax.ShapeDtypeStruct((t, MOBA_WIDTH), _CDT),
        grid=(batch, MOBA_HEADS // 2),
        in_specs=[blk(_CB_MQ), blk(_CB_MK), blk(_CB_MV),
                  pl.BlockSpec((None, nb, LANES), lambda b, p: (b, 0, p))],
        out_specs=pl.BlockSpec((seq, LANES), lambda b, p: (b, p)),
        compiler_params=_cparams(("parallel", "parallel")),
        name="moba_attn",
    )(proj, proj, proj, kmean)


def _mem_kv_kernel(mem_ref, g_ref, b_ref, w_ref, o_ref):
    mn = _layer_norm(mem_ref[...], g_ref[...], b_ref[...])
    o_ref[...] = _dot(mn.astype(_CDT), w_ref[...]).astype(o_ref.dtype)


def _mem_kv(mem2d, g, b, w_kv, *, tm):
    rows = mem2d.shape[0]
    vec = pl.BlockSpec((1, D_MODEL), lambda i: (0, 0))
    return pl.pallas_call(
        _mem_kv_kernel,
        out_shape=jax.ShapeDtypeStruct((rows, 2 * D_MODEL), _CDT),
        grid=(rows // tm,),
        in_specs=[pl.BlockSpec((tm, D_MODEL), lambda i: (i, 0)), vec, vec,
                  pl.BlockSpec((D_MODEL, 2 * D_MODEL), lambda i: (0, 0))],
        out_specs=pl.BlockSpec((tm, 2 * D_MODEL), lambda i: (i, 0)),
        compiler_params=_cparams(("parallel",)),
        name="mem_kv",
    )(mem2d, g, b, w_kv)


def _store_row_tiles(ref, val):
    n = val.shape[0]
    for c in range(ROW_TILE):
        ref[pl.ds(c, n, stride=ROW_TILE), :] = val[:, c * LANES:(c + 1) * LANES]


def _load_row_tiles(ref, n):
    return jnp.concatenate([ref[pl.ds(c, n, stride=ROW_TILE), :] for c in range(ROW_TILE)], axis=1)


def _post_attn_kernel(x_ref, oa_ref, ob_ref, wout_ref, g1_ref, b1_ref, kv_ref, wq_ref, wo_ref,
                      g2_ref, b2_ref, wr_ref, br_ref, h2t_ref, idx_ref, wt_ref, cnt_ref, cnt_acc, *, tm):
    mix = _dot(oa_ref[...], wout_ref[:DIFF_WIDTH, :]) + _dot(ob_ref[...], wout_ref[DIFF_WIDTH:, :])
    h1 = _layer_norm(DEEPNORM_ALPHA * x_ref[...] + mix, g1_ref[...], b1_ref[...])

    q = (_dot(h1.astype(_CDT), wq_ref[...]) * (MEM_HEAD_DIM ** -0.5)).astype(_CDT)
    heads = []
    for h in range(MEM_HEADS):
        c0 = h * MEM_HEAD_DIM
        kh = kv_ref[:, c0:c0 + MEM_HEAD_DIM]
        vh = kv_ref[:, D_MODEL + c0:D_MODEL + c0 + MEM_HEAD_DIM]
        s = _dot_nt(q[:, c0:c0 + MEM_HEAD_DIM], kh)
        p = jnp.exp(s - jnp.max(s, axis=1, keepdims=True))
        o = _dot(p.astype(_CDT), vh) / jnp.sum(p, axis=1, keepdims=True)
        heads.append(o.astype(_CDT))
    xatt = _dot(jnp.concatenate(heads, axis=1), wo_ref[...])
    h2 = _layer_norm(DEEPNORM_ALPHA * h1 + xatt, g2_ref[...], b2_ref[...])
    _store_row_tiles(h2t_ref, h2)

    logits = _dot(h2.astype(_CDT), wr_ref[...]) + br_ref[...]
    lane = lax.broadcasted_iota(jnp.int32, (tm, LANES), 1)
    lane_f = lane.astype(_F32)
    vals, idxs = [], []
    work = logits
    for _ in range(TOP_K):
        m = jnp.max(work, axis=1, keepdims=True)
        ik = jnp.min(jnp.where(work == m, lane_f, float(LANES)), axis=1, keepdims=True)
        vals.append(m)
        idxs.append(ik)
        work = jnp.where(lane_f == ik, -jnp.inf, work)
    exps = [jnp.exp(v - vals[0]) for v in vals]
    denom = exps[0] + exps[1] + exps[2] + exps[3]
    idx_w = jnp.zeros((tm, LANES), _F32)
    wt_w = jnp.zeros((tm, LANES), _F32)
    for k in range(TOP_K):
        idx_w = jnp.where(lane == k, idxs[k], idx_w)
        wt_w = jnp.where(lane == k, exps[k] / denom, wt_w)
    idx_ref[...] = idx_w[:, :TOP_K].astype(jnp.int32)
    wt_ref[...] = wt_w[:, :TOP_K]

    @pl.when(pl.program_id(0) == 0)
    def _init_counts():
        cnt_acc[...] = jnp.zeros_like(cnt_acc)

    chosen = functools.reduce(jnp.add, [jnp.where(lane_f == ik, 1.0, 0.0) for ik in idxs])
    cnt_acc[...] = cnt_acc[...] + jnp.sum(chosen, axis=0, keepdims=True)
    cnt_ref[...] = cnt_acc[...].astype(jnp.int32)


def _post_attn(x2d, out_a, out_b, w_out, g1, b1, kv, w_q, w_o, g2, b2, w_r, b_r, seq, mem_len, *, tm):
    t = x2d.shape[0]
    per_b = seq // tm
    full = lambda shape: pl.BlockSpec(shape, lambda i: (0,) * len(shape))
    vec = full((1, D_MODEL))
    return pl.pallas_call(
        functools.partial(_post_attn_kernel, tm=tm),
        out_shape=(jax.ShapeDtypeStruct((t * ROW_TILE, LANES), _F32),
                   jax.ShapeDtypeStruct((t, TOP_K), jnp.int32),
                   jax.ShapeDtypeStruct((t, TOP_K), _F32),
                   jax.ShapeDtypeStruct((1, LANES), jnp.int32)),
        grid=(t // tm,),
        in_specs=[pl.BlockSpec((tm, D_MODEL), lambda i: (i, 0)),
                  pl.BlockSpec((tm, DIFF_WIDTH), lambda i: (i, 0)),
                  pl.BlockSpec((tm, MOBA_WIDTH), lambda i: (i, 0)),
                  full((D_MODEL, D_MODEL)), vec, vec,
                  pl.BlockSpec((mem_len, 2 * D_MODEL), lambda i: (i // per_b, 0)),
                  full((D_MODEL, D_MODEL)), full((D_MODEL, D_MODEL)), vec, vec,
                  full((D_MODEL, LANES)), full((1, LANES))],
        out_specs=(pl.BlockSpec((tm * ROW_TILE, LANES), lambda i: (i, 0)),
                   pl.BlockSpec((tm, TOP_K), lambda i: (i, 0)),
                   pl.BlockSpec((tm, TOP_K), lambda i: (i, 0)),
                   pl.BlockSpec((1, LANES), lambda i: (0, 0))),
        scratch_shapes=[pltpu.VMEM((1, LANES), _F32)],
        compiler_params=_cparams(("arbitrary",)),
        name="post_attn",
    )(x2d, out_a, out_b, w_out, g1, b1, kv, w_q, w_o, g2, b2, w_r, b_r)


def _rank_kernel(idx_ref, off_ref, pos_ref, next_ref, *, tm):
    @pl.when(pl.program_id(0) == 0)
    def _init():
        next_ref[...] = off_ref[...]

    idx = idx_ref[...]
    lane = lax.broadcasted_iota(jnp.int32, (tm, LANES), 1)
    hits = [lane == idx[:, k:k + 1] for k in range(TOP_K)]
    sel = jnp.zeros((tm, LANES), _F32)
    for hit in hits:
        sel = sel + jnp.where(hit, 1.0, 0.0)
    row = lax.broadcasted_iota(jnp.int32, (tm, tm), 0)
    col = lax.broadcasted_iota(jnp.int32, (tm, tm), 1)
    earlier = jnp.where(col < row, 1.0, 0.0).astype(jnp.bfloat16)
    dense = _dot(earlier, sel.astype(jnp.bfloat16)) + next_ref[...]
    pos_w = jnp.zeros((tm, LANES), _F32)
    for k, hit in enumerate(hits):
        pk = jnp.sum(jnp.where(hit, dense, 0.0), axis=1, keepdims=True)
        pos_w = jnp.where(lane == k, pk, pos_w)
    pos_ref[...] = pos_w[:, :TOP_K].astype(jnp.int32)
    next_ref[...] = next_ref[...] + jnp.sum(sel, axis=0, keepdims=True)


def _rank(idx, offsets, *, tm):
    t = idx.shape[0]
    return pl.pallas_call(
        functools.partial(_rank_kernel, tm=tm),
        out_shape=jax.ShapeDtypeStruct((t, TOP_K), jnp.int32),
        grid=(t // tm,),
        in_specs=[pl.BlockSpec((tm, TOP_K), lambda i: (i, 0)), pl.BlockSpec((1, LANES), lambda i: (0, 0))],
        out_specs=pl.BlockSpec((tm, TOP_K), lambda i: (i, 0)),
        scratch_shapes=[pltpu.VMEM((1, LANES), _F32)],
        compiler_params=_cparams(("arbitrary",)),
        name="rank",
    )(idx, offsets)


_DMA_UNROLL = 8
_DMA_THREADS = 2


def _for_each_row(n_rows, fn):
    tokens_per_group = _DMA_UNROLL // TOP_K

    def group(g, c):
        for u in range(_DMA_UNROLL):
            fn(g * _DMA_UNROLL + u, g * tokens_per_group + u // TOP_K, u % TOP_K, u)
        return c

    lax.fori_loop(0, n_rows // _DMA_UNROLL, group, 0)


def _dispatch_kernel(meta_ref, h_ref, pos_ref, xs_ref, zero_ref, sem, zsem, *, tm):
    n_rows = tm * TOP_K

    def zero_copy(r):
        return pltpu.make_async_copy(zero_ref, xs_ref.at[r], zsem)

    @pl.when(pl.program_id(0) == 0)
    def _zero_fill():
        zero_ref[...] = jnp.zeros_like(zero_ref)

        def per_expert(e, c):
            lo = meta_ref[e] + meta_ref[N_EXPERTS + e]
            hi = meta_ref[e] + meta_ref[2 * N_EXPERTS + e]
            lax.fori_loop(lo, hi, lambda r, c2: (zero_copy(r).start(), c2)[1], 0)
            lax.fori_loop(lo, hi, lambda r, c2: (zero_copy(r).wait(), c2)[1], 0)
            return c

        lax.fori_loop(0, N_EXPERTS, per_expert, 0)

    def row_copy(r, t, k, u):
        return pltpu.make_async_copy(h_ref.at[t], xs_ref.at[pos_ref[r]], sem)

    _for_each_row(n_rows, lambda r, t, k, u: row_copy(r, t, k, u).start(priority=u % _DMA_THREADS))
    _for_each_row(n_rows, lambda r, t, k, u: row_copy(r, t, k, u).wait())


def _dispatch(meta, h2t, pos_flat, n_sorted_rows, *, tm):
    t = h2t.shape[0]
    return pl.pallas_call(
        functools.partial(_dispatch_kernel, tm=tm),
        out_shape=jax.ShapeDtypeStruct((n_sorted_rows, ROW_TILE, LANES), _F32),
        grid_spec=pltpu.PrefetchScalarGridSpec(
            num_scalar_prefetch=1,
            grid=(t // tm,),
            in_specs=[pl.BlockSpec((tm, ROW_TILE, LANES), lambda i, meta: (i, 0, 0)),
                      pl.BlockSpec((tm * TOP_K,), lambda i, meta: (i,), memory_space=pltpu.SMEM)],
            out_specs=pl.BlockSpec(memory_space=pl.ANY),
            scratch_shapes=[pltpu.VMEM((ROW_TILE, LANES), _F32),
                            pltpu.SemaphoreType.DMA, pltpu.SemaphoreType.DMA]),
        compiler_params=_cparams(("arbitrary",)),
        name="dispatch",
    )(meta, h2t, pos_flat)


_DEINT = 2 * LANES


def _deinterleave_matrix():
    k = jnp.arange(_DEINT)[:, None]
    n = jnp.arange(_DEINT)[None, :]
    src = jnp.where(n < LANES, 2 * n, 2 * (n - LANES) + 1)
    return (k == src).astype(_CDT)


def _experts_kernel(te_ref, nv_ref, xs_ref, perm_ref, w1_ref, b1g_ref, b1l_ref, w2_ref, b2_ref, y_ref,
                    wg_ref, wl_ref, w2c_ref, *, tm):
    i = pl.program_id(0)
    live = i < nv_ref[0]
    new_expert = (i == 0) | (te_ref[i] != te_ref[jnp.maximum(i - 1, 0)])

    @pl.when(live & new_expert)
    def _stage_weights():
        for c in range(2 * D_FF // _DEINT):
            both = _dot(w1_ref[0, :, c * _DEINT:(c + 1) * _DEINT].astype(_CDT), perm_ref[...])
            wg_ref[:, c * LANES:(c + 1) * LANES] = both[:, :LANES].astype(_CDT)
            wl_ref[:, c * LANES:(c + 1) * LANES] = both[:, LANES:].astype(_CDT)
        w2c_ref[...] = w2_ref[0].astype(_CDT)

    @pl.when(live)
    def _mlp():
        x = _load_row_tiles(xs_ref, tm).astype(_CDT)
        glu = jnp.minimum(_dot(x, wg_ref[...]) + b1g_ref[0], SWIGLU_LIMIT)
        lin = jnp.clip(_dot(x, wl_ref[...]) + b1l_ref[0], -SWIGLU_LIMIT, SWIGLU_LIMIT)
        act = glu * jax.nn.sigmoid(SWIGLU_ALPHA * glu) * (lin + 1.0)
        _store_row_tiles(y_ref, _dot(act.astype(_CDT), w2c_ref[...]) + b2_ref[0])

    @pl.when(jnp.logical_not(live))
    def _unused_tile():
        y_ref[...] = jnp.zeros_like(y_ref)


def _experts(tile_expert, n_valid, xs, w1, b1g, b1l, w2, b2, *, tm):
    n_tiles = xs.shape[0] // (tm * ROW_TILE)
    by_expert = lambda shape: pl.BlockSpec((1,) + shape, lambda i, te, nv: (te[i], 0, 0))
    return pl.pallas_call(
        functools.partial(_experts_kernel, tm=tm),
        out_shape=jax.ShapeDtypeStruct(xs.shape, _F32),
        grid_spec=pltpu.PrefetchScalarGridSpec(
            num_scalar_prefetch=2,
            grid=(n_tiles,),
            in_specs=[pl.BlockSpec((tm * ROW_TILE, LANES), lambda i, te, nv: (jnp.minimum(i, nv[0] - 1), 0)),
                      pl.BlockSpec((_DEINT, _DEINT), lambda i, te, nv: (0, 0)),
                      by_expert((D_MODEL, 2 * D_FF)),
                      by_expert((1, D_FF)), by_expert((1, D_FF)),
                      by_expert((D_FF, D_MODEL)), by_expert((1, D_MODEL))],
            out_specs=pl.BlockSpec((tm * ROW_TILE, LANES), lambda i, te, nv: (i, 0)),
            scratch_shapes=[pltpu.VMEM((D_MODEL, D_FF), _CDT), pltpu.VMEM((D_MODEL, D_FF), _CDT),
                            pltpu.VMEM((D_FF, D_MODEL), _CDT)]),
        compiler_params=_cparams(("arbitrary",)),
        name="experts",
    )(tile_expert, n_valid, xs, _deinterleave_matrix(), w1, b1g, b1l, w2, b2)


def _combine_kernel(pos_ref, nxt_ref, wt_ref, h2t_ref, g_ref, b_ref, ys_ref, o_ref, buf_ref, sem, *, tm):
    n_rows = tm * TOP_K
    i = pl.program_id(0)
    slot = i & 1

    def row_copy(p_ref, s, r, t, k):
        tile_rows = pl.ds(pl.multiple_of(t * ROW_TILE, ROW_TILE), ROW_TILE)
        return pltpu.make_async_copy(ys_ref.at[p_ref[r]], buf_ref.at[s, k, tile_rows], sem.at[s])

    def start_gather(p_ref, s):
        _for_each_row(n_rows, lambda r, t, k, u: row_copy(p_ref, s, r, t, k).start(priority=u % _DMA_THREADS))

    @pl.when(i == 0)
    def _first():
        start_gather(pos_ref, 0)

    @pl.when(i + 1 < pl.num_programs(0))
    def _prefetch():
        start_gather(nxt_ref, 1 - slot)

    _for_each_row(n_rows, lambda r, t, k, u: row_copy(pos_ref, slot, r, t, k).wait())
    wt = wt_ref[...]
    ffn = wt[:, 0:1] * _load_row_tiles(buf_ref.at[slot, 0], tm)
    for k in range(1, TOP_K):
        ffn = ffn + wt[:, k:k + 1] * _load_row_tiles(buf_ref.at[slot, k], tm)
    o_ref[...] = _layer_norm(DEEPNORM_ALPHA * _load_row_tiles(h2t_ref, tm) + ffn, g_ref[...], b_ref[...])


def _combine(pos_flat, wts, h2t, g3, b3, ys, *, tm):
    t = h2t.shape[0] // ROW_TILE
    n_steps = t // tm
    vec = pl.BlockSpec((1, D_MODEL), lambda i: (0, 0))
    return pl.pallas_call(
        functools.partial(_combine_kernel, tm=tm),
        out_shape=jax.ShapeDtypeStruct((t, D_MODEL), _F32),
        grid=(n_steps,),
        in_specs=[pl.BlockSpec((tm * TOP_K,), lambda i: (i,), memory_space=pltpu.SMEM),
                  pl.BlockSpec((tm * TOP_K,), lambda i: (jnp.minimum(i + 1, n_steps - 1),), memory_space=pltpu.SMEM),
                  pl.BlockSpec((tm, TOP_K), lambda i: (i, 0)),
                  pl.BlockSpec((tm * ROW_TILE, LANES), lambda i: (i, 0)), vec, vec,
                  pl.BlockSpec(memory_space=pl.ANY)],
        out_specs=pl.BlockSpec((tm, D_MODEL), lambda i: (i, 0)),
        scratch_shapes=[pltpu.VMEM((2, TOP_K, tm * ROW_TILE, LANES), _F32), pltpu.SemaphoreType.DMA((2,))],
        compiler_params=_cparams(("arbitrary",)),
        name="combine",
    )(pos_flat, pos_flat, wts, h2t, g3, b3, ys)


def _tiles(seq):
    big = 512 if seq % 512 == 0 else MOBA_BLOCK
    return dict(inproj=big, attn_q=MOBA_BLOCK, post=big, rank=big, dispatch=big, expert=512, combine=MOBA_BLOCK)


def kernel(x, mem, w_in, diff_lambda_q1, diff_lambda_k1, diff_lambda_q2, diff_lambda_k2, diff_subln_g,
           w_mix_out, ln1_g, ln1_b, mem_ln_g, mem_ln_b, w_mem_q, w_mem_kv, w_mem_o, ln2_g, ln2_b, w_router,
           b_router, w_mlp1, b_mlp1, w_mlp2, b_mlp2, ln3_g, ln3_b):
    batch, seq, d = x.shape
    mem_len = mem.shape[1]
    assert d == D_MODEL and seq % MOBA_BLOCK == 0 and w_in.shape[0] == 1
    t = batch * seq
    ts = _tiles(seq)
    row = lambda v: v.reshape(1, -1).astype(_F32)

    col = jnp.arange(PROJ_WIDTH)
    is_q = (col < DIFF_WIDTH) | ((col >= 3 * DIFF_WIDTH) & (col < 3 * DIFF_WIDTH + MOBA_WIDTH))
    w_in_c = (w_in[0] * jnp.where(is_q, HEAD_DIM ** -0.5 * math.log2(math.e), 1.0)[None, :]).astype(_CDT)
    lam_vecs = jnp.stack([diff_lambda_q1[0], diff_lambda_k1[0], diff_lambda_q2[0], diff_lambda_k2[0]]).astype(_F32)
    w_r = jnp.zeros((D_MODEL, LANES), _F32).at[:, :N_EXPERTS].set(w_router[0]).astype(_CDT)
    b_r = jnp.full((1, LANES), _NEG, _F32).at[0, :N_EXPERTS].set(b_router[0])
    b1g = b_mlp1[0][:, None, 0::2].astype(_F32)
    b1l = b_mlp1[0][:, None, 1::2].astype(_F32)
    b2 = b_mlp2[0][:, None, :].astype(_F32)

    x2d = x.reshape(t, d)
    proj, kmean = _inproj(x2d, w_in_c, seq, tm=ts["inproj"])
    kmean = kmean.reshape(batch, seq // MOBA_BLOCK, MOBA_WIDTH)
    out_a = _diff_attn(proj, lam_vecs, row(diff_subln_g[0]), batch, seq, tq=ts["attn_q"])
    out_b = _moba_attn(proj, kmean, batch, seq)
    kv = _mem_kv(mem.reshape(batch * mem_len, d), row(mem_ln_g), row(mem_ln_b), w_mem_kv[0].astype(_CDT),
                 tm=mem_len)
    h2t, top_idx, top_w, counts = _post_attn(
        x2d, out_a, out_b, w_mix_out[0].astype(_CDT), row(ln1_g[0]), row(ln1_b[0]), kv,
        w_mem_q[0].astype(_CDT), w_mem_o[0].astype(_CDT), row(ln2_g[0]), row(ln2_b[0]), w_r, b_r,
        seq, mem_len, tm=ts["post"])

    tmx = ts["expert"]
    cnt = counts[0, :N_EXPERTS]
    padded = (cnt + tmx - 1) // tmx * tmx
    ends = jnp.cumsum(padded)
    offsets = ends - padded
    meta = jnp.concatenate([offsets, cnt, padded]).astype(jnp.int32)
    n_sorted_rows = t * TOP_K + N_EXPERTS * tmx
    n_tiles = n_sorted_rows // tmx
    tile_start = jnp.arange(n_tiles, dtype=jnp.int32) * tmx
    tile_expert = jnp.minimum(jnp.sum(tile_start[:, None] >= ends[None, :], axis=1), N_EXPERTS - 1).astype(jnp.int32)
    n_valid = (ends[-1:] // tmx).astype(jnp.int32)
    offsets_row = jnp.zeros((1, LANES), _F32).at[0, :N_EXPERTS].set(offsets.astype(_F32))
    pos = _rank(top_idx, offsets_row, tm=ts["rank"]).reshape(-1)

    xs = _dispatch(meta, h2t.reshape(t, ROW_TILE, LANES), pos, n_sorted_rows, tm=ts["dispatch"])
    ys = _experts(tile_expert, n_valid, xs.reshape(n_sorted_rows * ROW_TILE, LANES), w_mlp1[0], b1g, b1l,
                  w_mlp2[0], b2, tm=tmx)
    out = _combine(pos, top_w, h2t, row(ln3_g[0]), row(ln3_b[0]),
                   ys.reshape(n_sorted_rows, ROW_TILE, LANES), tm=ts["combine"])
    return out.reshape(batch, seq, d)
```

```python
import functools
import math

import jax
import jax.numpy as jnp
from jax import lax
from jax.experimental import pallas as pl
from jax.experimental.pallas import tpu as pltpu

D_MODEL = 1024
DIFF_HEADS = 4
HEAD_DIM = 64
DIFF_WIDTH = DIFF_HEADS * 2 * HEAD_DIM
MOBA_HEADS = 8
MOBA_WIDTH = MOBA_HEADS * HEAD_DIM
MOBA_BLOCK = 256
MOBA_TOPK = 3
ROPE_THETA = 500000.0
ROT_DIM = HEAD_DIM // 4
MEM_HEADS = 4
MEM_HEAD_DIM = D_MODEL // MEM_HEADS
N_EXPERTS = 32
TOP_K = 4
D_FF = D_MODEL
SWIGLU_ALPHA = 1.702
SWIGLU_LIMIT = 7.0
LN_EPS = 1e-5
RMS_EPS = 1e-5
DEEPNORM_ALPHA = 2.0 ** 0.25
LAMBDA_INIT = 0.8 - 0.6 * math.exp(0.0)

LANES = 128
ROW_TILE = 8
V7X_VMEM_LIMIT = 56 * 1024 * 1024

_CDT = jnp.bfloat16
_F32 = jnp.float32
_NEG = -1e30

_CB_DQ, _CB_DK, _CB_DV = 0, 4, 8
_CB_MQ, _CB_MK, _CB_MV = 12, 16, 20
PROJ_WIDTH = 3 * (DIFF_WIDTH + MOBA_WIDTH)


def _cparams(sem, vmem=V7X_VMEM_LIMIT):
    return pltpu.CompilerParams(dimension_semantics=sem, vmem_limit_bytes=vmem)


def _dot(a, b):
    return jnp.dot(a, b, preferred_element_type=_F32)


def _dot_nt(a, b):
    return lax.dot_general(a, b, (((1,), (1,)), ((), ())), preferred_element_type=_F32)


def _layer_norm(x, g, b):
    mu = jnp.mean(x, axis=-1, keepdims=True)
    xc = x - mu
    var = jnp.mean(xc * xc, axis=-1, keepdims=True)
    return xc * lax.rsqrt(var + LN_EPS) * g + b


_ROPE_BLOCKS = tuple(range(_CB_DQ, _CB_DV)) + tuple(range(_CB_MQ, _CB_MV))


def _inproj_kernel(x_ref, w_ref, cos_ref, sa_ref, sb_ref, o_ref, km_ref, *, tm):
    x = x_ref[...].astype(_CDT)
    cos, sa, sb = cos_ref[...], sa_ref[...], sb_ref[...]
    seg_w = 4 * LANES
    for seg in range(PROJ_WIDTH // seg_w):
        pseg = _dot(x, w_ref[:, seg * seg_w:(seg + 1) * seg_w])
        for c in range(4):
            cb = seg * 4 + c
            cols = slice(cb * LANES, (cb + 1) * LANES)
            p = pseg[:, c * LANES:(c + 1) * LANES]
            if cb in _ROPE_BLOCKS:
                p = p * cos + pltpu.roll(p, 8, 1) * sa + pltpu.roll(p, LANES - 8, 1) * sb
            o_ref[:, cols] = p.astype(o_ref.dtype)
            if _CB_MK <= cb < _CB_MV:
                kc = slice((cb - _CB_MK) * LANES, (cb - _CB_MK + 1) * LANES)
                for blk in range(tm // MOBA_BLOCK):
                    rows = p[blk * MOBA_BLOCK:(blk + 1) * MOBA_BLOCK, :]
                    km_ref[blk, :, kc] = jnp.sum(rows, axis=0, keepdims=True) * (1.0 / MOBA_BLOCK)


def _rope_tables(seq):
    half = ROT_DIM // 2
    inv_freq = ROPE_THETA ** (-jnp.arange(0, ROT_DIM, 2, dtype=_F32) / ROT_DIM)
    ang = jnp.arange(seq, dtype=_F32)[:, None] * inv_freq[None, :]
    cos, sin = jnp.cos(ang), jnp.sin(ang)
    lane = jnp.arange(LANES) % HEAD_DIM
    first, second = lane < half, (lane >= half) & (lane < ROT_DIM)
    idx = jnp.where(first, lane, jnp.where(second, lane - half, 0))
    cos_t = jnp.where((first | second)[None, :], cos[:, idx], 1.0)
    sa_t = jnp.where(second[None, :], sin[:, idx], 0.0)
    sb_t = jnp.where(first[None, :], -sin[:, idx], 0.0)
    return cos_t, sa_t, sb_t


def _inproj(x2d, w_in, seq, *, tm):
    t = x2d.shape[0]
    cos_t, sa_t, sb_t = _rope_tables(seq)
    n_pos = seq // tm
    tab_spec = pl.BlockSpec((tm, LANES), lambda i: (i % n_pos, 0))
    return pl.pallas_call(
        functools.partial(_inproj_kernel, tm=tm),
        out_shape=(jax.ShapeDtypeStruct((t, PROJ_WIDTH), _CDT),
                   jax.ShapeDtypeStruct((t // MOBA_BLOCK, 1, MOBA_WIDTH), _F32)),
        grid=(t // tm,),
        in_specs=[pl.BlockSpec((tm, D_MODEL), lambda i: (i, 0)),
                  pl.BlockSpec((D_MODEL, PROJ_WIDTH), lambda i: (0, 0)),
                  tab_spec, tab_spec, tab_spec],
        out_specs=(pl.BlockSpec((tm, PROJ_WIDTH), lambda i: (i, 0)),
                   pl.BlockSpec((tm // MOBA_BLOCK, 1, MOBA_WIDTH), lambda i: (i, 0, 0))),
        compiler_params=_cparams(("parallel",)),
        name="inproj",
    )(x2d, w_in, cos_t, sa_t, sb_t)


def _exp2_parts(parts):
    chunks = [s[:, c:c + LANES] for s in parts for c in range(0, s.shape[1], LANES)]
    m = jnp.max(functools.reduce(jnp.maximum, chunks), axis=1, keepdims=True)
    return [jnp.exp2(s - m).astype(_CDT) for s in parts]


_SCORES_AHEAD = 1


def _causal_mask(tq):
    row = lax.broadcasted_iota(jnp.int32, (tq, tq), 0)
    col = lax.broadcasted_iota(jnp.int32, (tq, tq), 1)
    return col <= row


def _diff_attn_kernel(q_ref, k_ref, v_ref, lam_ref, g_ref, o_ref, *, seq, tq):
    lane = lax.broadcasted_iota(jnp.int32, (tq, LANES), 1)
    mask = _causal_mask(tq)
    lv = lam_ref[...]
    lam = (jnp.exp(jnp.sum(lv[0:1] * lv[1:2], axis=1, keepdims=True))
           - jnp.exp(jnp.sum(lv[2:3] * lv[3:4], axis=1, keepdims=True)) + LAMBDA_INIT)
    gain = g_ref[...] * (1.0 - LAMBDA_INIT)

    def with_ones(v):
        return jnp.concatenate([v, jnp.ones_like(v)], axis=1)

    def scores(i):
        rows = slice(i * tq, (i + 1) * tq)
        q = q_ref[rows, :]
        zero = jnp.zeros_like(q)
        q1 = jnp.where(lane < HEAD_DIM, q, zero)
        q2 = jnp.where(lane >= HEAD_DIM, q, zero)
        kd = k_ref[rows, :]
        s1 = [jnp.where(mask, _dot_nt(q1, kd), _NEG)]
        s2 = [jnp.where(mask, _dot_nt(q2, kd), _NEG)]
        if i:
            kp = k_ref[:i * tq, :]
            s1.append(_dot_nt(q1, kp))
            s2.append(_dot_nt(q2, kp))
        return s1, s2

    def finish(i, s1, s2):
        rows = slice(i * tq, (i + 1) * tq)
        vs = [with_ones(v_ref[rows, :])]
        if i:
            vs.append(with_ones(v_ref[:i * tq, :]))
        nums = []
        for parts in (s1, s2):
            nl = functools.reduce(jnp.add, [_dot(e, v) for e, v in zip(_exp2_parts(parts), vs)])
            nums.append(nl[:, :LANES] / nl[:, LANES:LANES + 1])
        o = nums[0] - lam * nums[1]
        o = o * lax.rsqrt(jnp.mean(o * o, axis=1, keepdims=True) + RMS_EPS) * gain
        o_ref[rows, :] = o.astype(o_ref.dtype)

    n_tiles = seq // tq
    pending = [scores(i) for i in range(min(_SCORES_AHEAD, n_tiles))]
    for i in range(n_tiles):
        if i + _SCORES_AHEAD < n_tiles:
            pending.append(scores(i + _SCORES_AHEAD))
        finish(i, *pending.pop(0))


def _diff_attn(proj, lam_vecs, subln_g, batch, seq, *, tq):
    t = batch * seq
    blk = lambda cb: pl.BlockSpec((seq, LANES), lambda b, h: (b, cb + h))
    return pl.pallas_call(
        functools.partial(_diff_attn_kernel, seq=seq, tq=tq),
        out_shape=jax.ShapeDtypeStruct((t, DIFF_WIDTH), _CDT),
        grid=(batch, DIFF_HEADS),
        in_specs=[blk(_CB_DQ), blk(_CB_DK), blk(_CB_DV),
                  pl.BlockSpec((4, HEAD_DIM), lambda b, h: (0, 0)),
                  pl.BlockSpec((1, 2 * HEAD_DIM), lambda b, h: (0, 0))],
        out_specs=pl.BlockSpec((seq, LANES), lambda b, h: (b, h)),
        compiler_params=_cparams(("parallel", "parallel")),
        name="diff_attn",
    )(proj, proj, proj, lam_vecs, subln_g)


def _moba_selection(gate, n_past):
    blk = lax.broadcasted_iota(jnp.int32, gate.shape, 0)
    rank = jnp.zeros(gate.shape, _F32)
    for m_blk in range(n_past):
        gm = gate[m_blk:m_blk + 1, :]
        beats = (gm > gate) | ((gm == gate) & (m_blk < blk))
        rank = rank + jnp.where(beats, 1.0, 0.0)
    return jnp.where((blk < n_past) & (rank < MOBA_TOPK), 1.0, 0.0)


def _moba_kernel(q_ref, k_ref, v_ref, km_ref, o_ref, *, nb):
    tq = MOBA_BLOCK
    lane = lax.broadcasted_iota(jnp.int32, (tq, LANES), 1)
    mask = _causal_mask(tq)
    km = km_ref[...].astype(_CDT)
    in_head = [(lane >= hh * HEAD_DIM) & (lane < (hh + 1) * HEAD_DIM) for hh in range(2)]
    v_ones = [[jnp.where(in_head[hh], v_ref[j * tq:(j + 1) * tq, :], jnp.ones((tq, LANES), _CDT))
               for j in range(nb)] for hh in range(2)]
    def scores(i, hh):
        rows = slice(i * tq, (i + 1) * tq)
        q = q_ref[rows, :]
        qh = jnp.where(in_head[hh], q, jnp.zeros_like(q))
        parts = [jnp.where(mask, _dot_nt(qh, k_ref[rows, :]), _NEG)]
        gated = i > MOBA_TOPK
        if gated:
            sel = _moba_selection(_dot_nt(km, qh), i)
            sel_q = jnp.concatenate([sel, jnp.zeros((LANES - nb, tq), _F32)], axis=0).T
        for j in range(i):
            s = _dot_nt(qh, k_ref[j * tq:(j + 1) * tq, :])
            parts.append(jnp.where(sel_q[:, j:j + 1] > 0.5, s, _NEG) if gated else s)
        return parts

    def finish(i, hh, parts):
        es = _exp2_parts(parts)
        acc = _dot(es[0], v_ones[hh][i])
        for j in range(i):
            acc = acc + _dot(es[j + 1], v_ones[hh][j])
        sum_lane = (1 - hh) * HEAD_DIM
        return acc / acc[:, sum_lane:sum_lane + 1]

    units = [(i, hh) for i in range(nb) for hh in range(2)]
    pending = [scores(*u) for u in units[:_SCORES_AHEAD]]
    outs = {}
    for n, (i, hh) in enumerate(units):
        if n + _SCORES_AHEAD < len(units):
            pending.append(scores(*units[n + _SCORES_AHEAD]))
        outs[hh] = finish(i, hh, pending.pop(0))
        if hh == 1:
            o_ref[i * tq:(i + 1) * tq, :] = jnp.where(in_head[0], outs[0], outs[1]).astype(o_ref.dtype)


def _moba_attn(proj, kmean, batch, seq):
    nb = seq // MOBA_BLOCK
    t = batch * seq
    blk = lambda cb: pl.BlockSpec((seq, LANES), lambda b, p: (b, cb + p))
    return pl.pallas_call(
        functools.partial(_moba_kernel, nb=nb),
        out_shape=jax.ShapeDtypeStruct((t, MOBA_WIDTH), _CDT),
        grid=(batch, MOBA_HEADS // 2),
        in_specs=[blk(_CB_MQ), blk(_CB_MK), blk(_CB_MV),
                  pl.BlockSpec((None, nb, LANES), lambda b, p: (b, 0, p))],
        out_specs=pl.BlockSpec((seq, LANES), lambda b, p: (b, p)),
        compiler_params=_cparams(("parallel", "parallel")),
        name="moba_attn",
    )(proj, proj, proj, kmean)


def _mem_kv_kernel(mem_ref, g_ref, b_ref, w_ref, o_ref):
    mn = _layer_norm(mem_ref[...], g_ref[...], b_ref[...])
    o_ref[...] = _dot(mn.astype(_CDT), w_ref[...]).astype(o_ref.dtype)


def _mem_kv(mem2d, g, b, w_kv, *, tm):
    rows = mem2d.shape[0]
    vec = pl.BlockSpec((1, D_MODEL), lambda i: (0, 0))
    return pl.pallas_call(
        _mem_kv_kernel,
        out_shape=jax.ShapeDtypeStruct((rows, 2 * D_MODEL), _CDT),
        grid=(rows // tm,),
        in_specs=[pl.BlockSpec((tm, D_MODEL), lambda i: (i, 0)), vec, vec,
                  pl.BlockSpec((D_MODEL, 2 * D_MODEL), lambda i: (0, 0))],
        out_specs=pl.BlockSpec((tm, 2 * D_MODEL), lambda i: (i, 0)),
        compiler_params=_cparams(("parallel",)),
        name="mem_kv",
    )(mem2d, g, b, w_kv)


def _store_row_tiles(ref, val):
    n = val.shape[0]
    for c in range(ROW_TILE):
        ref[pl.ds(c, n, stride=ROW_TILE), :] = val[:, c * LANES:(c + 1) * LANES]


def _load_row_tiles(ref, n):
    return jnp.concatenate([ref[pl.ds(c, n, stride=ROW_TILE), :] for c in range(ROW_TILE)], axis=1)


def _post_attn_kernel(x_ref, oa_ref, ob_ref, wout_ref, g1_ref, b1_ref, kv_ref, wq_ref, wo_ref,
                      g2_ref, b2_ref, wr_ref, br_ref, h2t_ref, idx_ref, wt_ref, cnt_ref, cnt_acc, *, tm):
    mix = _dot(oa_ref[...], wout_ref[:DIFF_WIDTH, :]) + _dot(ob_ref[...], wout_ref[DIFF_WIDTH:, :])
    h1 = _layer_norm(DEEPNORM_ALPHA * x_ref[...] + mix, g1_ref[...], b1_ref[...])

    q = (_dot(h1.astype(_CDT), wq_ref[...]) * (MEM_HEAD_DIM ** -0.5)).astype(_CDT)
    heads = []
    for h in range(MEM_HEADS):
        c0 = h * MEM_HEAD_DIM
        kh = kv_ref[:, c0:c0 + MEM_HEAD_DIM]
        vh = kv_ref[:, D_MODEL + c0:D_MODEL + c0 + MEM_HEAD_DIM]
        s = _dot_nt(q[:, c0:c0 + MEM_HEAD_DIM], kh)
        p = jnp.exp(s - jnp.max(s, axis=1, keepdims=True))
        o = _dot(p.astype(_CDT), vh) / jnp.sum(p, axis=1, keepdims=True)
        heads.append(o.astype(_CDT))
    xatt = _dot(jnp.concatenate(heads, axis=1), wo_ref[...])
    h2 = _layer_norm(DEEPNORM_ALPHA * h1 + xatt, g2_ref[...], b2_ref[...])
    _store_row_tiles(h2t_ref, h2)

    logits = _dot(h2.astype(_CDT), wr_ref[...]) + br_ref[...]
    lane = lax.broadcasted_iota(jnp.int32, (tm, LANES), 1)
    lane_f = lane.astype(_F32)
    vals, idxs = [], []
    work = logits
    for _ in range(TOP_K):
        m = jnp.max(work, axis=1, keepdims=True)
        ik = jnp.min(jnp.where(work == m, lane_f, float(LANES)), axis=1, keepdims=True)
        vals.append(m)
        idxs.append(ik)
        work = jnp.where(lane_f == ik, -jnp.inf, work)
    exps = [jnp.exp(v - vals[0]) for v in vals]
    denom = exps[0] + exps[1] + exps[2] + exps[3]
    idx_w = jnp.zeros((tm, LANES), _F32)
    wt_w = jnp.zeros((tm, LANES), _F32)
    for k in range(TOP_K):
        idx_w = jnp.where(lane == k, idxs[k], idx_w)
        wt_w = jnp.where(lane == k, exps[k] / denom, wt_w)
    idx_ref[...] = idx_w[:, :TOP_K].astype(jnp.int32)
    wt_ref[...] = wt_w[:, :TOP_K]

    @pl.when(pl.program_id(0) == 0)
    def _init_counts():
        cnt_acc[...] = jnp.zeros_like(cnt_acc)

    chosen = functools.reduce(jnp.add, [jnp.where(lane_f == ik, 1.0, 0.0) for ik in idxs])
    cnt_acc[...] = cnt_acc[...] + jnp.sum(chosen, axis=0, keepdims=True)
    cnt_ref[...] = cnt_acc[...].astype(jnp.int32)


def _post_attn(x2d, out_a, out_b, w_out, g1, b1, kv, w_q, w_o, g2, b2, w_r, b_r, seq, mem_len, *, tm):
    t = x2d.shape[0]
    per_b = seq // tm
    full = lambda shape: pl.BlockSpec(shape, lambda i: (0,) * len(shape))
    vec = full((1, D_MODEL))
    return pl.pallas_call(
        functools.partial(_post_attn_kernel, tm=tm),
        out_shape=(jax.ShapeDtypeStruct((t * ROW_TILE, LANES), _F32),
                   jax.ShapeDtypeStruct((t, TOP_K), jnp.int32),
                   jax.ShapeDtypeStruct((t, TOP_K), _F32),
                   jax.ShapeDtypeStruct((1, LANES), jnp.int32)),
        grid=(t // tm,),
        in_specs=[pl.BlockSpec((tm, D_MODEL), lambda i: (i, 0)),
                  pl.BlockSpec((tm, DIFF_WIDTH), lambda i: (i, 0)),
                  pl.BlockSpec((tm, MOBA_WIDTH), lambda i: (i, 0)),
                  full((D_MODEL, D_MODEL)), vec, vec,
                  pl.BlockSpec((mem_len, 2 * D_MODEL), lambda i: (i // per_b, 0)),
                  full((D_MODEL, D_MODEL)), full((D_MODEL, D_MODEL)), vec, vec,
                  full((D_MODEL, LANES)), full((1, LANES))],
        out_specs=(pl.BlockSpec((tm * ROW_TILE, LANES), lambda i: (i, 0)),
                   pl.BlockSpec((tm, TOP_K), lambda i: (i, 0)),
                   pl.BlockSpec((tm, TOP_K), lambda i: (i, 0)),
                   pl.BlockSpec((1, LANES), lambda i: (0, 0))),
        scratch_shapes=[pltpu.VMEM((1, LANES), _F32)],
        compiler_params=_cparams(("arbitrary",)),
        name="post_attn",
    )(x2d, out_a, out_b, w_out, g1, b1, kv, w_q, w_o, g2, b2, w_r, b_r)


def _rank_kernel(idx_ref, off_ref, pos_ref, next_ref, *, tm):
    @pl.when(pl.program_id(0) == 0)
    def _init():
        next_ref[...] = off_ref[...]

    idx = idx_ref[...]
    lane = lax.broadcasted_iota(jnp.int32, (tm, LANES), 1)
    hits = [lane == idx[:, k:k + 1] for k in range(TOP_K)]
    sel = jnp.zeros((tm, LANES), _F32)
    for hit in hits:
        sel = sel + jnp.where(hit, 1.0, 0.0)
    row = lax.broadcasted_iota(jnp.int32, (tm, tm), 0)
    col = lax.broadcasted_iota(jnp.int32, (tm, tm), 1)
    earlier = jnp.where(col < row, 1.0, 0.0).astype(jnp.bfloat16)
    dense = _dot(earlier, sel.astype(jnp.bfloat16)) + next_ref[...]
    pos_w = jnp.zeros((tm, LANES), _F32)
    for k, hit in enumerate(hits):
        pk = jnp.sum(jnp.where(hit, dense, 0.0), axis=1, keepdims=True)
        pos_w = jnp.where(lane == k, pk, pos_w)
    pos_ref[...] = pos_w[:, :TOP_K].astype(jnp.int32)
    next_ref[...] = next_ref[...] + jnp.sum(sel, axis=0, keepdims=True)


def _rank(idx, offsets, *, tm):
    t = idx.shape[0]
    return pl.pallas_call(
        functools.partial(_rank_kernel, tm=tm),
        out_shape=jax.ShapeDtypeStruct((t, TOP_K), jnp.int32),
        grid=(t // tm,),
        in_specs=[pl.BlockSpec((tm, TOP_K), lambda i: (i, 0)), pl.BlockSpec((1, LANES), lambda i: (0, 0))],
        out_specs=pl.BlockSpec((tm, TOP_K), lambda i: (i, 0)),
        scratch_shapes=[pltpu.VMEM((1, LANES), _F32)],
        compiler_params=_cparams(("arbitrary",)),
        name="rank",
    )(idx, offsets)


_DMA_UNROLL = 8
_DMA_THREADS = 2


def _for_each_row(n_rows, fn):
    tokens_per_group = _DMA_UNROLL // TOP_K

    def group(g, c):
        for u in range(_DMA_UNROLL):
            fn(g * _DMA_UNROLL + u, g * tokens_per_group + u // TOP_K, u % TOP_K, u)
        return c

    lax.fori_loop(0, n_rows // _DMA_UNROLL, group, 0)


def _dispatch_kernel(meta_ref, h_ref, pos_ref, xs_ref, zero_ref, sem, zsem, *, tm):
    n_rows = tm * TOP_K

    def zero_copy(r):
        return pltpu.make_async_copy(zero_ref, xs_ref.at[r], zsem)

    @pl.when(pl.program_id(0) == 0)
    def _zero_fill():
        zero_ref[...] = jnp.zeros_like(zero_ref)

        def per_expert(e, c):
            lo = meta_ref[e] + meta_ref[N_EXPERTS + e]
            hi = meta_ref[e] + meta_ref[2 * N_EXPERTS + e]
            lax.fori_loop(lo, hi, lambda r, c2: (zero_copy(r).start(), c2)[1], 0)
            lax.fori_loop(lo, hi, lambda r, c2: (zero_copy(r).wait(), c2)[1], 0)
            return c

        lax.fori_loop(0, N_EXPERTS, per_expert, 0)

    def row_copy(r, t, k, u):
        return pltpu.make_async_copy(h_ref.at[t], xs_ref.at[pos_ref[r]], sem)

    _for_each_row(n_rows, lambda r, t, k, u: row_copy(r, t, k, u).start(priority=u % _DMA_THREADS))
    _for_each_row(n_rows, lambda r, t, k, u: row_copy(r, t, k, u).wait())


def _dispatch(meta, h2t, pos_flat, n_sorted_rows, *, tm):
    t = h2t.shape[0]
    return pl.pallas_call(
        functools.partial(_dispatch_kernel, tm=tm),
        out_shape=jax.ShapeDtypeStruct((n_sorted_rows, ROW_TILE, LANES), _F32),
        grid_spec=pltpu.PrefetchScalarGridSpec(
            num_scalar_prefetch=1,
            grid=(t // tm,),
            in_specs=[pl.BlockSpec((tm, ROW_TILE, LANES), lambda i, meta: (i, 0, 0)),
                      pl.BlockSpec((tm * TOP_K,), lambda i, meta: (i,), memory_space=pltpu.SMEM)],
            out_specs=pl.BlockSpec(memory_space=pl.ANY),
            scratch_shapes=[pltpu.VMEM((ROW_TILE, LANES), _F32),
                            pltpu.SemaphoreType.DMA, pltpu.SemaphoreType.DMA]),
        compiler_params=_cparams(("arbitrary",)),
        name="dispatch",
    )(meta, h2t, pos_flat)


_DEINT = 2 * LANES


def _deinterleave_matrix():
    k = jnp.arange(_DEINT)[:, None]
    n = jnp.arange(_DEINT)[None, :]
    src = jnp.where(n < LANES, 2 * n, 2 * (n - LANES) + 1)
    return (k == src).astype(_CDT)


def _experts_kernel(te_ref, nv_ref, xs_ref, perm_ref, w1_ref, b1g_ref, b1l_ref, w2_ref, b2_ref, y_ref,
                    wg_ref, wl_ref, w2c_ref, *, tm):
    i = pl.program_id(0)
    live = i < nv_ref[0]
    new_expert = (i == 0) | (te_ref[i] != te_ref[jnp.maximum(i - 1, 0)])

    @pl.when(live & new_expert)
    def _stage_weights():
        for c in range(2 * D_FF // _DEINT):
            both = _dot(w1_ref[0, :, c * _DEINT:(c + 1) * _DEINT].astype(_CDT), perm_ref[...])
            wg_ref[:, c * LANES:(c + 1) * LANES] = both[:, :LANES].astype(_CDT)
            wl_ref[:, c * LANES:(c + 1) * LANES] = both[:, LANES:].astype(_CDT)
        w2c_ref[...] = w2_ref[0].astype(_CDT)

    @pl.when(live)
    def _mlp():
        x = _load_row_tiles(xs_ref, tm).astype(_CDT)
        glu = jnp.minimum(_dot(x, wg_ref[...]) + b1g_ref[0], SWIGLU_LIMIT)
        lin = jnp.clip(_dot(x, wl_ref[...]) + b1l_ref[0], -SWIGLU_LIMIT, SWIGLU_LIMIT)
        act = glu * jax.nn.sigmoid(SWIGLU_ALPHA * glu) * (lin + 1.0)
        _store_row_tiles(y_ref, _dot(act.astype(_CDT), w2c_ref[...]) + b2_ref[0])

    @pl.when(jnp.logical_not(live))
    def _unused_tile():
        y_ref[...] = jnp.zeros_like(y_ref)


def _experts(tile_expert, n_valid, xs, w1, b1g, b1l, w2, b2, *, tm):
    n_tiles = xs.shape[0] // (tm * ROW_TILE)
    by_expert = lambda shape: pl.BlockSpec((1,) + shape, lambda i, te, nv: (te[i], 0, 0))
    return pl.pallas_call(
        functools.partial(_experts_kernel, tm=tm),
        out_shape=jax.ShapeDtypeStruct(xs.shape, _F32),
        grid_spec=pltpu.PrefetchScalarGridSpec(
            num_scalar_prefetch=2,
            grid=(n_tiles,),
            in_specs=[pl.BlockSpec((tm * ROW_TILE, LANES), lambda i, te, nv: (jnp.minimum(i, nv[0] - 1), 0)),
                      pl.BlockSpec((_DEINT, _DEINT), lambda i, te, nv: (0, 0)),
                      by_expert((D_MODEL, 2 * D_FF)),
                      by_expert((1, D_FF)), by_expert((1, D_FF)),
                      by_expert((D_FF, D_MODEL)), by_expert((1, D_MODEL))],
            out_specs=pl.BlockSpec((tm * ROW_TILE, LANES), lambda i, te, nv: (i, 0)),
            scratch_shapes=[pltpu.VMEM((D_MODEL, D_FF), _CDT), pltpu.VMEM((D_MODEL, D_FF), _CDT),
                            pltpu.VMEM((D_FF, D_MODEL), _CDT)]),
        compiler_params=_cparams(("arbitrary",)),
        name="experts",
    )(tile_expert, n_valid, xs, _deinterleave_matrix(), w1, b1g, b1l, w2, b2)


def _combine_kernel(pos_ref, nxt_ref, wt_ref, h2t_ref, g_ref, b_ref, ys_ref, o_ref, buf_ref, sem, *, tm):
    n_rows = tm * TOP_K
    i = pl.program_id(0)
    slot = i & 1

    def row_copy(p_ref, s, r, t, k):
        tile_rows = pl.ds(pl.multiple_of(t * ROW_TILE, ROW_TILE), ROW_TILE)
        return pltpu.make_async_copy(ys_ref.at[p_ref[r]], buf_ref.at[s, k, tile_rows], sem.at[s])

    def start_gather(p_ref, s):
        _for_each_row(n_rows, lambda r, t, k, u: row_copy(p_ref, s, r, t, k).start(priority=u % _DMA_THREADS))

    @pl.when(i == 0)
    def _first():
        start_gather(pos_ref, 0)

    @pl.when(i + 1 < pl.num_programs(0))
    def _prefetch():
        start_gather(nxt_ref, 1 - slot)

    _for_each_row(n_rows, lambda r, t, k, u: row_copy(pos_ref, slot, r, t, k).wait())
    wt = wt_ref[...]
    ffn = wt[:, 0:1] * _load_row_tiles(buf_ref.at[slot, 0], tm)
    for k in range(1, TOP_K):
        ffn = ffn + wt[:, k:k + 1] * _load_row_tiles(buf_ref.at[slot, k], tm)
    o_ref[...] = _layer_norm(DEEPNORM_ALPHA * _load_row_tiles(h2t_ref, tm) + ffn, g_ref[...], b_ref[...])


def _combine(pos_flat, wts, h2t, g3, b3, ys, *, tm):
    t = h2t.shape[0] // ROW_TILE
    n_steps = t // tm
    vec = pl.BlockSpec((1, D_MODEL), lambda i: (0, 0))
    return pl.pallas_call(
        functools.partial(_combine_kernel, tm=tm),
        out_shape=jax.ShapeDtypeStruct((t, D_MODEL), _F32),
        grid=(n_steps,),
        in_specs=[pl.BlockSpec((tm * TOP_K,), lambda i: (i,), memory_space=pltpu.SMEM),
                  pl.BlockSpec((tm * TOP_K,), lambda i: (jnp.minimum(i + 1, n_steps - 1),), memory_space=pltpu.SMEM),
                  pl.BlockSpec((tm, TOP_K), lambda i: (i, 0)),
                  pl.BlockSpec((tm * ROW_TILE, LANES), lambda i: (i, 0)), vec, vec,
                  pl.BlockSpec(memory_space=pl.ANY)],
        out_specs=pl.BlockSpec((tm, D_MODEL), lambda i: (i, 0)),
        scratch_shapes=[pltpu.VMEM((2, TOP_K, tm * ROW_TILE, LANES), _F32), pltpu.SemaphoreType.DMA((2,))],
        compiler_params=_cparams(("arbitrary",)),
        name="combine",
    )(pos_flat, pos_flat, wts, h2t, g3, b3, ys)


def _tiles(seq):
    big = 512 if seq % 512 == 0 else MOBA_BLOCK
    return dict(inproj=big, attn_q=MOBA_BLOCK, post=big, rank=big, dispatch=big, expert=512, combine=MOBA_BLOCK)


def kernel(x, mem, w_in, diff_lambda_q1, diff_lambda_k1, diff_lambda_q2, diff_lambda_k2, diff_subln_g,
           w_mix_out, ln1_g, ln1_b, mem_ln_g, mem_ln_b, w_mem_q, w_mem_kv, w_mem_o, ln2_g, ln2_b, w_router,
           b_router, w_mlp1, b_mlp1, w_mlp2, b_mlp2, ln3_g, ln3_b):
    batch, seq, d = x.shape
    mem_len = mem.shape[1]
    assert d == D_MODEL and seq % MOBA_BLOCK == 0 and w_in.shape[0] == 1
    t = batch * seq
    ts = _tiles(seq)
    row = lambda v: v.reshape(1, -1).astype(_F32)

    col = jnp.arange(PROJ_WIDTH)
    is_q = (col < DIFF_WIDTH) | ((col >= 3 * DIFF_WIDTH) & (col < 3 * DIFF_WIDTH + MOBA_WIDTH))
    w_in_c = (w_in[0] * jnp.where(is_q, HEAD_DIM ** -0.5 * math.log2(math.e), 1.0)[None, :]).astype(_CDT)
    lam_vecs = jnp.stack([diff_lambda_q1[0], diff_lambda_k1[0], diff_lambda_q2[0], diff_lambda_k2[0]]).astype(_F32)
    w_r = jnp.zeros((D_MODEL, LANES), _F32).at[:, :N_EXPERTS].set(w_router[0]).astype(_CDT)
    b_r = jnp.full((1, LANES), _NEG, _F32).at[0, :N_EXPERTS].set(b_router[0])
    b1g = b_mlp1[0][:, None, 0::2].astype(_F32)
    b1l = b_mlp1[0][:, None, 1::2].astype(_F32)
    b2 = b_mlp2[0][:, None, :].astype(_F32)

    x2d = x.reshape(t, d)
    proj, kmean = _inproj(x2d, w_in_c, seq, tm=ts["inproj"])
    kmean = kmean.reshape(batch, seq // MOBA_BLOCK, MOBA_WIDTH)
    out_a = _diff_attn(proj, lam_vecs, row(diff_subln_g[0]), batch, seq, tq=ts["attn_q"])
    out_b = _moba_attn(proj, kmean, batch, seq)
    kv = _mem_kv(mem.reshape(batch * mem_len, d), row(mem_ln_g), row(mem_ln_b), w_mem_kv[0].astype(_CDT),
                 tm=mem_len)
    h2t, top_idx, top_w, counts = _post_attn(
        x2d, out_a, out_b, w_mix_out[0].astype(_CDT), row(ln1_g[0]), row(ln1_b[0]), kv,
        w_mem_q[0].astype(_CDT), w_mem_o[0].astype(_CDT), row(ln2_g[0]), row(ln2_b[0]), w_r, b_r,
        seq, mem_len, tm=ts["post"])

    tmx = ts["expert"]
    cnt = counts[0, :N_EXPERTS]
    padded = (cnt + tmx - 1) // tmx * tmx
    ends = jnp.cumsum(padded)
    offsets = ends - padded
    meta = jnp.concatenate([offsets, cnt, padded]).astype(jnp.int32)
    n_sorted_rows = t * TOP_K + N_EXPERTS * tmx
    n_tiles = n_sorted_rows // tmx
    tile_start = jnp.arange(n_tiles, dtype=jnp.int32) * tmx
    tile_expert = jnp.minimum(jnp.sum(tile_start[:, None] >= ends[None, :], axis=1), N_EXPERTS - 1).astype(jnp.int32)
    n_valid = (ends[-1:] // tmx).astype(jnp.int32)
    offsets_row = jnp.zeros((1, LANES), _F32).at[0, :N_EXPERTS].set(offsets.astype(_F32))
    pos = _rank(top_idx, offsets_row, tm=ts["rank"]).reshape(-1)

    xs = _dispatch(meta, h2t.reshape(t, ROW_TILE, LANES), pos, n_sorted_rows, tm=ts["dispatch"])
    ys = _experts(tile_expert, n_valid, xs.reshape(n_sorted_rows * ROW_TILE, LANES), w_mlp1[0], b1g, b1l,
                  w_mlp2[0], b2, tm=tmx)
    out = _combine(pos, top_w, h2t, row(ln3_g[0]), row(ln3_b[0]),
                   ys.reshape(n_sorted_rows, ROW_TILE, LANES), tm=ts["combine"])
    return out.reshape(batch, seq, d)
```

```python
import functools
import math

import jax
import jax.numpy as jnp
from jax import lax
from jax.experimental import pallas as pl
from jax.experimental.pallas import tpu as pltpu

D_MODEL = 1024
DIFF_HEADS = 4
HEAD_DIM = 64
DIFF_WIDTH = DIFF_HEADS * 2 * HEAD_DIM
MOBA_HEADS = 8
MOBA_WIDTH = MOBA_HEADS * HEAD_DIM
MOBA_BLOCK = 256
MOBA_TOPK = 3
ROPE_THETA = 500000.0
ROT_DIM = HEAD_DIM // 4
MEM_HEADS = 4
MEM_HEAD_DIM = D_MODEL // MEM_HEADS
N_EXPERTS = 32
TOP_K = 4
D_FF = D_MODEL
SWIGLU_ALPHA = 1.702
SWIGLU_LIMIT = 7.0
LN_EPS = 1e-5
RMS_EPS = 1e-5
DEEPNORM_ALPHA = 2.0 ** 0.25
LAMBDA_INIT = 0.8 - 0.6 * math.exp(0.0)

LANES = 128
ROW_TILE = 8
V7X_VMEM_LIMIT = 56 * 1024 * 1024

_CDT = jnp.bfloat16
_F32 = jnp.float32
_NEG = -1e30

_CB_DQ, _CB_DK, _CB_DV = 0, 4, 8
_CB_MQ, _CB_MK, _CB_MV = 12, 16, 20
PROJ_WIDTH = 3 * (DIFF_WIDTH + MOBA_WIDTH)


def _cparams(sem, vmem=V7X_VMEM_LIMIT):
    return pltpu.CompilerParams(dimension_semantics=sem, vmem_limit_bytes=vmem)


def _dot(a, b):
    return jnp.dot(a, b, preferred_element_type=_F32)


def _dot_nt(a, b):
    return lax.dot_general(a, b, (((1,), (1,)), ((), ())), preferred_element_type=_F32)


def _layer_norm(x, g, b):
    mu = jnp.mean(x, axis=-1, keepdims=True)
    xc = x - mu
    var = jnp.mean(xc * xc, axis=-1, keepdims=True)
    return xc * lax.rsqrt(var + LN_EPS) * g + b


_ROPE_BLOCKS = tuple(range(_CB_DQ, _CB_DV)) + tuple(range(_CB_MQ, _CB_MV))


def _inproj_kernel(x_ref, w_ref, cos_ref, sa_ref, sb_ref, o_ref, km_ref, *, tm):
    x = x_ref[...].astype(_CDT)
    cos, sa, sb = cos_ref[...], sa_ref[...], sb_ref[...]
    seg_w = 4 * LANES
    for seg in range(PROJ_WIDTH // seg_w):
        pseg = _dot(x, w_ref[:, seg * seg_w:(seg + 1) * seg_w])
        for c in range(4):
            cb = seg * 4 + c
            cols = slice(cb * LANES, (cb + 1) * LANES)
            p = pseg[:, c * LANES:(c + 1) * LANES]
            if cb in _ROPE_BLOCKS:
                p = p * cos + pltpu.roll(p, 8, 1) * sa + pltpu.roll(p, LANES - 8, 1) * sb
            o_ref[:, cols] = p.astype(o_ref.dtype)
            if _CB_MK <= cb < _CB_MV:
                kc = slice((cb - _CB_MK) * LANES, (cb - _CB_MK + 1) * LANES)
                for blk in range(tm // MOBA_BLOCK):
                    rows = p[blk * MOBA_BLOCK:(blk + 1) * MOBA_BLOCK, :]
                    km_ref[blk, :, kc] = jnp.sum(rows, axis=0, keepdims=True) * (1.0 / MOBA_BLOCK)


def _rope_tables(seq):
    half = ROT_DIM // 2
    inv_freq = ROPE_THETA ** (-jnp.arange(0, ROT_DIM, 2, dtype=_F32) / ROT_DIM)
    ang = jnp.arange(seq, dtype=_F32)[:, None] * inv_freq[None, :]
    cos, sin = jnp.cos(ang), jnp.sin(ang)
    lane = jnp.arange(LANES) % HEAD_DIM
    first, second = lane < half, (lane >= half) & (lane < ROT_DIM)
    idx = jnp.where(first, lane, jnp.where(second, lane - half, 0))
    cos_t = jnp.where((first | second)[None, :], cos[:, idx], 1.0)
    sa_t = jnp.where(second[None, :], sin[:, idx], 0.0)
    sb_t = jnp.where(first[None, :], -sin[:, idx], 0.0)
    return cos_t, sa_t, sb_t


def _inproj(x2d, w_in, seq, *, tm):
    t = x2d.shape[0]
    cos_t, sa_t, sb_t = _rope_tables(seq)
    n_pos = seq // tm
    tab_spec = pl.BlockSpec((tm, LANES), lambda i: (i % n_pos, 0))
    return pl.pallas_call(
        functools.partial(_inproj_kernel, tm=tm),
        out_shape=(jax.ShapeDtypeStruct((t, PROJ_WIDTH), _CDT),
                   jax.ShapeDtypeStruct((t // MOBA_BLOCK, 1, MOBA_WIDTH), _F32)),
        grid=(t // tm,),
        in_specs=[pl.BlockSpec((tm, D_MODEL), lambda i: (i, 0)),
                  pl.BlockSpec((D_MODEL, PROJ_WIDTH), lambda i: (0, 0)),
                  tab_spec, tab_spec, tab_spec],
        out_specs=(pl.BlockSpec((tm, PROJ_WIDTH), lambda i: (i, 0)),
                   pl.BlockSpec((tm // MOBA_BLOCK, 1, MOBA_WIDTH), lambda i: (i, 0, 0))),
        compiler_params=_cparams(("parallel",)),
        name="inproj",
    )(x2d, w_in, cos_t, sa_t, sb_t)


def _exp2_parts(parts):
    chunks = [s[:, c:c + LANES] for s in parts for c in range(0, s.shape[1], LANES)]
    m = jnp.max(functools.reduce(jnp.maximum, chunks), axis=1, keepdims=True)
    return [jnp.exp2(s - m).astype(_CDT) for s in parts]


_SCORES_AHEAD = 1


def _causal_mask(tq):
    row = lax.broadcasted_iota(jnp.int32, (tq, tq), 0)
    col = lax.broadcasted_iota(jnp.int32, (tq, tq), 1)
    return col <= row


def _diff_attn_kernel(q_ref, k_ref, v_ref, lam_ref, g_ref, o_ref, *, seq, tq):
    lane = lax.broadcasted_iota(jnp.int32, (tq, LANES), 1)
    mask = _causal_mask(tq)
    lv = lam_ref[...]
    lam = (jnp.exp(jnp.sum(lv[0:1] * lv[1:2], axis=1, keepdims=True))
           - jnp.exp(jnp.sum(lv[2:3] * lv[3:4], axis=1, keepdims=True)) + LAMBDA_INIT)
    gain = g_ref[...] * (1.0 - LAMBDA_INIT)

    def with_ones(v):
        return jnp.concatenate([v, jnp.ones_like(v)], axis=1)

    def scores(i):
        rows = slice(i * tq, (i + 1) * tq)
        q = q_ref[rows, :]
        zero = jnp.zeros_like(q)
        q1 = jnp.where(lane < HEAD_DIM, q, zero)
        q2 = jnp.where(lane >= HEAD_DIM, q, zero)
        kd = k_ref[rows, :]
        s1 = [jnp.where(mask, _dot_nt(q1, kd), _NEG)]
        s2 = [jnp.where(mask, _dot_nt(q2, kd), _NEG)]
        if i:
            kp = k_ref[:i * tq, :]
            s1.append(_dot_nt(q1, kp))
            s2.append(_dot_nt(q2, kp))
        return s1, s2

    def finish(i, s1, s2):
        rows = slice(i * tq, (i + 1) * tq)
        v_all = with_ones(v_ref[:(i + 1) * tq, :])
        nums = []
        for parts in (s1, s2):
            es = _exp2_parts(parts)
            nl = _dot(jnp.concatenate(es[1:] + es[:1], axis=1), v_all)
            nums.append(nl[:, :LANES] / nl[:, LANES:LANES + 1])
        o = nums[0] - lam * nums[1]
        o = o * lax.rsqrt(jnp.mean(o * o, axis=1, keepdims=True) + RMS_EPS) * gain
        o_ref[rows, :] = o.astype(o_ref.dtype)

    n_tiles = seq // tq
    pending = [scores(i) for i in range(min(_SCORES_AHEAD, n_tiles))]
    for i in range(n_tiles):
        if i + _SCORES_AHEAD < n_tiles:
            pending.append(scores(i + _SCORES_AHEAD))
        finish(i, *pending.pop(0))


def _diff_attn(proj, lam_vecs, subln_g, batch, seq, *, tq):
    t = batch * seq
    blk = lambda cb: pl.BlockSpec((seq, LANES), lambda b, h: (b, cb + h))
    return pl.pallas_call(
        functools.partial(_diff_attn_kernel, seq=seq, tq=tq),
        out_shape=jax.ShapeDtypeStruct((t, DIFF_WIDTH), _CDT),
        grid=(batch, DIFF_HEADS),
        in_specs=[blk(_CB_DQ), blk(_CB_DK), blk(_CB_DV),
                  pl.BlockSpec((4, HEAD_DIM), lambda b, h: (0, 0)),
                  pl.BlockSpec((1, 2 * HEAD_DIM), lambda b, h: (0, 0))],
        out_specs=pl.BlockSpec((seq, LANES), lambda b, h: (b, h)),
        compiler_params=_cparams(("parallel", "parallel")),
        name="diff_attn",
    )(proj, proj, proj, lam_vecs, subln_g)


def _moba_selection(gate, n_past):
    blk = lax.broadcasted_iota(jnp.int32, gate.shape, 0)
    rank = jnp.zeros(gate.shape, _F32)
    for m_blk in range(n_past):
        gm = gate[m_blk:m_blk + 1, :]
        beats = (gm > gate) | ((gm == gate) & (m_blk < blk))
        rank = rank + jnp.where(beats, 1.0, 0.0)
    return jnp.where((blk < n_past) & (rank < MOBA_TOPK), 1.0, 0.0)


def _moba_kernel(q_ref, k_ref, v_ref, km_ref, o_ref, *, nb):
    tq = MOBA_BLOCK
    lane = lax.broadcasted_iota(jnp.int32, (tq, LANES), 1)
    mask = _causal_mask(tq)
    km = km_ref[...].astype(_CDT)
    in_head = [(lane >= hh * HEAD_DIM) & (lane < (hh + 1) * HEAD_DIM) for hh in range(2)]
    seq_lane = lax.broadcasted_iota(jnp.int32, (nb * tq, LANES), 1)
    v_ones = [jnp.where((seq_lane >= hh * HEAD_DIM) & (seq_lane < (hh + 1) * HEAD_DIM), v_ref[...],
                        jnp.ones((nb * tq, LANES), _CDT)) for hh in range(2)]

    def scores(i, hh):
        rows = slice(i * tq, (i + 1) * tq)
        q = q_ref[rows, :]
        qh = jnp.where(in_head[hh], q, jnp.zeros_like(q))
        parts = [jnp.where(mask, _dot_nt(qh, k_ref[rows, :]), _NEG)]
        gated = i > MOBA_TOPK
        if gated:
            sel = _moba_selection(_dot_nt(km, qh), i)
            sel_q = jnp.concatenate([sel, jnp.zeros((LANES - nb, tq), _F32)], axis=0).T
        for j in range(i):
            s = _dot_nt(qh, k_ref[j * tq:(j + 1) * tq, :])
            parts.append(jnp.where(sel_q[:, j:j + 1] > 0.5, s, _NEG) if gated else s)
        return parts

    def finish(i, hh, parts):
        es = _exp2_parts(parts)
        acc = _dot(jnp.concatenate(es[1:] + es[:1], axis=1), v_ones[hh][:(i + 1) * tq, :])
        sum_lane = (1 - hh) * HEAD_DIM
        return acc / acc[:, sum_lane:sum_lane + 1]

    units = [(i, hh) for i in range(nb) for hh in range(2)]
    pending = [scores(*u) for u in units[:_SCORES_AHEAD]]
    outs = {}
    for n, (i, hh) in enumerate(units):
        if n + _SCORES_AHEAD < len(units):
            pending.append(scores(*units[n + _SCORES_AHEAD]))
        outs[hh] = finish(i, hh, pending.pop(0))
        if hh == 1:
            o_ref[i * tq:(i + 1) * tq, :] = jnp.where(in_head[0], outs[0], outs[1]).astype(o_ref.dtype)


def _moba_attn(proj, kmean, batch, seq):
    nb = seq // MOBA_BLOCK
    t = batch * seq
    blk = lambda cb: pl.BlockSpec((seq, LANES), lambda b, p: (b, cb + p))
    return pl.pallas_call(
        functools.partial(_moba_kernel, nb=nb),
        out_shape=jax.ShapeDtypeStruct((t, MOBA_WIDTH), _CDT),
        grid=(batch, MOBA_HEADS // 2),
        in_specs=[blk(_CB_MQ), blk(_CB_MK), blk(_CB_MV),
                  pl.BlockSpec((None, nb, LANES), lambda b, p: (b, 0, p))],
        out_specs=pl.BlockSpec((seq, LANES), lambda b, p: (b, p)),
        compiler_params=_cparams(("parallel", "parallel")),
        name="moba_attn",
    )(proj, proj, proj, kmean)


def _mem_kv_kernel(mem_ref, g_ref, b_ref, w_ref, o_ref):
    mn = _layer_norm(mem_ref[...], g_ref[...], b_ref[...])
    o_ref[...] = _dot(mn.astype(_CDT), w_ref[...]).astype(o_ref.dtype)


def _mem_kv(mem2d, g, b, w_kv, *, tm):
    rows = mem2d.shape[0]
    vec = pl.BlockSpec((1, D_MODEL), lambda i: (0, 0))
    return pl.pallas_call(
        _mem_kv_kernel,
        out_shape=jax.ShapeDtypeStruct((rows, 2 * D_MODEL), _CDT),
        grid=(rows // tm,),
        in_specs=[pl.BlockSpec((tm, D_MODEL), lambda i: (i, 0)), vec, vec,
                  pl.BlockSpec((D_MODEL, 2 * D_MODEL), lambda i: (0, 0))],
        out_specs=pl.BlockSpec((tm, 2 * D_MODEL), lambda i: (i, 0)),
        compiler_params=_cparams(("parallel",)),
        name="mem_kv",
    )(mem2d, g, b, w_kv)


def _store_row_tiles(ref, val):
    n = val.shape[0]
    for c in range(ROW_TILE):
        ref[pl.ds(c, n, stride=ROW_TILE), :] = val[:, c * LANES:(c + 1) * LANES]


def _load_row_tiles(ref, n):
    return jnp.concatenate([ref[pl.ds(c, n, stride=ROW_TILE), :] for c in range(ROW_TILE)], axis=1)


def _interleave(stage_generators):
    live = list(stage_generators)
    while live:
        live = [g for g in live if next(g, _DONE) is not _DONE]


_DONE = object()


def _post_attn_kernel(x_ref, oa_ref, ob_ref, wout_ref, g1_ref, b1_ref, kv_ref, wq_ref, wo_ref,
                      g2_ref, b2_ref, wr_ref, br_ref, h2t_ref, idx_ref, wt_ref, cnt_ref, cnt_acc, *, tm, groups):
    @pl.when(pl.program_id(0) == 0)
    def _init_counts():
        cnt_acc[...] = jnp.zeros_like(cnt_acc)

    n = tm // groups
    lane = lax.broadcasted_iota(jnp.int32, (n, LANES), 1)
    lane_f = lane.astype(_F32)

    def row_group(g):
        rows = slice(g * n, (g + 1) * n)
        mix = _dot(oa_ref[rows, :], wout_ref[:DIFF_WIDTH, :]) + _dot(ob_ref[rows, :], wout_ref[DIFF_WIDTH:, :])
        yield
        h1 = _layer_norm(DEEPNORM_ALPHA * x_ref[rows, :] + mix, g1_ref[...], b1_ref[...])
        q = (_dot(h1.astype(_CDT), wq_ref[...]) * (MEM_HEAD_DIM ** -0.5)).astype(_CDT)
        yield
        heads = []
        for h in range(MEM_HEADS):
            c0 = h * MEM_HEAD_DIM
            kh = kv_ref[:, c0:c0 + MEM_HEAD_DIM]
            vh = kv_ref[:, D_MODEL + c0:D_MODEL + c0 + MEM_HEAD_DIM]
            s = _dot_nt(q[:, c0:c0 + MEM_HEAD_DIM], kh)
            p = jnp.exp(s - jnp.max(s, axis=1, keepdims=True))
            o = _dot(p.astype(_CDT), vh) / jnp.sum(p, axis=1, keepdims=True)
            heads.append(o.astype(_CDT))
        yield
        xatt = _dot(jnp.concatenate(heads, axis=1), wo_ref[...])
        yield
        h2 = _layer_norm(DEEPNORM_ALPHA * h1 + xatt, g2_ref[...], b2_ref[...])
        _store_row_tiles(h2t_ref.at[pl.ds(g * n * ROW_TILE, n * ROW_TILE)], h2)
        logits = _dot(h2.astype(_CDT), wr_ref[...]) + br_ref[...]
        yield
        vals, idxs = [], []
        work = logits
        for _ in range(TOP_K):
            m = jnp.max(work, axis=1, keepdims=True)
            ik = jnp.min(jnp.where(work == m, lane_f, float(LANES)), axis=1, keepdims=True)
            vals.append(m)
            idxs.append(ik)
            work = jnp.where(lane_f == ik, -jnp.inf, work)
        exps = [jnp.exp(v - vals[0]) for v in vals]
        denom = exps[0] + exps[1] + exps[2] + exps[3]
        idx_w = jnp.zeros((n, LANES), _F32)
        wt_w = jnp.zeros((n, LANES), _F32)
        for k in range(TOP_K):
            idx_w = jnp.where(lane == k, idxs[k], idx_w)
            wt_w = jnp.where(lane == k, exps[k] / denom, wt_w)
        idx_ref[rows, :] = idx_w[:, :TOP_K].astype(jnp.int32)
        wt_ref[rows, :] = wt_w[:, :TOP_K]
        chosen = functools.reduce(jnp.add, [jnp.where(lane_f == ik, 1.0, 0.0) for ik in idxs])
        cnt_acc[...] = cnt_acc[...] + jnp.sum(chosen, axis=0, keepdims=True)

    _interleave(row_group(g) for g in range(groups))
    cnt_ref[...] = cnt_acc[...].astype(jnp.int32)


def _post_attn(x2d, out_a, out_b, w_out, g1, b1, kv, w_q, w_o, g2, b2, w_r, b_r, seq, mem_len, *, tm):
    t = x2d.shape[0]
    per_b = seq // tm
    full = lambda shape: pl.BlockSpec(shape, lambda i: (0,) * len(shape))
    vec = full((1, D_MODEL))
    return pl.pallas_call(
        functools.partial(_post_attn_kernel, tm=tm, groups=max(1, tm // MOBA_BLOCK)),
        out_shape=(jax.ShapeDtypeStruct((t * ROW_TILE, LANES), _F32),
                   jax.ShapeDtypeStruct((t, TOP_K), jnp.int32),
                   jax.ShapeDtypeStruct((t, TOP_K), _F32),
                   jax.ShapeDtypeStruct((1, LANES), jnp.int32)),
        grid=(t // tm,),
        in_specs=[pl.BlockSpec((tm, D_MODEL), lambda i: (i, 0)),
                  pl.BlockSpec((tm, DIFF_WIDTH), lambda i: (i, 0)),
                  pl.BlockSpec((tm, MOBA_WIDTH), lambda i: (i, 0)),
                  full((D_MODEL, D_MODEL)), vec, vec,
                  pl.BlockSpec((mem_len, 2 * D_MODEL), lambda i: (i // per_b, 0)),
                  full((D_MODEL, D_MODEL)), full((D_MODEL, D_MODEL)), vec, vec,
                  full((D_MODEL, LANES)), full((1, LANES))],
        out_specs=(pl.BlockSpec((tm * ROW_TILE, LANES), lambda i: (i, 0)),
                   pl.BlockSpec((tm, TOP_K), lambda i: (i, 0)),
                   pl.BlockSpec((tm, TOP_K), lambda i: (i, 0)),
                   pl.BlockSpec((1, LANES), lambda i: (0, 0))),
        scratch_shapes=[pltpu.VMEM((1, LANES), _F32)],
        compiler_params=_cparams(("arbitrary",)),
        name="post_attn",
    )(x2d, out_a, out_b, w_out, g1, b1, kv, w_q, w_o, g2, b2, w_r, b_r)


def _rank_kernel(idx_ref, off_ref, pos_ref, next_ref, *, tm):
    @pl.when(pl.program_id(0) == 0)
    def _init():
        next_ref[...] = off_ref[...]

    idx = idx_ref[...]
    lane = lax.broadcasted_iota(jnp.int32, (tm, LANES), 1)
    hits = [lane == idx[:, k:k + 1] for k in range(TOP_K)]
    sel = jnp.zeros((tm, LANES), _F32)
    for hit in hits:
        sel = sel + jnp.where(hit, 1.0, 0.0)
    row = lax.broadcasted_iota(jnp.int32, (tm, tm), 0)
    col = lax.broadcasted_iota(jnp.int32, (tm, tm), 1)
    earlier = jnp.where(col < row, 1.0, 0.0).astype(jnp.bfloat16)
    dense = _dot(earlier, sel.astype(jnp.bfloat16)) + next_ref[...]
    pos_w = jnp.zeros((tm, LANES), _F32)
    for k, hit in enumerate(hits):
        pk = jnp.sum(jnp.where(hit, dense, 0.0), axis=1, keepdims=True)
        pos_w = jnp.where(lane == k, pk, pos_w)
    pos_ref[...] = pos_w[:, :TOP_K].astype(jnp.int32)
    next_ref[...] = next_ref[...] + jnp.sum(sel, axis=0, keepdims=True)


def _rank(idx, offsets, *, tm):
    t = idx.shape[0]
    return pl.pallas_call(
        functools.partial(_rank_kernel, tm=tm),
        out_shape=jax.ShapeDtypeStruct((t, TOP_K), jnp.int32),
        grid=(t // tm,),
        in_specs=[pl.BlockSpec((tm, TOP_K), lambda i: (i, 0)), pl.BlockSpec((1, LANES), lambda i: (0, 0))],
        out_specs=pl.BlockSpec((tm, TOP_K), lambda i: (i, 0)),
        scratch_shapes=[pltpu.VMEM((1, LANES), _F32)],
        compiler_params=_cparams(("arbitrary",)),
        name="rank",
    )(idx, offsets)


_DMA_UNROLL = 8
_DMA_THREADS = 2


def _for_each_row(n_rows, fn):
    tokens_per_group = _DMA_UNROLL // TOP_K

    def group(g, c):
        for u in range(_DMA_UNROLL):
            fn(g * _DMA_UNROLL + u, g * tokens_per_group + u // TOP_K, u % TOP_K, u)
        return c

    lax.fori_loop(0, n_rows // _DMA_UNROLL, group, 0)


def _dispatch_kernel(meta_ref, h_ref, pos_ref, xs_ref, zero_ref, sem, zsem, *, tm):
    n_rows = tm * TOP_K

    def zero_copy(r):
        return pltpu.make_async_copy(zero_ref, xs_ref.at[r], zsem)

    @pl.when(pl.program_id(0) == 0)
    def _zero_fill():
        zero_ref[...] = jnp.zeros_like(zero_ref)

        def per_expert(e, c):
            lo = meta_ref[e] + meta_ref[N_EXPERTS + e]
            hi = meta_ref[e] + meta_ref[2 * N_EXPERTS + e]
            lax.fori_loop(lo, hi, lambda r, c2: (zero_copy(r).start(), c2)[1], 0)
            lax.fori_loop(lo, hi, lambda r, c2: (zero_copy(r).wait(), c2)[1], 0)
            return c

        lax.fori_loop(0, N_EXPERTS, per_expert, 0)

    def row_copy(r, t, k, u):
        return pltpu.make_async_copy(h_ref.at[t], xs_ref.at[pos_ref[r]], sem)

    _for_each_row(n_rows, lambda r, t, k, u: row_copy(r, t, k, u).start(priority=u % _DMA_THREADS))
    _for_each_row(n_rows, lambda r, t, k, u: row_copy(r, t, k, u).wait())


def _dispatch(meta, h2t, pos_flat, n_sorted_rows, *, tm):
    t = h2t.shape[0]
    return pl.pallas_call(
        functools.partial(_dispatch_kernel, tm=tm),
        out_shape=jax.ShapeDtypeStruct((n_sorted_rows, ROW_TILE, LANES), _F32),
        grid_spec=pltpu.PrefetchScalarGridSpec(
            num_scalar_prefetch=1,
            grid=(t // tm,),
            in_specs=[pl.BlockSpec((tm, ROW_TILE, LANES), lambda i, meta: (i, 0, 0)),
                      pl.BlockSpec((tm * TOP_K,), lambda i, meta: (i,), memory_space=pltpu.SMEM)],
            out_specs=pl.BlockSpec(memory_space=pl.ANY),
            scratch_shapes=[pltpu.VMEM((ROW_TILE, LANES), _F32),
                            pltpu.SemaphoreType.DMA, pltpu.SemaphoreType.DMA]),
        compiler_params=_cparams(("arbitrary",)),
        name="dispatch",
    )(meta, h2t, pos_flat)


_DEINT = 2 * LANES


def _deinterleave_matrix():
    k = jnp.arange(_DEINT)[:, None]
    n = jnp.arange(_DEINT)[None, :]
    src = jnp.where(n < LANES, 2 * n, 2 * (n - LANES) + 1)
    return (k == src).astype(_CDT)


def _experts_kernel(te_ref, nv_ref, xs_ref, perm_ref, w1_ref, b1g_ref, b1l_ref, w2_ref, b2_ref, y_ref,
                    wg_ref, wl_ref, w2c_ref, *, tm):
    i = pl.program_id(0)
    live = i < nv_ref[0]
    new_expert = (i == 0) | (te_ref[i] != te_ref[jnp.maximum(i - 1, 0)])

    @pl.when(live & new_expert)
    def _stage_weights():
        for c in range(2 * D_FF // _DEINT):
            both = _dot(w1_ref[0, :, c * _DEINT:(c + 1) * _DEINT].astype(_CDT), perm_ref[...])
            wg_ref[:, c * LANES:(c + 1) * LANES] = both[:, :LANES].astype(_CDT)
            wl_ref[:, c * LANES:(c + 1) * LANES] = both[:, LANES:].astype(_CDT)
        w2c_ref[...] = w2_ref[0].astype(_CDT)

    @pl.when(live)
    def _mlp():
        x = _load_row_tiles(xs_ref, tm).astype(_CDT)
        glu = jnp.minimum(_dot(x, wg_ref[...]) + b1g_ref[0], SWIGLU_LIMIT)
        lin = jnp.clip(_dot(x, wl_ref[...]) + b1l_ref[0], -SWIGLU_LIMIT, SWIGLU_LIMIT)
        act = glu * jax.nn.sigmoid(SWIGLU_ALPHA * glu) * (lin + 1.0)
        _store_row_tiles(y_ref, _dot(act.astype(_CDT), w2c_ref[...]) + b2_ref[0])

    @pl.when(jnp.logical_not(live))
    def _unused_tile():
        y_ref[...] = jnp.zeros_like(y_ref)


def _experts(tile_expert, n_valid, xs, w1, b1g, b1l, w2, b2, *, tm):
    n_tiles = xs.shape[0] // (tm * ROW_TILE)
    by_expert = lambda shape: pl.BlockSpec((1,) + shape, lambda i, te, nv: (te[i], 0, 0))
    return pl.pallas_call(
        functools.partial(_experts_kernel, tm=tm),
        out_shape=jax.ShapeDtypeStruct(xs.shape, _F32),
        grid_spec=pltpu.PrefetchScalarGridSpec(
            num_scalar_prefetch=2,
            grid=(n_tiles,),
            in_specs=[pl.BlockSpec((tm * ROW_TILE, LANES), lambda i, te, nv: (jnp.minimum(i, nv[0] - 1), 0)),
                      pl.BlockSpec((_DEINT, _DEINT), lambda i, te, nv: (0, 0)),
                      by_expert((D_MODEL, 2 * D_FF)),
                      by_expert((1, D_FF)), by_expert((1, D_FF)),
                      by_expert((D_FF, D_MODEL)), by_expert((1, D_MODEL))],
            out_specs=pl.BlockSpec((tm * ROW_TILE, LANES), lambda i, te, nv: (i, 0)),
            scratch_shapes=[pltpu.VMEM((D_MODEL, D_FF), _CDT), pltpu.VMEM((D_MODEL, D_FF), _CDT),
                            pltpu.VMEM((D_FF, D_MODEL), _CDT)]),
        compiler_params=_cparams(("arbitrary",)),
        name="experts",
    )(tile_expert, n_valid, xs, _deinterleave_matrix(), w1, b1g, b1l, w2, b2)


def _combine_kernel(pos_ref, nxt_ref, wt_ref, h2t_ref, g_ref, b_ref, ys_ref, o_ref, buf_ref, sem, *, tm):
    n_rows = tm * TOP_K
    i = pl.program_id(0)
    slot = i & 1

    def row_copy(p_ref, s, r, t, k):
        tile_rows = pl.ds(pl.multiple_of(t * ROW_TILE, ROW_TILE), ROW_TILE)
        return pltpu.make_async_copy(ys_ref.at[p_ref[r]], buf_ref.at[s, k, tile_rows], sem.at[s])

    def start_gather(p_ref, s):
        _for_each_row(n_rows, lambda r, t, k, u: row_copy(p_ref, s, r, t, k).start(priority=u % _DMA_THREADS))

    @pl.when(i == 0)
    def _first():
        start_gather(pos_ref, 0)

    _for_each_row(n_rows, lambda r, t, k, u: row_copy(pos_ref, slot, r, t, k).wait())

    for r in range(n_rows):
        row_copy(nxt_ref, 1 - slot, r, r // TOP_K, r % TOP_K).start(priority=r % _DMA_THREADS)
    wt = wt_ref[...]
    ffn = wt[:, 0:1] * _load_row_tiles(buf_ref.at[slot, 0], tm)
    for k in range(1, TOP_K):
        ffn = ffn + wt[:, k:k + 1] * _load_row_tiles(buf_ref.at[slot, k], tm)
    o_ref[...] = _layer_norm(DEEPNORM_ALPHA * _load_row_tiles(h2t_ref, tm) + ffn, g_ref[...], b_ref[...])

    @pl.when(i + 1 == pl.num_programs(0))
    def _drain():
        _for_each_row(n_rows, lambda r, t, k, u: row_copy(nxt_ref, 1 - slot, r, t, k).wait())


def _combine(pos_flat, wts, h2t, g3, b3, ys, *, tm):
    t = h2t.shape[0] // ROW_TILE
    n_steps = t // tm
    vec = pl.BlockSpec((1, D_MODEL), lambda i: (0, 0))
    return pl.pallas_call(
        functools.partial(_combine_kernel, tm=tm),
        out_shape=jax.ShapeDtypeStruct((t, D_MODEL), _F32),
        grid=(n_steps,),
        in_specs=[pl.BlockSpec((tm * TOP_K,), lambda i: (i,), memory_space=pltpu.SMEM),
                  pl.BlockSpec((tm * TOP_K,), lambda i: (jnp.minimum(i + 1, n_steps - 1),), memory_space=pltpu.SMEM),
                  pl.BlockSpec((tm, TOP_K), lambda i: (i, 0)),
                  pl.BlockSpec((tm * ROW_TILE, LANES), lambda i: (i, 0)), vec, vec,
                  pl.BlockSpec(memory_space=pl.ANY)],
        out_specs=pl.BlockSpec((tm, D_MODEL), lambda i: (i, 0)),
        scratch_shapes=[pltpu.VMEM((2, TOP_K, tm * ROW_TILE, LANES), _F32), pltpu.SemaphoreType.DMA((2,))],
        compiler_params=_cparams(("arbitrary",)),
        name="combine",
    )(pos_flat, pos_flat, wts, h2t, g3, b3, ys)


def _tiles(seq):
    big = 512 if seq % 512 == 0 else MOBA_BLOCK
    post = 1024 if seq % 1024 == 0 else big
    return dict(inproj=big, attn_q=MOBA_BLOCK, post=post, rank=big, dispatch=big, expert=512, combine=MOBA_BLOCK)


def kernel(x, mem, w_in, diff_lambda_q1, diff_lambda_k1, diff_lambda_q2, diff_lambda_k2, diff_subln_g,
           w_mix_out, ln1_g, ln1_b, mem_ln_g, mem_ln_b, w_mem_q, w_mem_kv, w_mem_o, ln2_g, ln2_b, w_router,
           b_router, w_mlp1, b_mlp1, w_mlp2, b_mlp2, ln3_g, ln3_b):
    batch, seq, d = x.shape
    mem_len = mem.shape[1]
    assert d == D_MODEL and seq % MOBA_BLOCK == 0 and w_in.shape[0] == 1
    t = batch * seq
    ts = _tiles(seq)
    row = lambda v: v.reshape(1, -1).astype(_F32)

    col = jnp.arange(PROJ_WIDTH)
    is_q = (col < DIFF_WIDTH) | ((col >= 3 * DIFF_WIDTH) & (col < 3 * DIFF_WIDTH + MOBA_WIDTH))
    w_in_c = (w_in[0] * jnp.where(is_q, HEAD_DIM ** -0.5 * math.log2(math.e), 1.0)[None, :]).astype(_CDT)
    lam_vecs = jnp.stack([diff_lambda_q1[0], diff_lambda_k1[0], diff_lambda_q2[0], diff_lambda_k2[0]]).astype(_F32)
    w_r = jnp.zeros((D_MODEL, LANES), _F32).at[:, :N_EXPERTS].set(w_router[0]).astype(_CDT)
    b_r = jnp.full((1, LANES), _NEG, _F32).at[0, :N_EXPERTS].set(b_router[0])
    b1g = b_mlp1[0][:, None, 0::2].astype(_F32)
    b1l = b_mlp1[0][:, None, 1::2].astype(_F32)
    b2 = b_mlp2[0][:, None, :].astype(_F32)

    x2d = x.reshape(t, d)
    proj, kmean = _inproj(x2d, w_in_c, seq, tm=ts["inproj"])
    kmean = kmean.reshape(batch, seq // MOBA_BLOCK, MOBA_WIDTH)
    out_a = _diff_attn(proj, lam_vecs, row(diff_subln_g[0]), batch, seq, tq=ts["attn_q"])
    out_b = _moba_attn(proj, kmean, batch, seq)
    kv = _mem_kv(mem.reshape(batch * mem_len, d), row(mem_ln_g), row(mem_ln_b), w_mem_kv[0].astype(_CDT),
                 tm=mem_len)
    h2t, top_idx, top_w, counts = _post_attn(
        x2d, out_a, out_b, w_mix_out[0].astype(_CDT), row(ln1_g[0]), row(ln1_b[0]), kv,
        w_mem_q[0].astype(_CDT), w_mem_o[0].astype(_CDT), row(ln2_g[0]), row(ln2_b[0]), w_r, b_r,
        seq, mem_len, tm=ts["post"])

    tmx = ts["expert"]
    cnt = counts[0, :N_EXPERTS]
    padded = (cnt + tmx - 1) // tmx * tmx
    ends = jnp.cumsum(padded)
    offsets = ends - padded
    meta = jnp.concatenate([offsets, cnt, padded]).astype(jnp.int32)
    n_sorted_rows = t * TOP_K + N_EXPERTS * tmx
    n_tiles = n_sorted_rows // tmx
    tile_start = jnp.arange(n_tiles, dtype=jnp.int32) * tmx
    tile_expert = jnp.minimum(jnp.sum(tile_start[:, None] >= ends[None, :], axis=1), N_EXPERTS - 1).astype(jnp.int32)
    n_valid = (ends[-1:] // tmx).astype(jnp.int32)
    offsets_row = jnp.zeros((1, LANES), _F32).at[0, :N_EXPERTS].set(offsets.astype(_F32))
    pos = _rank(top_idx, offsets_row, tm=ts["rank"]).reshape(-1)

    xs = _dispatch(meta, h2t.reshape(t, ROW_TILE, LANES), pos, n_sorted_rows, tm=ts["dispatch"])
    ys = _experts(tile_expert, n_valid, xs.reshape(n_sorted_rows * ROW_TILE, LANES), w_mlp1[0], b1g, b1l,
                  w_mlp2[0], b2, tm=tmx)
    out = _combine(pos, top_w, h2t, row(ln3_g[0]), row(ln3_b[0]),
                   ys.reshape(n_sorted_rows, ROW_TILE, LANES), tm=ts["combine"])
    return out.reshape(batch, seq, d)
```

```python
import functools
import math

import jax
import jax.numpy as jnp
from jax import lax
from jax.experimental import pallas as pl
from jax.experimental.pallas import tpu as pltpu

D_MODEL = 1024
DIFF_HEADS = 4
HEAD_DIM = 64
DIFF_WIDTH = DIFF_HEADS * 2 * HEAD_DIM
MOBA_HEADS = 8
MOBA_WIDTH = MOBA_HEADS * HEAD_DIM
MOBA_BLOCK = 256
MOBA_TOPK = 3
ROPE_THETA = 500000.0
ROT_DIM = HEAD_DIM // 4
MEM_HEADS = 4
MEM_HEAD_DIM = D_MODEL // MEM_HEADS
N_EXPERTS = 32
TOP_K = 4
D_FF = D_MODEL
SWIGLU_ALPHA = 1.702
SWIGLU_LIMIT = 7.0
LN_EPS = 1e-5
RMS_EPS = 1e-5
DEEPNORM_ALPHA = 2.0 ** 0.25
LAMBDA_INIT = 0.8 - 0.6 * math.exp(0.0)

LANES = 128
ROW_TILE = 8
V7X_VMEM_LIMIT = 56 * 1024 * 1024

_CDT = jnp.bfloat16
_F32 = jnp.float32
_NEG = -1e30

_CB_DQ, _CB_DK, _CB_DV = 0, 4, 8
_CB_MQ, _CB_MK, _CB_MV = 12, 16, 20
PROJ_WIDTH = 3 * (DIFF_WIDTH + MOBA_WIDTH)


def _cparams(sem, vmem=V7X_VMEM_LIMIT):
    return pltpu.CompilerParams(dimension_semantics=sem, vmem_limit_bytes=vmem)


def _dot(a, b):
    return jnp.dot(a, b, preferred_element_type=_F32)


def _dot_nt(a, b):
    return lax.dot_general(a, b, (((1,), (1,)), ((), ())), preferred_element_type=_F32)


def _layer_norm(x, g, b):
    mu = jnp.mean(x, axis=-1, keepdims=True)
    xc = x - mu
    var = jnp.mean(xc * xc, axis=-1, keepdims=True)
    return xc * lax.rsqrt(var + LN_EPS) * g + b


_ROPE_BLOCKS = tuple(range(_CB_DQ, _CB_DV)) + tuple(range(_CB_MQ, _CB_MV))


def _inproj_kernel(x_ref, w_ref, cos_ref, sa_ref, sb_ref, o_ref, km_ref, *, tm):
    x = x_ref[...].astype(_CDT)
    cos, sa, sb = cos_ref[...], sa_ref[...], sb_ref[...]
    seg_w = 4 * LANES
    for seg in range(PROJ_WIDTH // seg_w):
        pseg = _dot(x, w_ref[:, seg * seg_w:(seg + 1) * seg_w])
        for c in range(4):
            cb = seg * 4 + c
            p = pseg[:, c * LANES:(c + 1) * LANES]
            if cb in _ROPE_BLOCKS:
                p = p * cos + pltpu.roll(p, 8, 1) * sa + pltpu.roll(p, LANES - 8, 1) * sb
            o_ref[cb] = p.astype(o_ref.dtype)
            if _CB_MK <= cb < _CB_MV:
                kc = slice((cb - _CB_MK) * LANES, (cb - _CB_MK + 1) * LANES)
                for blk in range(tm // MOBA_BLOCK):
                    rows = p[blk * MOBA_BLOCK:(blk + 1) * MOBA_BLOCK, :]
                    km_ref[blk, :, kc] = jnp.sum(rows, axis=0, keepdims=True) * (1.0 / MOBA_BLOCK)


def _rope_tables(seq):
    half = ROT_DIM // 2
    inv_freq = ROPE_THETA ** (-jnp.arange(0, ROT_DIM, 2, dtype=_F32) / ROT_DIM)
    ang = jnp.arange(seq, dtype=_F32)[:, None] * inv_freq[None, :]
    cos, sin = jnp.cos(ang), jnp.sin(ang)
    lane = jnp.arange(LANES) % HEAD_DIM
    first, second = lane < half, (lane >= half) & (lane < ROT_DIM)
    idx = jnp.where(first, lane, jnp.where(second, lane - half, 0))
    cos_t = jnp.where((first | second)[None, :], cos[:, idx], 1.0)
    sa_t = jnp.where(second[None, :], sin[:, idx], 0.0)
    sb_t = jnp.where(first[None, :], -sin[:, idx], 0.0)
    return cos_t, sa_t, sb_t


def _inproj(x2d, w_in, seq, *, tm):
    t = x2d.shape[0]
    cos_t, sa_t, sb_t = _rope_tables(seq)
    n_pos = seq // tm
    tab_spec = pl.BlockSpec((tm, LANES), lambda i: (i % n_pos, 0))
    return pl.pallas_call(
        functools.partial(_inproj_kernel, tm=tm),
        out_shape=(jax.ShapeDtypeStruct((PROJ_WIDTH // LANES, t, LANES), _CDT),
                   jax.ShapeDtypeStruct((t // MOBA_BLOCK, 1, MOBA_WIDTH), _F32)),
        grid=(t // tm,),
        in_specs=[pl.BlockSpec((tm, D_MODEL), lambda i: (i, 0)),
                  pl.BlockSpec((D_MODEL, PROJ_WIDTH), lambda i: (0, 0)),
                  tab_spec, tab_spec, tab_spec],
        out_specs=(pl.BlockSpec((PROJ_WIDTH // LANES, tm, LANES), lambda i: (0, i, 0)),
                   pl.BlockSpec((tm // MOBA_BLOCK, 1, MOBA_WIDTH), lambda i: (i, 0, 0))),
        compiler_params=_cparams(("parallel",)),
        name="inproj",
    )(x2d, w_in, cos_t, sa_t, sb_t)


def _exp2_parts(parts):
    chunks = [s[:, c:c + LANES] for s in parts for c in range(0, s.shape[1], LANES)]
    m = jnp.max(functools.reduce(jnp.maximum, chunks), axis=1, keepdims=True)
    return [jnp.exp2(s - m).astype(_CDT) for s in parts]


_SCORES_AHEAD = 1


def _causal_mask(tq):
    row = lax.broadcasted_iota(jnp.int32, (tq, tq), 0)
    col = lax.broadcasted_iota(jnp.int32, (tq, tq), 1)
    return col <= row


def _diff_attn_kernel(q_ref, k_ref, v_ref, lam_ref, g_ref, o_ref, *, seq, tq):
    lane = lax.broadcasted_iota(jnp.int32, (tq, LANES), 1)
    mask = _causal_mask(tq)
    lv = lam_ref[...]
    lam = (jnp.exp(jnp.sum(lv[0:1] * lv[1:2], axis=1, keepdims=True))
           - jnp.exp(jnp.sum(lv[2:3] * lv[3:4], axis=1, keepdims=True)) + LAMBDA_INIT)
    gain = g_ref[...] * (1.0 - LAMBDA_INIT)

    def with_ones(v):
        return jnp.concatenate([v, jnp.ones_like(v)], axis=1)

    def scores(i):
        rows = slice(i * tq, (i + 1) * tq)
        q = q_ref[rows, :]
        zero = jnp.zeros_like(q)
        q1 = jnp.where(lane < HEAD_DIM, q, zero)
        q2 = jnp.where(lane >= HEAD_DIM, q, zero)
        kd = k_ref[rows, :]
        s1 = [jnp.where(mask, _dot_nt(q1, kd), _NEG)]
        s2 = [jnp.where(mask, _dot_nt(q2, kd), _NEG)]
        if i:
            kp = k_ref[:i * tq, :]
            s1.append(_dot_nt(q1, kp))
            s2.append(_dot_nt(q2, kp))
        return s1, s2

    def finish(i, s1, s2):
        rows = slice(i * tq, (i + 1) * tq)
        v_all = with_ones(v_ref[:(i + 1) * tq, :])
        nums = []
        for parts in (s1, s2):
            es = _exp2_parts(parts)
            nl = _dot(jnp.concatenate(es[1:] + es[:1], axis=1), v_all)
            nums.append(nl[:, :LANES] / nl[:, LANES:LANES + 1])
        o = nums[0] - lam * nums[1]
        o = o * lax.rsqrt(jnp.mean(o * o, axis=1, keepdims=True) + RMS_EPS) * gain
        o_ref[rows, :] = o.astype(o_ref.dtype)

    n_tiles = seq // tq
    pending = [scores(i) for i in range(min(_SCORES_AHEAD, n_tiles))]
    for i in range(n_tiles):
        if i + _SCORES_AHEAD < n_tiles:
            pending.append(scores(i + _SCORES_AHEAD))
        finish(i, *pending.pop(0))


def _diff_attn(proj, lam_vecs, subln_g, batch, seq, *, tq):
    t = batch * seq
    blk = lambda cb: pl.BlockSpec((None, seq, LANES), lambda b, h: (cb + h, b, 0))
    return pl.pallas_call(
        functools.partial(_diff_attn_kernel, seq=seq, tq=tq),
        out_shape=jax.ShapeDtypeStruct((DIFF_WIDTH // LANES, t, LANES), _CDT),
        grid=(batch, DIFF_HEADS),
        in_specs=[blk(_CB_DQ), blk(_CB_DK), blk(_CB_DV),
                  pl.BlockSpec((4, HEAD_DIM), lambda b, h: (0, 0)),
                  pl.BlockSpec((1, 2 * HEAD_DIM), lambda b, h: (0, 0))],
        out_specs=pl.BlockSpec((None, seq, LANES), lambda b, h: (h, b, 0)),
        compiler_params=_cparams(("parallel", "parallel")),
        name="diff_attn",
    )(proj, proj, proj, lam_vecs, subln_g)


def _moba_selection(gate, n_past):
    blk = lax.broadcasted_iota(jnp.int32, gate.shape, 0)
    rank = jnp.zeros(gate.shape, _F32)
    for m_blk in range(n_past):
        gm = gate[m_blk:m_blk + 1, :]
        beats = (gm > gate) | ((gm == gate) & (m_blk < blk))
        rank = rank + jnp.where(beats, 1.0, 0.0)
    return jnp.where((blk < n_past) & (rank < MOBA_TOPK), 1.0, 0.0)


def _moba_kernel(q_ref, k_ref, v_ref, km_ref, o_ref, *, nb):
    tq = MOBA_BLOCK
    lane = lax.broadcasted_iota(jnp.int32, (tq, LANES), 1)
    mask = _causal_mask(tq)
    km = km_ref[...].astype(_CDT)
    in_head = [(lane >= hh * HEAD_DIM) & (lane < (hh + 1) * HEAD_DIM) for hh in range(2)]
    seq_lane = lax.broadcasted_iota(jnp.int32, (nb * tq, LANES), 1)
    v_ones = [jnp.where((seq_lane >= hh * HEAD_DIM) & (seq_lane < (hh + 1) * HEAD_DIM), v_ref[...],
                        jnp.ones((nb * tq, LANES), _CDT)) for hh in range(2)]

    def scores(i, hh):
        rows = slice(i * tq, (i + 1) * tq)
        q = q_ref[rows, :]
        qh = jnp.where(in_head[hh], q, jnp.zeros_like(q))
        parts = [jnp.where(mask, _dot_nt(qh, k_ref[rows, :]), _NEG)]
        gated = i > MOBA_TOPK
        if gated:
            sel = _moba_selection(_dot_nt(km, qh), i)
            sel_q = jnp.concatenate([sel, jnp.zeros((LANES - nb, tq), _F32)], axis=0).T
        for j in range(i):
            s = _dot_nt(qh, k_ref[j * tq:(j + 1) * tq, :])
            parts.append(jnp.where(sel_q[:, j:j + 1] > 0.5, s, _NEG) if gated else s)
        return parts

    def finish(i, hh, parts):
        es = _exp2_parts(parts)
        acc = _dot(jnp.concatenate(es[1:] + es[:1], axis=1), v_ones[hh][:(i + 1) * tq, :])
        sum_lane = (1 - hh) * HEAD_DIM
        return acc / acc[:, sum_lane:sum_lane + 1]

    units = [(i, hh) for i in range(nb) for hh in range(2)]
    pending = [scores(*u) for u in units[:_SCORES_AHEAD]]
    outs = {}
    for n, (i, hh) in enumerate(units):
        if n + _SCORES_AHEAD < len(units):
            pending.append(scores(*units[n + _SCORES_AHEAD]))
        outs[hh] = finish(i, hh, pending.pop(0))
        if hh == 1:
            o_ref[i * tq:(i + 1) * tq, :] = jnp.where(in_head[0], outs[0], outs[1]).astype(o_ref.dtype)


def _moba_attn(proj, kmean, batch, seq):
    nb = seq // MOBA_BLOCK
    t = batch * seq
    blk = lambda cb: pl.BlockSpec((None, seq, LANES), lambda b, p: (cb + p, b, 0))
    return pl.pallas_call(
        functools.partial(_moba_kernel, nb=nb),
        out_shape=jax.ShapeDtypeStruct((MOBA_WIDTH // LANES, t, LANES), _CDT),
        grid=(batch, MOBA_HEADS // 2),
        in_specs=[blk(_CB_MQ), blk(_CB_MK), blk(_CB_MV),
                  pl.BlockSpec((None, nb, LANES), lambda b, p: (b, 0, p))],
        out_specs=pl.BlockSpec((None, seq, LANES), lambda b, p: (p, b, 0)),
        compiler_params=_cparams(("parallel", "parallel")),
        name="moba_attn",
    )(proj, proj, proj, kmean)


def _mem_kv_kernel(mem_ref, g_ref, b_ref, w_ref, o_ref):
    mn = _layer_norm(mem_ref[...], g_ref[...], b_ref[...])
    o_ref[...] = _dot(mn.astype(_CDT), w_ref[...]).astype(o_ref.dtype)


def _mem_kv(mem2d, g, b, w_kv, *, tm):
    rows = mem2d.shape[0]
    vec = pl.BlockSpec((1, D_MODEL), lambda i: (0, 0))
    return pl.pallas_call(
        _mem_kv_kernel,
        out_shape=jax.ShapeDtypeStruct((rows, 2 * D_MODEL), _CDT),
        grid=(rows // tm,),
        in_specs=[pl.BlockSpec((tm, D_MODEL), lambda i: (i, 0)), vec, vec,
                  pl.BlockSpec((D_MODEL, 2 * D_MODEL), lambda i: (0, 0))],
        out_specs=pl.BlockSpec((tm, 2 * D_MODEL), lambda i: (i, 0)),
        compiler_params=_cparams(("parallel",)),
        name="mem_kv",
    )(mem2d, g, b, w_kv)


def _store_row_tiles(ref, val):
    n = val.shape[0]
    for c in range(ROW_TILE):
        ref[pl.ds(c, n, stride=ROW_TILE), :] = val[:, c * LANES:(c + 1) * LANES]


def _load_row_tiles(ref, n):
    return jnp.concatenate([ref[pl.ds(c, n, stride=ROW_TILE), :] for c in range(ROW_TILE)], axis=1)


def _interleave(stage_generators):
    live = list(stage_generators)
    while live:
        live = [g for g in live if next(g, _DONE) is not _DONE]


_DONE = object()


def _post_attn_kernel(x_ref, oa_ref, ob_ref, wout_ref, g1_ref, b1_ref, kv_ref, wq_ref, wo_ref,
                      g2_ref, b2_ref, wr_ref, br_ref, h2t_ref, idx_ref, wt_ref, cnt_ref, cnt_acc, *, tm, groups):
    @pl.when(pl.program_id(0) == 0)
    def _init_counts():
        cnt_acc[...] = jnp.zeros_like(cnt_acc)

    n = tm // groups
    lane = lax.broadcasted_iota(jnp.int32, (n, LANES), 1)
    lane_f = lane.astype(_F32)

    def row_group(g):
        rows = slice(g * n, (g + 1) * n)
        mixed = jnp.concatenate([ref[cb, rows, :] for ref in (oa_ref, ob_ref) for cb in range(ref.shape[0])], axis=1)
        mix = _dot(mixed, wout_ref[...])
        yield
        h1 = _layer_norm(DEEPNORM_ALPHA * x_ref[rows, :] + mix, g1_ref[...], b1_ref[...])
        q = (_dot(h1.astype(_CDT), wq_ref[...]) * (MEM_HEAD_DIM ** -0.5)).astype(_CDT)
        yield
        heads = []
        for h in range(MEM_HEADS):
            c0 = h * MEM_HEAD_DIM
            kh = kv_ref[:, c0:c0 + MEM_HEAD_DIM]
            vh = kv_ref[:, D_MODEL + c0:D_MODEL + c0 + MEM_HEAD_DIM]
            s = _dot_nt(q[:, c0:c0 + MEM_HEAD_DIM], kh)
            p = jnp.exp(s - jnp.max(s, axis=1, keepdims=True))
            o = _dot(p.astype(_CDT), vh) / jnp.sum(p, axis=1, keepdims=True)
            heads.append(o.astype(_CDT))
        yield
        xatt = _dot(jnp.concatenate(heads, axis=1), wo_ref[...])
        yield
        h2 = _layer_norm(DEEPNORM_ALPHA * h1 + xatt, g2_ref[...], b2_ref[...])
        _store_row_tiles(h2t_ref.at[pl.ds(g * n * ROW_TILE, n * ROW_TILE)], h2)
        logits = _dot(h2.astype(_CDT), wr_ref[...]) + br_ref[...]
        yield
        vals, idxs = [], []
        work = logits
        for _ in range(TOP_K):
            m = jnp.max(work, axis=1, keepdims=True)
            ik = jnp.min(jnp.where(work == m, lane_f, float(LANES)), axis=1, keepdims=True)
            vals.append(m)
            idxs.append(ik)
            work = jnp.where(lane_f == ik, -jnp.inf, work)
        exps = [jnp.exp(v - vals[0]) for v in vals]
        denom = exps[0] + exps[1] + exps[2] + exps[3]
        idx_w = jnp.zeros((n, LANES), _F32)
        wt_w = jnp.zeros((n, LANES), _F32)
        for k in range(TOP_K):
            idx_w = jnp.where(lane == k, idxs[k], idx_w)
            wt_w = jnp.where(lane == k, exps[k] / denom, wt_w)
        idx_ref[rows, :] = idx_w[:, :TOP_K].astype(jnp.int32)
        wt_ref[rows, :] = wt_w[:, :TOP_K]
        chosen = functools.reduce(jnp.add, [jnp.where(lane_f == ik, 1.0, 0.0) for ik in idxs])
        cnt_acc[...] = cnt_acc[...] + jnp.sum(chosen, axis=0, keepdims=True)

    _interleave(row_group(g) for g in range(groups))
    cnt_ref[...] = cnt_acc[...].astype(jnp.int32)


def _post_attn(x2d, out_a, out_b, w_out, g1, b1, kv, w_q, w_o, g2, b2, w_r, b_r, seq, mem_len, *, tm):
    t = x2d.shape[0]
    per_b = seq // tm
    full = lambda shape: pl.BlockSpec(shape, lambda i: (0,) * len(shape))
    vec = full((1, D_MODEL))
    return pl.pallas_call(
        functools.partial(_post_attn_kernel, tm=tm, groups=max(1, tm // MOBA_BLOCK)),
        out_shape=(jax.ShapeDtypeStruct((t * ROW_TILE, LANES), _F32),
                   jax.ShapeDtypeStruct((t, TOP_K), jnp.int32),
                   jax.ShapeDtypeStruct((t, TOP_K), _F32),
                   jax.ShapeDtypeStruct((1, LANES), jnp.int32)),
        grid=(t // tm,),
        in_specs=[pl.BlockSpec((tm, D_MODEL), lambda i: (i, 0)),
                  pl.BlockSpec((DIFF_WIDTH // LANES, tm, LANES), lambda i: (0, i, 0)),
                  pl.BlockSpec((MOBA_WIDTH // LANES, tm, LANES), lambda i: (0, i, 0)),
                  full((D_MODEL, D_MODEL)), vec, vec,
                  pl.BlockSpec((mem_len, 2 * D_MODEL), lambda i: (i // per_b, 0)),
                  full((D_MODEL, D_MODEL)), full((D_MODEL, D_MODEL)), vec, vec,
                  full((D_MODEL, LANES)), full((1, LANES))],
        out_specs=(pl.BlockSpec((tm * ROW_TILE, LANES), lambda i: (i, 0)),
                   pl.BlockSpec((tm, TOP_K), lambda i: (i, 0)),
                   pl.BlockSpec((tm, TOP_K), lambda i: (i, 0)),
                   pl.BlockSpec((1, LANES), lambda i: (0, 0))),
        scratch_shapes=[pltpu.VMEM((1, LANES), _F32)],
        compiler_params=_cparams(("arbitrary",)),
        name="post_attn",
    )(x2d, out_a, out_b, w_out, g1, b1, kv, w_q, w_o, g2, b2, w_r, b_r)


def _rank_kernel(idx_ref, off_ref, pos_ref, next_ref, *, tm):
    @pl.when(pl.program_id(0) == 0)
    def _init():
        next_ref[...] = off_ref[...]

    idx = idx_ref[...]
    lane = lax.broadcasted_iota(jnp.int32, (tm, LANES), 1)
    hits = [lane == idx[:, k:k + 1] for k in range(TOP_K)]
    sel = jnp.zeros((tm, LANES), _F32)
    for hit in hits:
        sel = sel + jnp.where(hit, 1.0, 0.0)
    row = lax.broadcasted_iota(jnp.int32, (tm, tm), 0)
    col = lax.broadcasted_iota(jnp.int32, (tm, tm), 1)
    earlier = jnp.where(col < row, 1.0, 0.0).astype(jnp.bfloat16)
    dense = _dot(earlier, sel.astype(jnp.bfloat16)) + next_ref[...]
    pos_w = jnp.zeros((tm, LANES), _F32)
    for k, hit in enumerate(hits):
        pk = jnp.sum(jnp.where(hit, dense, 0.0), axis=1, keepdims=True)
        pos_w = jnp.where(lane == k, pk, pos_w)
    pos_ref[...] = pos_w[:, :TOP_K].astype(jnp.int32)
    next_ref[...] = next_ref[...] + jnp.sum(sel, axis=0, keepdims=True)


def _rank(idx, offsets, *, tm):
    t = idx.shape[0]
    return pl.pallas_call(
        functools.partial(_rank_kernel, tm=tm),
        out_shape=jax.ShapeDtypeStruct((t, TOP_K), jnp.int32),
        grid=(t // tm,),
        in_specs=[pl.BlockSpec((tm, TOP_K), lambda i: (i, 0)), pl.BlockSpec((1, LANES), lambda i: (0, 0))],
        out_specs=pl.BlockSpec((tm, TOP_K), lambda i: (i, 0)),
        scratch_shapes=[pltpu.VMEM((1, LANES), _F32)],
        compiler_params=_cparams(("arbitrary",)),
        name="rank",
    )(idx, offsets)


_DMA_UNROLL = 8
_DMA_THREADS = 2


def _for_each_row(n_rows, fn):
    tokens_per_group = _DMA_UNROLL // TOP_K

    def group(g, c):
        for u in range(_DMA_UNROLL):
            fn(g * _DMA_UNROLL + u, g * tokens_per_group + u // TOP_K, u % TOP_K, u)
        return c

    lax.fori_loop(0, n_rows // _DMA_UNROLL, group, 0)


def _dispatch_kernel(meta_ref, h_ref, pos_ref, xs_ref, zero_ref, sem, zsem, *, tm):
    n_rows = tm * TOP_K

    def zero_copy(r):
        return pltpu.make_async_copy(zero_ref, xs_ref.at[r], zsem)

    @pl.when(pl.program_id(0) == 0)
    def _zero_fill():
        zero_ref[...] = jnp.zeros_like(zero_ref)

        def per_expert(e, c):
            lo = meta_ref[e] + meta_ref[N_EXPERTS + e]
            hi = meta_ref[e] + meta_ref[2 * N_EXPERTS + e]
            lax.fori_loop(lo, hi, lambda r, c2: (zero_copy(r).start(), c2)[1], 0)
            lax.fori_loop(lo, hi, lambda r, c2: (zero_copy(r).wait(), c2)[1], 0)
            return c

        lax.fori_loop(0, N_EXPERTS, per_expert, 0)

    def row_copy(r, t, k, u):
        return pltpu.make_async_copy(h_ref.at[t], xs_ref.at[pos_ref[r]], sem)

    _for_each_row(n_rows, lambda r, t, k, u: row_copy(r, t, k, u).start(priority=u % _DMA_THREADS))
    _for_each_row(n_rows, lambda r, t, k, u: row_copy(r, t, k, u).wait())


def _dispatch(meta, h2t, pos_flat, n_sorted_rows, *, tm):
    t = h2t.shape[0]
    return pl.pallas_call(
        functools.partial(_dispatch_kernel, tm=tm),
        out_shape=jax.ShapeDtypeStruct((n_sorted_rows, ROW_TILE, LANES), _F32),
        grid_spec=pltpu.PrefetchScalarGridSpec(
            num_scalar_prefetch=1,
            grid=(t // tm,),
            in_specs=[pl.BlockSpec((tm, ROW_TILE, LANES), lambda i, meta: (i, 0, 0)),
                      pl.BlockSpec((tm * TOP_K,), lambda i, meta: (i,), memory_space=pltpu.SMEM)],
            out_specs=pl.BlockSpec(memory_space=pl.ANY),
            scratch_shapes=[pltpu.VMEM((ROW_TILE, LANES), _F32),
                            pltpu.SemaphoreType.DMA, pltpu.SemaphoreType.DMA]),
        compiler_params=_cparams(("arbitrary",)),
        name="dispatch",
    )(meta, h2t, pos_flat)


_DEINT = 2 * LANES


def _deinterleave_matrix():
    k = jnp.arange(_DEINT)[:, None]
    n = jnp.arange(_DEINT)[None, :]
    src = jnp.where(n < LANES, 2 * n, 2 * (n - LANES) + 1)
    return (k == src).astype(_CDT)


def _experts_kernel(te_ref, nv_ref, xs_ref, perm_ref, w1_ref, b1g_ref, b1l_ref, w2_ref, b2_ref, y_ref,
                    wg_ref, wl_ref, w2c_ref, *, tm):
    i = pl.program_id(0)
    live = i < nv_ref[0]
    new_expert = (i == 0) | (te_ref[i] != te_ref[jnp.maximum(i - 1, 0)])

    @pl.when(live & new_expert)
    def _stage_weights():
        for c in range(2 * D_FF // _DEINT):
            both = _dot(w1_ref[0, :, c * _DEINT:(c + 1) * _DEINT].astype(_CDT), perm_ref[...])
            wg_ref[:, c * LANES:(c + 1) * LANES] = both[:, :LANES].astype(_CDT)
            wl_ref[:, c * LANES:(c + 1) * LANES] = both[:, LANES:].astype(_CDT)
        w2c_ref[...] = w2_ref[0].astype(_CDT)

    @pl.when(live)
    def _mlp():
        x = _load_row_tiles(xs_ref, tm).astype(_CDT)
        glu = jnp.minimum(_dot(x, wg_ref[...]) + b1g_ref[0], SWIGLU_LIMIT)
        lin = jnp.clip(_dot(x, wl_ref[...]) + b1l_ref[0], -SWIGLU_LIMIT, SWIGLU_LIMIT)
        act = glu * jax.nn.sigmoid(SWIGLU_ALPHA * glu) * (lin + 1.0)
        _store_row_tiles(y_ref, _dot(act.astype(_CDT), w2c_ref[...]) + b2_ref[0])

    @pl.when(jnp.logical_not(live))
    def _unused_tile():
        y_ref[...] = jnp.zeros_like(y_ref)


def _experts(tile_expert, n_valid, xs, w1, b1g, b1l, w2, b2, *, tm):
    n_tiles = xs.shape[0] // (tm * ROW_TILE)
    by_expert = lambda shape: pl.BlockSpec((1,) + shape, lambda i, te, nv: (te[i], 0, 0))
    return pl.pallas_call(
        functools.partial(_experts_kernel, tm=tm),
        out_shape=jax.ShapeDtypeStruct(xs.shape, _F32),
        grid_spec=pltpu.PrefetchScalarGridSpec(
            num_scalar_prefetch=2,
            grid=(n_tiles,),
            in_specs=[pl.BlockSpec((tm * ROW_TILE, LANES), lambda i, te, nv: (jnp.minimum(i, nv[0] - 1), 0)),
                      pl.BlockSpec((_DEINT, _DEINT), lambda i, te, nv: (0, 0)),
                      by_expert((D_MODEL, 2 * D_FF)),
                      by_expert((1, D_FF)), by_expert((1, D_FF)),
                      by_expert((D_FF, D_MODEL)), by_expert((1, D_MODEL))],
            out_specs=pl.BlockSpec((tm * ROW_TILE, LANES), lambda i, te, nv: (i, 0)),
            scratch_shapes=[pltpu.VMEM((D_MODEL, D_FF), _CDT), pltpu.VMEM((D_MODEL, D_FF), _CDT),
                            pltpu.VMEM((D_FF, D_MODEL), _CDT)]),
        compiler_params=_cparams(("arbitrary",)),
        name="experts",
    )(tile_expert, n_valid, xs, _deinterleave_matrix(), w1, b1g, b1l, w2, b2)


def _combine_kernel(pos_ref, nxt_ref, wt_ref, h2t_ref, g_ref, b_ref, ys_ref, o_ref, buf_ref, sem, *, tm):
    n_rows = tm * TOP_K
    i = pl.program_id(0)
    slot = i & 1

    def row_copy(p_ref, s, r, t, k):
        tile_rows = pl.ds(pl.multiple_of(t * ROW_TILE, ROW_TILE), ROW_TILE)
        return pltpu.make_async_copy(ys_ref.at[p_ref[r]], buf_ref.at[s, k, tile_rows], sem.at[s])

    def start_gather(p_ref, s):
        _for_each_row(n_rows, lambda r, t, k, u: row_copy(p_ref, s, r, t, k).start(priority=u % _DMA_THREADS))

    @pl.when(i == 0)
    def _first():
        start_gather(pos_ref, 0)

    _for_each_row(n_rows, lambda r, t, k, u: row_copy(pos_ref, slot, r, t, k).wait())

    for r in range(n_rows):
        row_copy(nxt_ref, 1 - slot, r, r // TOP_K, r % TOP_K).start(priority=r % _DMA_THREADS)
    wt = wt_ref[...]
    ffn = wt[:, 0:1] * _load_row_tiles(buf_ref.at[slot, 0], tm)
    for k in range(1, TOP_K):
        ffn = ffn + wt[:, k:k + 1] * _load_row_tiles(buf_ref.at[slot, k], tm)
    o_ref[...] = _layer_norm(DEEPNORM_ALPHA * _load_row_tiles(h2t_ref, tm) + ffn, g_ref[...], b_ref[...])

    @pl.when(i + 1 == pl.num_programs(0))
    def _drain():
        _for_each_row(n_rows, lambda r, t, k, u: row_copy(nxt_ref, 1 - slot, r, t, k).wait())


def _combine(pos_flat, wts, h2t, g3, b3, ys, *, tm):
    t = h2t.shape[0] // ROW_TILE
    n_steps = t // tm
    vec = pl.BlockSpec((1, D_MODEL), lambda i: (0, 0))
    return pl.pallas_call(
        functools.partial(_combine_kernel, tm=tm),
        out_shape=jax.ShapeDtypeStruct((t, D_MODEL), _F32),
        grid=(n_steps,),
        in_specs=[pl.BlockSpec((tm * TOP_K,), lambda i: (i,), memory_space=pltpu.SMEM),
                  pl.BlockSpec((tm * TOP_K,), lambda i: (jnp.minimum(i + 1, n_steps - 1),), memory_space=pltpu.SMEM),
                  pl.BlockSpec((tm, TOP_K), lambda i: (i, 0)),
                  pl.BlockSpec((tm * ROW_TILE, LANES), lambda i: (i, 0)), vec, vec,
                  pl.BlockSpec(memory_space=pl.ANY)],
        out_specs=pl.BlockSpec((tm, D_MODEL), lambda i: (i, 0)),
        scratch_shapes=[pltpu.VMEM((2, TOP_K, tm * ROW_TILE, LANES), _F32), pltpu.SemaphoreType.DMA((2,))],
        compiler_params=_cparams(("arbitrary",)),
        name="combine",
    )(pos_flat, pos_flat, wts, h2t, g3, b3, ys)


def _tiles(seq):
    big = 512 if seq % 512 == 0 else MOBA_BLOCK
    post = 1024 if seq % 1024 == 0 else big
    return dict(inproj=big, attn_q=MOBA_BLOCK, post=post, rank=big, dispatch=big, expert=512, combine=MOBA_BLOCK)


def kernel(x, mem, w_in, diff_lambda_q1, diff_lambda_k1, diff_lambda_q2, diff_lambda_k2, diff_subln_g,
           w_mix_out, ln1_g, ln1_b, mem_ln_g, mem_ln_b, w_mem_q, w_mem_kv, w_mem_o, ln2_g, ln2_b, w_router,
           b_router, w_mlp1, b_mlp1, w_mlp2, b_mlp2, ln3_g, ln3_b):
    batch, seq, d = x.shape
    mem_len = mem.shape[1]
    assert d == D_MODEL and seq % MOBA_BLOCK == 0 and w_in.shape[0] == 1
    t = batch * seq
    ts = _tiles(seq)
    row = lambda v: v.reshape(1, -1).astype(_F32)

    col = jnp.arange(PROJ_WIDTH)
    is_q = (col < DIFF_WIDTH) | ((col >= 3 * DIFF_WIDTH) & (col < 3 * DIFF_WIDTH + MOBA_WIDTH))
    w_in_c = (w_in[0] * jnp.where(is_q, HEAD_DIM ** -0.5 * math.log2(math.e), 1.0)[None, :]).astype(_CDT)
    lam_vecs = jnp.stack([diff_lambda_q1[0], diff_lambda_k1[0], diff_lambda_q2[0], diff_lambda_k2[0]]).astype(_F32)
    w_r = jnp.zeros((D_MODEL, LANES), _F32).at[:, :N_EXPERTS].set(w_router[0]).astype(_CDT)
    b_r = jnp.full((1, LANES), _NEG, _F32).at[0, :N_EXPERTS].set(b_router[0])
    b1g = b_mlp1[0][:, None, 0::2].astype(_F32)
    b1l = b_mlp1[0][:, None, 1::2].astype(_F32)
    b2 = b_mlp2[0][:, None, :].astype(_F32)

    x2d = x.reshape(t, d)
    proj, kmean = _inproj(x2d, w_in_c, seq, tm=ts["inproj"])
    kmean = kmean.reshape(batch, seq // MOBA_BLOCK, MOBA_WIDTH)
    out_a = _diff_attn(proj, lam_vecs, row(diff_subln_g[0]), batch, seq, tq=ts["attn_q"])
    out_b = _moba_attn(proj, kmean, batch, seq)
    kv = _mem_kv(mem.reshape(batch * mem_len, d), row(mem_ln_g), row(mem_ln_b), w_mem_kv[0].astype(_CDT),
                 tm=mem_len)
    h2t, top_idx, top_w, counts = _post_attn(
        x2d, out_a, out_b, w_mix_out[0].astype(_CDT), row(ln1_g[0]), row(ln1_b[0]), kv,
        w_mem_q[0].astype(_CDT), w_mem_o[0].astype(_CDT), row(ln2_g[0]), row(ln2_b[0]), w_r, b_r,
        seq, mem_len, tm=ts["post"])

    tmx = ts["expert"]
    cnt = counts[0, :N_EXPERTS]
    padded = (cnt + tmx - 1) // tmx * tmx
    ends = jnp.cumsum(padded)
    offsets = ends - padded
    meta = jnp.concatenate([offsets, cnt, padded]).astype(jnp.int32)
    n_sorted_rows = t * TOP_K + N_EXPERTS * tmx
    n_tiles = n_sorted_rows // tmx
    tile_start = jnp.arange(n_tiles, dtype=jnp.int32) * tmx
    tile_expert = jnp.minimum(jnp.sum(tile_start[:, None] >= ends[None, :], axis=1), N_EXPERTS - 1).astype(jnp.int32)
    n_valid = (ends[-1:] // tmx).astype(jnp.int32)
    offsets_row = jnp.zeros((1, LANES), _F32).at[0, :N_EXPERTS].set(offsets.astype(_F32))
    pos = _rank(top_idx, offsets_row, tm=ts["rank"]).reshape(-1)

    xs = _dispatch(meta, h2t.reshape(t, ROW_TILE, LANES), pos, n_sorted_rows, tm=ts["dispatch"])
    ys = _experts(tile_expert, n_valid, xs.reshape(n_sorted_rows * ROW_TILE, LANES), w_mlp1[0], b1g, b1l,
                  w_mlp2[0], b2, tm=tmx)
    out = _combine(pos, top_w, h2t, row(ln3_g[0]), row(ln3_b[0]),
                   ys.reshape(n_sorted_rows, ROW_TILE, LANES), tm=ts["combine"])
    return out.reshape(batch, seq, d)
```

```python
import functools
import math

import jax
import jax.numpy as jnp
from jax import lax
from jax.experimental import pallas as pl
from jax.experimental.pallas import tpu as pltpu

D_MODEL = 1024
DIFF_HEADS = 4
HEAD_DIM = 64
DIFF_WIDTH = DIFF_HEADS * 2 * HEAD_DIM
MOBA_HEADS = 8
MOBA_WIDTH = MOBA_HEADS * HEAD_DIM
MOBA_BLOCK = 256
MOBA_TOPK = 3
ROPE_THETA = 500000.0
ROT_DIM = HEAD_DIM // 4
MEM_HEADS = 4
MEM_HEAD_DIM = D_MODEL // MEM_HEADS
N_EXPERTS = 32
TOP_K = 4
D_FF = D_MODEL
SWIGLU_ALPHA = 1.702
SWIGLU_LIMIT = 7.0
LN_EPS = 1e-5
RMS_EPS = 1e-5
DEEPNORM_ALPHA = 2.0 ** 0.25
LAMBDA_INIT = 0.8 - 0.6 * math.exp(0.0)

LANES = 128
ROW_TILE = 8
V7X_VMEM_LIMIT = 56 * 1024 * 1024

_CDT = jnp.bfloat16
_F32 = jnp.float32
_NEG = -1e30

_CB_DQ, _CB_DK, _CB_DV = 0, 4, 8
_CB_MQ, _CB_MK, _CB_MV = 12, 16, 20
PROJ_WIDTH = 3 * (DIFF_WIDTH + MOBA_WIDTH)


def _cparams(sem, vmem=V7X_VMEM_LIMIT):
    return pltpu.CompilerParams(dimension_semantics=sem, vmem_limit_bytes=vmem)


def _dot(a, b):
    return jnp.dot(a, b, preferred_element_type=_F32)


def _dot_nt(a, b):
    return lax.dot_general(a, b, (((1,), (1,)), ((), ())), preferred_element_type=_F32)


def _layer_norm(x, g, b):
    mu = jnp.mean(x, axis=-1, keepdims=True)
    xc = x - mu
    var = jnp.mean(xc * xc, axis=-1, keepdims=True)
    return xc * lax.rsqrt(var + LN_EPS) * g + b


_ROPE_BLOCKS = tuple(range(_CB_DQ, _CB_DV)) + tuple(range(_CB_MQ, _CB_MV))


def _inproj_kernel(x_ref, w_ref, cos_ref, sa_ref, sb_ref, o_ref, km_ref, *, tm):
    x = x_ref[...].astype(_CDT)
    cos, sa, sb = cos_ref[...], sa_ref[...], sb_ref[...]
    seg_w = 4 * LANES
    for seg in range(PROJ_WIDTH // seg_w):
        pseg = _dot(x, w_ref[:, seg * seg_w:(seg + 1) * seg_w])
        for c in range(4):
            cb = seg * 4 + c
            p = pseg[:, c * LANES:(c + 1) * LANES]
            if cb in _ROPE_BLOCKS:
                p = p * cos + pltpu.roll(p, 8, 1) * sa + pltpu.roll(p, LANES - 8, 1) * sb
            o_ref[cb] = p.astype(o_ref.dtype)
            if _CB_MK <= cb < _CB_MV:
                kc = slice((cb - _CB_MK) * LANES, (cb - _CB_MK + 1) * LANES)
                for blk in range(tm // MOBA_BLOCK):
                    rows = p[blk * MOBA_BLOCK:(blk + 1) * MOBA_BLOCK, :]
                    km_ref[blk, :, kc] = jnp.sum(rows, axis=0, keepdims=True) * (1.0 / MOBA_BLOCK)


def _rope_tables(seq):
    half = ROT_DIM // 2
    inv_freq = ROPE_THETA ** (-jnp.arange(0, ROT_DIM, 2, dtype=_F32) / ROT_DIM)
    ang = jnp.arange(seq, dtype=_F32)[:, None] * inv_freq[None, :]
    cos, sin = jnp.cos(ang), jnp.sin(ang)
    lane = jnp.arange(LANES) % HEAD_DIM
    first, second = lane < half, (lane >= half) & (lane < ROT_DIM)
    idx = jnp.where(first, lane, jnp.where(second, lane - half, 0))
    cos_t = jnp.where((first | second)[None, :], cos[:, idx], 1.0)
    sa_t = jnp.where(second[None, :], sin[:, idx], 0.0)
    sb_t = jnp.where(first[None, :], -sin[:, idx], 0.0)
    return cos_t, sa_t, sb_t


def _inproj(x2d, w_in, seq, *, tm):
    t = x2d.shape[0]
    cos_t, sa_t, sb_t = _rope_tables(seq)
    n_pos = seq // tm
    tab_spec = pl.BlockSpec((tm, LANES), lambda i: (i % n_pos, 0))
    return pl.pallas_call(
        functools.partial(_inproj_kernel, tm=tm),
        out_shape=(jax.ShapeDtypeStruct((PROJ_WIDTH // LANES, t, LANES), _CDT),
                   jax.ShapeDtypeStruct((t // MOBA_BLOCK, 1, MOBA_WIDTH), _F32)),
        grid=(t // tm,),
        in_specs=[pl.BlockSpec((tm, D_MODEL), lambda i: (i, 0)),
                  pl.BlockSpec((D_MODEL, PROJ_WIDTH), lambda i: (0, 0)),
                  tab_spec, tab_spec, tab_spec],
        out_specs=(pl.BlockSpec((PROJ_WIDTH // LANES, tm, LANES), lambda i: (0, i, 0)),
                   pl.BlockSpec((tm // MOBA_BLOCK, 1, MOBA_WIDTH), lambda i: (i, 0, 0))),
        compiler_params=_cparams(("parallel",)),
        name="inproj",
    )(x2d, w_in, cos_t, sa_t, sb_t)


def _exp2_parts(parts):
    chunks = [s[:, c:c + LANES] for s in parts for c in range(0, s.shape[1], LANES)]
    m = jnp.max(functools.reduce(jnp.maximum, chunks), axis=1, keepdims=True)
    return [jnp.exp2(s - m).astype(_CDT) for s in parts]


_SCORES_AHEAD = 1


def _causal_mask(tq):
    row = lax.broadcasted_iota(jnp.int32, (tq, tq), 0)
    col = lax.broadcasted_iota(jnp.int32, (tq, tq), 1)
    return col <= row


def _diff_attn_kernel(q_ref, k_ref, v_ref, lam_ref, g_ref, o_ref, *, seq, tq):
    lane = lax.broadcasted_iota(jnp.int32, (tq, LANES), 1)
    mask = _causal_mask(tq)
    lv = lam_ref[...]
    lam = (jnp.exp(jnp.sum(lv[0:1] * lv[1:2], axis=1, keepdims=True))
           - jnp.exp(jnp.sum(lv[2:3] * lv[3:4], axis=1, keepdims=True)) + LAMBDA_INIT)
    gain = g_ref[...] * (1.0 - LAMBDA_INIT)

    def with_ones(v):
        return jnp.concatenate([v, jnp.ones_like(v)], axis=1)

    def scores(i):
        rows = slice(i * tq, (i + 1) * tq)
        q = q_ref[rows, :]
        zero = jnp.zeros_like(q)
        q1 = jnp.where(lane < HEAD_DIM, q, zero)
        q2 = jnp.where(lane >= HEAD_DIM, q, zero)
        kd = k_ref[rows, :]
        s1 = [jnp.where(mask, _dot_nt(q1, kd), _NEG)]
        s2 = [jnp.where(mask, _dot_nt(q2, kd), _NEG)]
        if i:
            kp = k_ref[:i * tq, :]
            s1.append(_dot_nt(q1, kp))
            s2.append(_dot_nt(q2, kp))
        return s1, s2

    def finish(i, s1, s2):
        rows = slice(i * tq, (i + 1) * tq)
        v_all = with_ones(v_ref[:(i + 1) * tq, :])
        nums = []
        for parts in (s1, s2):
            es = _exp2_parts(parts)
            nl = _dot(jnp.concatenate(es[1:] + es[:1], axis=1), v_all)
            nums.append(nl[:, :LANES] / nl[:, LANES:LANES + 1])
        o = nums[0] - lam * nums[1]
        o = o * lax.rsqrt(jnp.mean(o * o, axis=1, keepdims=True) + RMS_EPS) * gain
        o_ref[rows, :] = o.astype(o_ref.dtype)

    n_tiles = seq // tq
    pending = [scores(i) for i in range(min(_SCORES_AHEAD, n_tiles))]
    for i in range(n_tiles):
        if i + _SCORES_AHEAD < n_tiles:
            pending.append(scores(i + _SCORES_AHEAD))
        finish(i, *pending.pop(0))


def _diff_attn(proj, lam_vecs, subln_g, batch, seq, *, tq):
    t = batch * seq
    blk = lambda cb: pl.BlockSpec((None, seq, LANES), lambda b, h: (cb + h, b, 0))
    return pl.pallas_call(
        functools.partial(_diff_attn_kernel, seq=seq, tq=tq),
        out_shape=jax.ShapeDtypeStruct((DIFF_WIDTH // LANES, t, LANES), _CDT),
        grid=(batch, DIFF_HEADS),
        in_specs=[blk(_CB_DQ), blk(_CB_DK), blk(_CB_DV),
                  pl.BlockSpec((4, HEAD_DIM), lambda b, h: (0, 0)),
                  pl.BlockSpec((1, 2 * HEAD_DIM), lambda b, h: (0, 0))],
        out_specs=pl.BlockSpec((None, seq, LANES), lambda b, h: (h, b, 0)),
        compiler_params=_cparams(("parallel", "parallel")),
        name="diff_attn",
    )(proj, proj, proj, lam_vecs, subln_g)


def _moba_selection(gate, n_past):
    blk = lax.broadcasted_iota(jnp.int32, gate.shape, 0)
    rank = jnp.zeros(gate.shape, _F32)
    for m_blk in range(n_past):
        gm = gate[m_blk:m_blk + 1, :]
        beats = (gm > gate) | ((gm == gate) & (m_blk < blk))
        rank = rank + jnp.where(beats, 1.0, 0.0)
    return jnp.where((blk < n_past) & (rank < MOBA_TOPK), 1.0, 0.0)


def _moba_kernel(q_ref, k_ref, v_ref, km_ref, o_ref, *, nb):
    tq = MOBA_BLOCK
    lane = lax.broadcasted_iota(jnp.int32, (tq, LANES), 1)
    mask = _causal_mask(tq)
    km = km_ref[...].astype(_CDT)
    in_head = [(lane >= hh * HEAD_DIM) & (lane < (hh + 1) * HEAD_DIM) for hh in range(2)]
    seq_lane = lax.broadcasted_iota(jnp.int32, (nb * tq, LANES), 1)
    v_ones = [jnp.where((seq_lane >= hh * HEAD_DIM) & (seq_lane < (hh + 1) * HEAD_DIM), v_ref[...],
                        jnp.ones((nb * tq, LANES), _CDT)) for hh in range(2)]

    def scores(i, hh):
        rows = slice(i * tq, (i + 1) * tq)
        q = q_ref[rows, :]
        qh = jnp.where(in_head[hh], q, jnp.zeros_like(q))
        parts = [jnp.where(mask, _dot_nt(qh, k_ref[rows, :]), _NEG)]
        gated = i > MOBA_TOPK
        if gated:
            sel = _moba_selection(_dot_nt(km, qh), i)
            sel_q = jnp.concatenate([sel, jnp.zeros((LANES - nb, tq), _F32)], axis=0).T
        for j in range(i):
            s = _dot_nt(qh, k_ref[j * tq:(j + 1) * tq, :])
            parts.append(jnp.where(sel_q[:, j:j + 1] > 0.5, s, _NEG) if gated else s)
        return parts

    def finish(i, hh, parts):
        es = _exp2_parts(parts)
        acc = _dot(jnp.concatenate(es[1:] + es[:1], axis=1), v_ones[hh][:(i + 1) * tq, :])
        sum_lane = (1 - hh) * HEAD_DIM
        return acc / acc[:, sum_lane:sum_lane + 1]

    units = [(i, hh) for i in range(nb) for hh in range(2)]
    pending = [scores(*u) for u in units[:_SCORES_AHEAD]]
    outs = {}
    for n, (i, hh) in enumerate(units):
        if n + _SCORES_AHEAD < len(units):
            pending.append(scores(*units[n + _SCORES_AHEAD]))
        outs[hh] = finish(i, hh, pending.pop(0))
        if hh == 1:
            o_ref[i * tq:(i + 1) * tq, :] = jnp.where(in_head[0], outs[0], outs[1]).astype(o_ref.dtype)


def _moba_attn(proj, kmean, batch, seq):
    nb = seq // MOBA_BLOCK
    t = batch * seq
    blk = lambda cb: pl.BlockSpec((None, seq, LANES), lambda b, p: (cb + p, b, 0))
    return pl.pallas_call(
        functools.partial(_moba_kernel, nb=nb),
        out_shape=jax.ShapeDtypeStruct((MOBA_WIDTH // LANES, t, LANES), _CDT),
        grid=(batch, MOBA_HEADS // 2),
        in_specs=[blk(_CB_MQ), blk(_CB_MK), blk(_CB_MV),
                  pl.BlockSpec((None, nb, LANES), lambda b, p: (b, 0, p))],
        out_specs=pl.BlockSpec((None, seq, LANES), lambda b, p: (p, b, 0)),
        compiler_params=_cparams(("parallel", "parallel")),
        name="moba_attn",
    )(proj, proj, proj, kmean)


def _mem_kv_kernel(mem_ref, g_ref, b_ref, w_ref, o_ref):
    mn = _layer_norm(mem_ref[...], g_ref[...], b_ref[...])
    o_ref[...] = _dot(mn.astype(_CDT), w_ref[...]).astype(o_ref.dtype)


def _mem_kv(mem2d, g, b, w_kv, *, tm):
    rows = mem2d.shape[0]
    vec = pl.BlockSpec((1, D_MODEL), lambda i: (0, 0))
    return pl.pallas_call(
        _mem_kv_kernel,
        out_shape=jax.ShapeDtypeStruct((rows, 2 * D_MODEL), _CDT),
        grid=(rows // tm,),
        in_specs=[pl.BlockSpec((tm, D_MODEL), lambda i: (i, 0)), vec, vec,
                  pl.BlockSpec((D_MODEL, 2 * D_MODEL), lambda i: (0, 0))],
        out_specs=pl.BlockSpec((tm, 2 * D_MODEL), lambda i: (i, 0)),
        compiler_params=_cparams(("parallel",)),
        name="mem_kv",
    )(mem2d, g, b, w_kv)


def _store_row_tiles(ref, val):
    n = val.shape[0]
    for c in range(ROW_TILE):
        ref[pl.ds(c, n, stride=ROW_TILE), :] = val[:, c * LANES:(c + 1) * LANES]


def _load_row_tiles(ref, n):
    return jnp.concatenate([ref[pl.ds(c, n, stride=ROW_TILE), :] for c in range(ROW_TILE)], axis=1)


def _interleave(stage_generators):
    live = list(stage_generators)
    while live:
        live = [g for g in live if next(g, _DONE) is not _DONE]


_DONE = object()


def _post_attn_kernel(x_ref, oa_ref, ob_ref, wout_ref, g1_ref, b1_ref, kv_ref, wq_ref, wo_ref,
                      g2_ref, b2_ref, wr_ref, br_ref, h2t_ref, idx_ref, wt_ref, cnt_ref, cnt_acc, *, tm, groups):
    @pl.when(pl.program_id(0) == 0)
    def _init_counts():
        cnt_acc[...] = jnp.zeros_like(cnt_acc)

    n = tm // groups
    lane = lax.broadcasted_iota(jnp.int32, (n, LANES), 1)
    lane_f = lane.astype(_F32)

    def row_group(g):
        rows = slice(g * n, (g + 1) * n)
        mixed = jnp.concatenate([ref[cb, rows, :] for ref in (oa_ref, ob_ref) for cb in range(ref.shape[0])], axis=1)
        mix = _dot(mixed, wout_ref[...])
        yield
        h1 = _layer_norm(DEEPNORM_ALPHA * x_ref[rows, :] + mix, g1_ref[...], b1_ref[...])
        q = (_dot(h1.astype(_CDT), wq_ref[...]) * (MEM_HEAD_DIM ** -0.5)).astype(_CDT)
        yield
        heads = []
        for h in range(MEM_HEADS):
            c0 = h * MEM_HEAD_DIM
            kh = kv_ref[:, c0:c0 + MEM_HEAD_DIM]
            vh = kv_ref[:, D_MODEL + c0:D_MODEL + c0 + MEM_HEAD_DIM]
            s = _dot_nt(q[:, c0:c0 + MEM_HEAD_DIM], kh)
            p = jnp.exp(s - jnp.max(s, axis=1, keepdims=True))
            o = _dot(p.astype(_CDT), vh) / jnp.sum(p, axis=1, keepdims=True)
            heads.append(o.astype(_CDT))
        yield
        xatt = _dot(jnp.concatenate(heads, axis=1), wo_ref[...])
        yield
        h2 = _layer_norm(DEEPNORM_ALPHA * h1 + xatt, g2_ref[...], b2_ref[...])
        _store_row_tiles(h2t_ref.at[pl.ds(g * n * ROW_TILE, n * ROW_TILE)], h2)
        logits = _dot(h2.astype(_CDT), wr_ref[...]) + br_ref[...]
        yield
        vals, idxs = [], []
        work = logits
        for _ in range(TOP_K):
            m = jnp.max(work, axis=1, keepdims=True)
            ik = jnp.min(jnp.where(work == m, lane_f, float(LANES)), axis=1, keepdims=True)
            vals.append(m)
            idxs.append(ik)
            work = jnp.where(lane_f == ik, -jnp.inf, work)
        exps = [jnp.exp(v - vals[0]) for v in vals]
        denom = exps[0] + exps[1] + exps[2] + exps[3]
        idx_w = jnp.zeros((n, LANES), _F32)
        wt_w = jnp.zeros((n, LANES), _F32)
        for k in range(TOP_K):
            idx_w = jnp.where(lane == k, idxs[k], idx_w)
            wt_w = jnp.where(lane == k, exps[k] / denom, wt_w)
        idx_ref[rows, :] = idx_w[:, :TOP_K].astype(jnp.int32)
        wt_ref[rows, :] = wt_w[:, :TOP_K]
        chosen = functools.reduce(jnp.add, [jnp.where(lane_f == ik, 1.0, 0.0) for ik in idxs])
        cnt_acc[...] = cnt_acc[...] + jnp.sum(chosen, axis=0, keepdims=True)

    _interleave(row_group(g) for g in range(groups))
    cnt_ref[...] = cnt_acc[...].astype(jnp.int32)


def _post_attn(x2d, out_a, out_b, w_out, g1, b1, kv, w_q, w_o, g2, b2, w_r, b_r, seq, mem_len, *, tm):
    t = x2d.shape[0]
    per_b = seq // tm
    full = lambda shape: pl.BlockSpec(shape, lambda i: (0,) * len(shape))
    vec = full((1, D_MODEL))
    return pl.pallas_call(
        functools.partial(_post_attn_kernel, tm=tm, groups=max(1, tm // MOBA_BLOCK)),
        out_shape=(jax.ShapeDtypeStruct((t * ROW_TILE, LANES), _F32),
                   jax.ShapeDtypeStruct((t, TOP_K), jnp.int32),
                   jax.ShapeDtypeStruct((t, TOP_K), _F32),
                   jax.ShapeDtypeStruct((1, LANES), jnp.int32)),
        grid=(t // tm,),
        in_specs=[pl.BlockSpec((tm, D_MODEL), lambda i: (i, 0)),
                  pl.BlockSpec((DIFF_WIDTH // LANES, tm, LANES), lambda i: (0, i, 0)),
                  pl.BlockSpec((MOBA_WIDTH // LANES, tm, LANES), lambda i: (0, i, 0)),
                  full((D_MODEL, D_MODEL)), vec, vec,
                  pl.BlockSpec((mem_len, 2 * D_MODEL), lambda i: (i // per_b, 0)),
                  full((D_MODEL, D_MODEL)), full((D_MODEL, D_MODEL)), vec, vec,
                  full((D_MODEL, LANES)), full((1, LANES))],
        out_specs=(pl.BlockSpec((tm * ROW_TILE, LANES), lambda i: (i, 0)),
                   pl.BlockSpec((tm, TOP_K), lambda i: (i, 0)),
                   pl.BlockSpec((tm, TOP_K), lambda i: (i, 0)),
                   pl.BlockSpec((1, LANES), lambda i: (0, 0))),
        scratch_shapes=[pltpu.VMEM((1, LANES), _F32)],
        compiler_params=_cparams(("arbitrary",)),
        name="post_attn",
    )(x2d, out_a, out_b, w_out, g1, b1, kv, w_q, w_o, g2, b2, w_r, b_r)


def _rank_kernel(idx_ref, off_ref, pos_ref, next_ref, *, tm):
    @pl.when(pl.program_id(0) == 0)
    def _init():
        next_ref[...] = off_ref[...]

    idx = idx_ref[...]
    lane = lax.broadcasted_iota(jnp.int32, (tm, LANES), 1)
    hits = [lane == idx[:, k:k + 1] for k in range(TOP_K)]
    sel = jnp.zeros((tm, LANES), _F32)
    for hit in hits:
        sel = sel + jnp.where(hit, 1.0, 0.0)
    row = lax.broadcasted_iota(jnp.int32, (tm, tm), 0)
    col = lax.broadcasted_iota(jnp.int32, (tm, tm), 1)
    earlier = jnp.where(col < row, 1.0, 0.0).astype(jnp.bfloat16)
    dense = _dot(earlier, sel.astype(jnp.bfloat16)) + next_ref[...]
    pos_w = jnp.zeros((tm, LANES), _F32)
    for k, hit in enumerate(hits):
        pk = jnp.sum(jnp.where(hit, dense, 0.0), axis=1, keepdims=True)
        pos_w = jnp.where(lane == k, pk, pos_w)
    pos_ref[...] = pos_w[:, :TOP_K].astype(jnp.int32)
    next_ref[...] = next_ref[...] + jnp.sum(sel, axis=0, keepdims=True)


def _rank(idx, offsets, *, tm):
    t = idx.shape[0]
    return pl.pallas_call(
        functools.partial(_rank_kernel, tm=tm),
        out_shape=jax.ShapeDtypeStruct((t, TOP_K), jnp.int32),
        grid=(t // tm,),
        in_specs=[pl.BlockSpec((tm, TOP_K), lambda i: (i, 0)), pl.BlockSpec((1, LANES), lambda i: (0, 0))],
        out_specs=pl.BlockSpec((tm, TOP_K), lambda i: (i, 0)),
        scratch_shapes=[pltpu.VMEM((1, LANES), _F32)],
        compiler_params=_cparams(("arbitrary",)),
        name="rank",
    )(idx, offsets)


_DMA_UNROLL = 8
_DMA_THREADS = 2


def _for_each_row(n_rows, fn):
    tokens_per_group = _DMA_UNROLL // TOP_K

    def group(g, c):
        for u in range(_DMA_UNROLL):
            fn(g * _DMA_UNROLL + u, g * tokens_per_group + u // TOP_K, u % TOP_K, u)
        return c

    lax.fori_loop(0, n_rows // _DMA_UNROLL, group, 0)


def _dispatch_kernel(meta_ref, h_ref, pos_ref, xs_ref, zero_ref, sem, zsem, *, tm):
    n_rows = tm * TOP_K

    def zero_copy(r):
        return pltpu.make_async_copy(zero_ref, xs_ref.at[r], zsem)

    @pl.when(pl.program_id(0) == 0)
    def _zero_fill():
        zero_ref[...] = jnp.zeros_like(zero_ref)

        def per_expert(e, c):
            lo = meta_ref[e] + meta_ref[N_EXPERTS + e]
            hi = meta_ref[e] + meta_ref[2 * N_EXPERTS + e]
            lax.fori_loop(lo, hi, lambda r, c2: (zero_copy(r).start(), c2)[1], 0)
            lax.fori_loop(lo, hi, lambda r, c2: (zero_copy(r).wait(), c2)[1], 0)
            return c

        lax.fori_loop(0, N_EXPERTS, per_expert, 0)

    def row_copy(r, t, k, u):
        return pltpu.make_async_copy(h_ref.at[t], xs_ref.at[pos_ref[r]], sem)

    _for_each_row(n_rows, lambda r, t, k, u: row_copy(r, t, k, u).start(priority=u % _DMA_THREADS))
    _for_each_row(n_rows, lambda r, t, k, u: row_copy(r, t, k, u).wait())


def _dispatch(meta, h2t, pos_flat, n_sorted_rows, *, tm):
    t = h2t.shape[0]
    return pl.pallas_call(
        functools.partial(_dispatch_kernel, tm=tm),
        out_shape=jax.ShapeDtypeStruct((n_sorted_rows, ROW_TILE, LANES), _F32),
        grid_spec=pltpu.PrefetchScalarGridSpec(
            num_scalar_prefetch=1,
            grid=(t // tm,),
            in_specs=[pl.BlockSpec((tm, ROW_TILE, LANES), lambda i, meta: (i, 0, 0)),
                      pl.BlockSpec((tm * TOP_K,), lambda i, meta: (i,), memory_space=pltpu.SMEM)],
            out_specs=pl.BlockSpec(memory_space=pl.ANY),
            scratch_shapes=[pltpu.VMEM((ROW_TILE, LANES), _F32),
                            pltpu.SemaphoreType.DMA, pltpu.SemaphoreType.DMA]),
        compiler_params=_cparams(("arbitrary",)),
        name="dispatch",
    )(meta, h2t, pos_flat)


_DEINT = 2 * LANES


def _deinterleave_matrix():
    k = jnp.arange(_DEINT)[:, None]
    n = jnp.arange(_DEINT)[None, :]
    src = jnp.where(n < LANES, 2 * n, 2 * (n - LANES) + 1)
    return (k == src).astype(_CDT)


def _experts_kernel(te_ref, nv_ref, xs_ref, perm_ref, w1_ref, b1g_ref, b1l_ref, w2_ref, b2_ref, y_ref,
                    wg_ref, wl_ref, w2c_ref, *, tm):
    i = pl.program_id(0)
    live = i < nv_ref[0]
    new_expert = (i == 0) | (te_ref[i] != te_ref[jnp.maximum(i - 1, 0)])

    @pl.when(live & new_expert)
    def _stage_weights():
        for c in range(2 * D_FF // _DEINT):
            both = _dot(w1_ref[0, :, c * _DEINT:(c + 1) * _DEINT].astype(_CDT), perm_ref[...])
            wg_ref[:, c * LANES:(c + 1) * LANES] = both[:, :LANES].astype(_CDT)
            wl_ref[:, c * LANES:(c + 1) * LANES] = both[:, LANES:].astype(_CDT)
        w2c_ref[...] = w2_ref[0].astype(_CDT)

    @pl.when(live)
    def _mlp():
        x = _load_row_tiles(xs_ref, tm).astype(_CDT)
        glu = jnp.minimum(_dot(x, wg_ref[...]) + b1g_ref[0], SWIGLU_LIMIT)
        lin = jnp.clip(_dot(x, wl_ref[...]) + b1l_ref[0], -SWIGLU_LIMIT, SWIGLU_LIMIT)
        act = glu * jax.nn.sigmoid(SWIGLU_ALPHA * glu) * (lin + 1.0)
        _store_row_tiles(y_ref, _dot(act.astype(_CDT), w2c_ref[...]) + b2_ref[0])

    @pl.when(jnp.logical_not(live))
    def _unused_tile():
        y_ref[...] = jnp.zeros_like(y_ref)


def _experts(tile_expert, n_valid, xs, w1, b1g, b1l, w2, b2, *, tm):
    n_tiles = xs.shape[0] // (tm * ROW_TILE)
    by_expert = lambda shape: pl.BlockSpec((1,) + shape, lambda i, te, nv: (te[i], 0, 0))
    return pl.pallas_call(
        functools.partial(_experts_kernel, tm=tm),
        out_shape=jax.ShapeDtypeStruct(xs.shape, _F32),
        grid_spec=pltpu.PrefetchScalarGridSpec(
            num_scalar_prefetch=2,
            grid=(n_tiles,),
            in_specs=[pl.BlockSpec((tm * ROW_TILE, LANES), lambda i, te, nv: (jnp.minimum(i, nv[0] - 1), 0)),
                      pl.BlockSpec((_DEINT, _DEINT), lambda i, te, nv: (0, 0)),
                      by_expert((D_MODEL, 2 * D_FF)),
                      by_expert((1, D_FF)), by_expert((1, D_FF)),
                      by_expert((D_FF, D_MODEL)), by_expert((1, D_MODEL))],
            out_specs=pl.BlockSpec((tm * ROW_TILE, LANES), lambda i, te, nv: (i, 0)),
            scratch_shapes=[pltpu.VMEM((D_MODEL, D_FF), _CDT), pltpu.VMEM((D_MODEL, D_FF), _CDT),
                            pltpu.VMEM((D_FF, D_MODEL), _CDT)]),
        compiler_params=_cparams(("arbitrary",)),
        name="experts",
    )(tile_expert, n_valid, xs, _deinterleave_matrix(), w1, b1g, b1l, w2, b2)


def _combine_kernel(pos_ref, nxt_ref, wt_ref, h2t_ref, g_ref, b_ref, ys_ref, o_ref, buf_ref, sem, *, tm):
    n_rows = tm * TOP_K
    i = pl.program_id(0)
    slot = i & 1

    def row_copy(p_ref, s, r, t, k):
        tile_rows = pl.ds(pl.multiple_of(t * ROW_TILE, ROW_TILE), ROW_TILE)
        return pltpu.make_async_copy(ys_ref.at[p_ref[r]], buf_ref.at[s, k, tile_rows], sem.at[s])

    def start_gather(p_ref, s):
        _for_each_row(n_rows, lambda r, t, k, u: row_copy(p_ref, s, r, t, k).start(priority=u % _DMA_THREADS))

    @pl.when(i == 0)
    def _first():
        start_gather(pos_ref, 0)

    for r in range(n_rows):
        row_copy(pos_ref, slot, r, r // TOP_K, r % TOP_K).wait()

    for r in range(n_rows):
        row_copy(nxt_ref, 1 - slot, r, r // TOP_K, r % TOP_K).start(priority=r % _DMA_THREADS)
    wt = wt_ref[...]
    ffn = wt[:, 0:1] * _load_row_tiles(buf_ref.at[slot, 0], tm)
    for k in range(1, TOP_K):
        ffn = ffn + wt[:, k:k + 1] * _load_row_tiles(buf_ref.at[slot, k], tm)
    o_ref[...] = _layer_norm(DEEPNORM_ALPHA * _load_row_tiles(h2t_ref, tm) + ffn, g_ref[...], b_ref[...])

    @pl.when(i + 1 == pl.num_programs(0))
    def _drain():
        _for_each_row(n_rows, lambda r, t, k, u: row_copy(nxt_ref, 1 - slot, r, t, k).wait())


def _combine(pos_flat, wts, h2t, g3, b3, ys, *, tm):
    t = h2t.shape[0] // ROW_TILE
    n_steps = t // tm
    vec = pl.BlockSpec((1, D_MODEL), lambda i: (0, 0))
    return pl.pallas_call(
        functools.partial(_combine_kernel, tm=tm),
        out_shape=jax.ShapeDtypeStruct((t, D_MODEL), _F32),
        grid=(n_steps,),
        in_specs=[pl.BlockSpec((tm * TOP_K,), lambda i: (i,), memory_space=pltpu.SMEM),
                  pl.BlockSpec((tm * TOP_K,), lambda i: (jnp.minimum(i + 1, n_steps - 1),), memory_space=pltpu.SMEM),
                  pl.BlockSpec((tm, TOP_K), lambda i: (i, 0)),
                  pl.BlockSpec((tm * ROW_TILE, LANES), lambda i: (i, 0)), vec, vec,
                  pl.BlockSpec(memory_space=pl.ANY)],
        out_specs=pl.BlockSpec((tm, D_MODEL), lambda i: (i, 0)),
        scratch_shapes=[pltpu.VMEM((2, TOP_K, tm * ROW_TILE, LANES), _F32), pltpu.SemaphoreType.DMA((2,))],
        compiler_params=_cparams(("arbitrary",)),
        name="combine",
    )(pos_flat, pos_flat, wts, h2t, g3, b3, ys)


def _tiles(seq):
    big = 512 if seq % 512 == 0 else MOBA_BLOCK
    post = 1024 if seq % 1024 == 0 else big
    return dict(inproj=big, attn_q=MOBA_BLOCK, post=post, rank=big, dispatch=post, expert=512, combine=big)


def kernel(x, mem, w_in, diff_lambda_q1, diff_lambda_k1, diff_lambda_q2, diff_lambda_k2, diff_subln_g,
           w_mix_out, ln1_g, ln1_b, mem_ln_g, mem_ln_b, w_mem_q, w_mem_kv, w_mem_o, ln2_g, ln2_b, w_router,
           b_router, w_mlp1, b_mlp1, w_mlp2, b_mlp2, ln3_g, ln3_b):
    batch, seq, d = x.shape
    mem_len = mem.shape[1]
    assert d == D_MODEL and seq % MOBA_BLOCK == 0 and w_in.shape[0] == 1
    t = batch * seq
    ts = _tiles(seq)
    row = lambda v: v.reshape(1, -1).astype(_F32)

    col = jnp.arange(PROJ_WIDTH)
    is_q = (col < DIFF_WIDTH) | ((col >= 3 * DIFF_WIDTH) & (col < 3 * DIFF_WIDTH + MOBA_WIDTH))
    w_in_c = (w_in[0] * jnp.where(is_q, HEAD_DIM ** -0.5 * math.log2(math.e), 1.0)[None, :]).astype(_CDT)
    lam_vecs = jnp.stack([diff_lambda_q1[0], diff_lambda_k1[0], diff_lambda_q2[0], diff_lambda_k2[0]]).astype(_F32)
    w_r = jnp.zeros((D_MODEL, LANES), _F32).at[:, :N_EXPERTS].set(w_router[0]).astype(_CDT)
    b_r = jnp.full((1, LANES), _NEG, _F32).at[0, :N_EXPERTS].set(b_router[0])
    b1g = b_mlp1[0][:, None, 0::2].astype(_F32)
    b1l = b_mlp1[0][:, None, 1::2].astype(_F32)
    b2 = b_mlp2[0][:, None, :].astype(_F32)

    x2d = x.reshape(t, d)
    proj, kmean = _inproj(x2d, w_in_c, seq, tm=ts["inproj"])
    kmean = kmean.reshape(batch, seq // MOBA_BLOCK, MOBA_WIDTH)
    out_a = _diff_attn(proj, lam_vecs, row(diff_subln_g[0]), batch, seq, tq=ts["attn_q"])
    out_b = _moba_attn(proj, kmean, batch, seq)
    kv = _mem_kv(mem.reshape(batch * mem_len, d), row(mem_ln_g), row(mem_ln_b), w_mem_kv[0].astype(_CDT),
                 tm=mem_len)
    h2t, top_idx, top_w, counts = _post_attn(
        x2d, out_a, out_b, w_mix_out[0].astype(_CDT), row(ln1_g[0]), row(ln1_b[0]), kv,
        w_mem_q[0].astype(_CDT), w_mem_o[0].astype(_CDT), row(ln2_g[0]), row(ln2_b[0]), w_r, b_r,
        seq, mem_len, tm=ts["post"])

    tmx = ts["expert"]
    cnt = counts[0, :N_EXPERTS]
    padded = (cnt + tmx - 1) // tmx * tmx
    ends = jnp.cumsum(padded)
    offsets = ends - padded
    meta = jnp.concatenate([offsets, cnt, padded]).astype(jnp.int32)
    n_sorted_rows = t * TOP_K + N_EXPERTS * tmx
    n_tiles = n_sorted_rows // tmx
    tile_start = jnp.arange(n_tiles, dtype=jnp.int32) * tmx
    tile_expert = jnp.minimum(jnp.sum(tile_start[:, None] >= ends[None, :], axis=1), N_EXPERTS - 1).astype(jnp.int32)
    n_valid = (ends[-1:] // tmx).astype(jnp.int32)
    offsets_row = jnp.zeros((1, LANES), _F32).at[0, :N_EXPERTS].set(offsets.astype(_F32))
    pos = _rank(top_idx, offsets_row, tm=ts["rank"]).reshape(-1)

    xs = _dispatch(meta, h2t.reshape(t, ROW_TILE, LANES), pos, n_sorted_rows, tm=ts["dispatch"])
    ys = _experts(tile_expert, n_valid, xs.reshape(n_sorted_rows * ROW_TILE, LANES), w_mlp1[0], b1g, b1l,
                  w_mlp2[0], b2, tm=tmx)
    out = _combine(pos, top_w, h2t, row(ln3_g[0]), row(ln3_b[0]),
                   ys.reshape(n_sorted_rows, ROW_TILE, LANES), tm=ts["combine"])
    return out.reshape(batch, seq, d)
```

```python
import functools
import math

import jax
import jax.numpy as jnp
from jax import lax
from jax.experimental import pallas as pl
from jax.experimental.pallas import tpu as pltpu

D_MODEL = 1024
DIFF_HEADS = 4
HEAD_DIM = 64
DIFF_WIDTH = DIFF_HEADS * 2 * HEAD_DIM
MOBA_HEADS = 8
MOBA_WIDTH = MOBA_HEADS * HEAD_DIM
MOBA_BLOCK = 256
MOBA_TOPK = 3
ROPE_THETA = 500000.0
ROT_DIM = HEAD_DIM // 4
MEM_HEADS = 4
MEM_HEAD_DIM = D_MODEL // MEM_HEADS
N_EXPERTS = 32
TOP_K = 4
D_FF = D_MODEL
SWIGLU_ALPHA = 1.702
SWIGLU_LIMIT = 7.0
LN_EPS = 1e-5
RMS_EPS = 1e-5
DEEPNORM_ALPHA = 2.0 ** 0.25
LAMBDA_INIT = 0.8 - 0.6 * math.exp(0.0)

LANES = 128
ROW_TILE = 8
V7X_VMEM_LIMIT = 56 * 1024 * 1024

_CDT = jnp.bfloat16
_F32 = jnp.float32
_NEG = -1e30

_CB_DQ, _CB_DK, _CB_DV = 0, 4, 8
_CB_MQ, _CB_MK, _CB_MV = 12, 16, 20
PROJ_WIDTH = 3 * (DIFF_WIDTH + MOBA_WIDTH)


def _cparams(sem, vmem=V7X_VMEM_LIMIT):
    return pltpu.CompilerParams(dimension_semantics=sem, vmem_limit_bytes=vmem)


def _dot(a, b):
    return jnp.dot(a, b, preferred_element_type=_F32)


def _dot_nt(a, b):
    return lax.dot_general(a, b, (((1,), (1,)), ((), ())), preferred_element_type=_F32)


def _layer_norm(x, g, b):
    mu = jnp.mean(x, axis=-1, keepdims=True)
    xc = x - mu
    var = jnp.mean(xc * xc, axis=-1, keepdims=True)
    return xc * lax.rsqrt(var + LN_EPS) * g + b


_ROPE_BLOCKS = tuple(range(_CB_DQ, _CB_DV)) + tuple(range(_CB_MQ, _CB_MV))


def _inproj_kernel(x_ref, w_ref, cos_ref, sa_ref, sb_ref, o_ref, km_ref, *, tm):
    x = x_ref[...].astype(_CDT)
    cos, sa, sb = cos_ref[...], sa_ref[...], sb_ref[...]
    seg_w = 4 * LANES
    for seg in range(PROJ_WIDTH // seg_w):
        pseg = _dot(x, w_ref[:, seg * seg_w:(seg + 1) * seg_w])
        for c in range(4):
            cb = seg * 4 + c
            cols = slice(cb * LANES, (cb + 1) * LANES)
            p = pseg[:, c * LANES:(c + 1) * LANES]
            if cb in _ROPE_BLOCKS:
                p = p * cos + pltpu.roll(p, 8, 1) * sa + pltpu.roll(p, LANES - 8, 1) * sb
            o_ref[:, cols] = p.astype(o_ref.dtype)
            if _CB_MK <= cb < _CB_MV:
                kc = slice((cb - _CB_MK) * LANES, (cb - _CB_MK + 1) * LANES)
                for blk in range(tm // MOBA_BLOCK):
                    rows = p[blk * MOBA_BLOCK:(blk + 1) * MOBA_BLOCK, :]
                    km_ref[blk, :, kc] = jnp.sum(rows, axis=0, keepdims=True) * (1.0 / MOBA_BLOCK)


def _rope_tables(seq):
    half = ROT_DIM // 2
    inv_freq = ROPE_THETA ** (-jnp.arange(0, ROT_DIM, 2, dtype=_F32) / ROT_DIM)
    ang = jnp.arange(seq, dtype=_F32)[:, None] * inv_freq[None, :]
    cos, sin = jnp.cos(ang), jnp.sin(ang)
    lane = jnp.arange(LANES) % HEAD_DIM
    first, second = lane < half, (lane >= half) & (lane < ROT_DIM)
    idx = jnp.where(first, lane, jnp.where(second, lane - half, 0))
    cos_t = jnp.where((first | second)[None, :], cos[:, idx], 1.0)
    sa_t = jnp.where(second[None, :], sin[:, idx], 0.0)
    sb_t = jnp.where(first[None, :], -sin[:, idx], 0.0)
    return cos_t, sa_t, sb_t


def _inproj(x2d, w_in, seq, *, tm):
    t = x2d.shape[0]
    cos_t, sa_t, sb_t = _rope_tables(seq)
    n_pos = seq // tm
    tab_spec = pl.BlockSpec((tm, LANES), lambda i: (i % n_pos, 0))
    return pl.pallas_call(
        functools.partial(_inproj_kernel, tm=tm),
        out_shape=(jax.ShapeDtypeStruct((t, PROJ_WIDTH), _CDT),
                   jax.ShapeDtypeStruct((t // MOBA_BLOCK, 1, MOBA_WIDTH), _F32)),
        grid=(t // tm,),
        in_specs=[pl.BlockSpec((tm, D_MODEL), lambda i: (i, 0)),
                  pl.BlockSpec((D_MODEL, PROJ_WIDTH), lambda i: (0, 0)),
                  tab_spec, tab_spec, tab_spec],
        out_specs=(pl.BlockSpec((tm, PROJ_WIDTH), lambda i: (i, 0)),
                   pl.BlockSpec((tm // MOBA_BLOCK, 1, MOBA_WIDTH), lambda i: (i, 0, 0))),
        compiler_params=_cparams(("parallel",)),
        name="inproj",
    )(x2d, w_in, cos_t, sa_t, sb_t)


def _exp2_parts(parts):
    chunks = [s[:, c:c + LANES] for s in parts for c in range(0, s.shape[1], LANES)]
    m = jnp.max(functools.reduce(jnp.maximum, chunks), axis=1, keepdims=True)
    return [jnp.exp2(s - m).astype(_CDT) for s in parts]


_SCORES_AHEAD = 1


def _causal_mask(tq):
    row = lax.broadcasted_iota(jnp.int32, (tq, tq), 0)
    col = lax.broadcasted_iota(jnp.int32, (tq, tq), 1)
    return col <= row


def _diff_attn_kernel(q_ref, k_ref, v_ref, lam_ref, g_ref, o_ref, *, seq, tq):
    lane = lax.broadcasted_iota(jnp.int32, (tq, LANES), 1)
    mask = _causal_mask(tq)
    lv = lam_ref[...]
    lam = (jnp.exp(jnp.sum(lv[0:1] * lv[1:2], axis=1, keepdims=True))
           - jnp.exp(jnp.sum(lv[2:3] * lv[3:4], axis=1, keepdims=True)) + LAMBDA_INIT)
    gain = g_ref[...] * (1.0 - LAMBDA_INIT)

    def with_ones(v):
        return jnp.concatenate([v, jnp.ones_like(v)], axis=1)

    def scores(i):
        rows = slice(i * tq, (i + 1) * tq)
        q = q_ref[rows, :]
        zero = jnp.zeros_like(q)
        q1 = jnp.where(lane < HEAD_DIM, q, zero)
        q2 = jnp.where(lane >= HEAD_DIM, q, zero)
        kd = k_ref[rows, :]
        s1 = [jnp.where(mask, _dot_nt(q1, kd), _NEG)]
        s2 = [jnp.where(mask, _dot_nt(q2, kd), _NEG)]
        if i:
            kp = k_ref[:i * tq, :]
            s1.append(_dot_nt(q1, kp))
            s2.append(_dot_nt(q2, kp))
        return s1, s2

    def finish(i, s1, s2):
        rows = slice(i * tq, (i + 1) * tq)
        v_all = with_ones(v_ref[:(i + 1) * tq, :])
        nums = []
        for parts in (s1, s2):
            es = _exp2_parts(parts)
            nl = _dot(jnp.concatenate(es[1:] + es[:1], axis=1), v_all)
            nums.append(nl[:, :LANES] / nl[:, LANES:LANES + 1])
        o = nums[0] - lam * nums[1]
        o = o * lax.rsqrt(jnp.mean(o * o, axis=1, keepdims=True) + RMS_EPS) * gain
        o_ref[rows, :] = o.astype(o_ref.dtype)

    n_tiles = seq // tq
    pending = [scores(i) for i in range(min(_SCORES_AHEAD, n_tiles))]
    for i in range(n_tiles):
        if i + _SCORES_AHEAD < n_tiles:
            pending.append(scores(i + _SCORES_AHEAD))
        finish(i, *pending.pop(0))


def _diff_attn(proj, lam_vecs, subln_g, batch, seq, *, tq):
    t = batch * seq
    blk = lambda cb: pl.BlockSpec((seq, LANES), lambda b, h: (b, cb + h))
    return pl.pallas_call(
        functools.partial(_diff_attn_kernel, seq=seq, tq=tq),
        out_shape=jax.ShapeDtypeStruct((t, DIFF_WIDTH), _CDT),
        grid=(batch, DIFF_HEADS),
        in_specs=[blk(_CB_DQ), blk(_CB_DK), blk(_CB_DV),
                  pl.BlockSpec((4, HEAD_DIM), lambda b, h: (0, 0)),
                  pl.BlockSpec((1, 2 * HEAD_DIM), lambda b, h: (0, 0))],
        out_specs=pl.BlockSpec((seq, LANES), lambda b, h: (b, h)),
        compiler_params=_cparams(("parallel", "parallel")),
        name="diff_attn",
    )(proj, proj, proj, lam_vecs, subln_g)


def _moba_selection(gate, n_past):
    blk = lax.broadcasted_iota(jnp.int32, gate.shape, 0)
    rank = jnp.zeros(gate.shape, _F32)
    for m_blk in range(n_past):
        gm = gate[m_blk:m_blk + 1, :]
        beats = (gm > gate) | ((gm == gate) & (m_blk < blk))
        rank = rank + jnp.where(beats, 1.0, 0.0)
    return jnp.where((blk < n_past) & (rank < MOBA_TOPK), 1.0, 0.0)


def _moba_kernel(q_ref, k_ref, v_ref, km_ref, o_ref, *, nb):
    tq = MOBA_BLOCK
    lane = lax.broadcasted_iota(jnp.int32, (tq, LANES), 1)
    mask = _causal_mask(tq)
    km = km_ref[...].astype(_CDT)
    in_head = [(lane >= hh * HEAD_DIM) & (lane < (hh + 1) * HEAD_DIM) for hh in range(2)]
    seq_lane = lax.broadcasted_iota(jnp.int32, (nb * tq, LANES), 1)
    v_ones = [jnp.where((seq_lane >= hh * HEAD_DIM) & (seq_lane < (hh + 1) * HEAD_DIM), v_ref[...],
                        jnp.ones((nb * tq, LANES), _CDT)) for hh in range(2)]

    def scores(i, hh):
        rows = slice(i * tq, (i + 1) * tq)
        q = q_ref[rows, :]
        qh = jnp.where(in_head[hh], q, jnp.zeros_like(q))
        parts = [jnp.where(mask, _dot_nt(qh, k_ref[rows, :]), _NEG)]
        gated = i > MOBA_TOPK
        if gated:
            sel = _moba_selection(_dot_nt(km, qh), i)
            sel_q = jnp.concatenate([sel, jnp.zeros((LANES - nb, tq), _F32)], axis=0).T
        for j in range(i):
            s = _dot_nt(qh, k_ref[j * tq:(j + 1) * tq, :])
            parts.append(jnp.where(sel_q[:, j:j + 1] > 0.5, s, _NEG) if gated else s)
        return parts

    def finish(i, hh, parts):
        es = _exp2_parts(parts)
        acc = _dot(jnp.concatenate(es[1:] + es[:1], axis=1), v_ones[hh][:(i + 1) * tq, :])
        sum_lane = (1 - hh) * HEAD_DIM
        return acc / acc[:, sum_lane:sum_lane + 1]

    units = [(i, hh) for i in range(nb) for hh in range(2)]
    pending = [scores(*u) for u in units[:_SCORES_AHEAD]]
    outs = {}
    for n, (i, hh) in enumerate(units):
        if n + _SCORES_AHEAD < len(units):
            pending.append(scores(*units[n + _SCORES_AHEAD]))
        outs[hh] = finish(i, hh, pending.pop(0))
        if hh == 1:
            o_ref[i * tq:(i + 1) * tq, :] = jnp.where(in_head[0], outs[0], outs[1]).astype(o_ref.dtype)


def _moba_attn(proj, kmean, batch, seq):
    nb = seq // MOBA_BLOCK
    t = batch * seq
    blk = lambda cb: pl.BlockSpec((seq, LANES), lambda b, p: (b, cb + p))
    return pl.pallas_call(
        functools.partial(_moba_kernel, nb=nb),
        out_shape=jax.ShapeDtypeStruct((t, MOBA_WIDTH), _CDT),
        grid=(batch, MOBA_HEADS // 2),
        in_specs=[blk(_CB_MQ), blk(_CB_MK), blk(_CB_MV),
                  pl.BlockSpec((None, nb, LANES), lambda b, p: (b, 0, p))],
        out_specs=pl.BlockSpec((seq, LANES), lambda b, p: (b, p)),
        compiler_params=_cparams(("parallel", "parallel")),
        name="moba_attn",
    )(proj, proj, proj, kmean)


def _mem_kv_kernel(mem_ref, g_ref, b_ref, w_ref, o_ref):
    mn = _layer_norm(mem_ref[...], g_ref[...], b_ref[...])
    o_ref[...] = _dot(mn.astype(_CDT), w_ref[...]).astype(o_ref.dtype)


def _mem_kv(mem2d, g, b, w_kv, *, tm):
    rows = mem2d.shape[0]
    vec = pl.BlockSpec((1, D_MODEL), lambda i: (0, 0))
    return pl.pallas_call(
        _mem_kv_kernel,
        out_shape=jax.ShapeDtypeStruct((rows, 2 * D_MODEL), _CDT),
        grid=(rows // tm,),
        in_specs=[pl.BlockSpec((tm, D_MODEL), lambda i: (i, 0)), vec, vec,
                  pl.BlockSpec((D_MODEL, 2 * D_MODEL), lambda i: (0, 0))],
        out_specs=pl.BlockSpec((tm, 2 * D_MODEL), lambda i: (i, 0)),
        compiler_params=_cparams(("parallel",)),
        name="mem_kv",
    )(mem2d, g, b, w_kv)


def _store_row_tiles(ref, val):
    n = val.shape[0]
    for c in range(ROW_TILE):
        ref[pl.ds(c, n, stride=ROW_TILE), :] = val[:, c * LANES:(c + 1) * LANES]


def _load_row_tiles(ref, n):
    return jnp.concatenate([ref[pl.ds(c, n, stride=ROW_TILE), :] for c in range(ROW_TILE)], axis=1)


_DONE = object()


def _interleave(stage_generators):
    live = list(stage_generators)
    while live:
        live = [g for g in live if next(g, _DONE) is not _DONE]


def _post_attn_kernel(x_ref, oa_ref, ob_ref, wout_ref, g1_ref, b1_ref, kv_ref, wq_ref, wo_ref,
                      g2_ref, b2_ref, wr_ref, br_ref, h2t_ref, idx_ref, wt_ref, cnt_ref, cnt_acc, *, tm, groups):
    @pl.when(pl.program_id(0) == 0)
    def _init_counts():
        cnt_acc[...] = jnp.zeros_like(cnt_acc)

    n = tm // groups
    lane = lax.broadcasted_iota(jnp.int32, (n, LANES), 1)
    lane_f = lane.astype(_F32)

    def row_group(g):
        rows = slice(g * n, (g + 1) * n)
        mix = _dot(oa_ref[rows, :], wout_ref[:DIFF_WIDTH, :]) + _dot(ob_ref[rows, :], wout_ref[DIFF_WIDTH:, :])
        yield
        h1 = _layer_norm(DEEPNORM_ALPHA * x_ref[rows, :] + mix, g1_ref[...], b1_ref[...])
        q = (_dot(h1.astype(_CDT), wq_ref[...]) * (MEM_HEAD_DIM ** -0.5)).astype(_CDT)
        yield
        heads = []
        for h in range(MEM_HEADS):
            c0 = h * MEM_HEAD_DIM
            kh = kv_ref[:, c0:c0 + MEM_HEAD_DIM]
            vh = kv_ref[:, D_MODEL + c0:D_MODEL + c0 + MEM_HEAD_DIM]
            s = _dot_nt(q[:, c0:c0 + MEM_HEAD_DIM], kh)
            p = jnp.exp(s - jnp.max(s, axis=1, keepdims=True))
            o = _dot(p.astype(_CDT), vh) / jnp.sum(p, axis=1, keepdims=True)
            heads.append(o.astype(_CDT))
        yield
        xatt = _dot(jnp.concatenate(heads, axis=1), wo_ref[...])
        yield
        h2 = _layer_norm(DEEPNORM_ALPHA * h1 + xatt, g2_ref[...], b2_ref[...])
        _store_row_tiles(h2t_ref.at[pl.ds(g * n * ROW_TILE, n * ROW_TILE)], h2)
        logits = _dot(h2.astype(_CDT), wr_ref[...]) + br_ref[...]
        yield
        vals, idxs = [], []
        work = logits
        for _ in range(TOP_K):
            m = jnp.max(work, axis=1, keepdims=True)
            ik = jnp.min(jnp.where(work == m, lane_f, float(LANES)), axis=1, keepdims=True)
            vals.append(m)
            idxs.append(ik)
            work = jnp.where(lane_f == ik, -jnp.inf, work)
        exps = [jnp.exp(v - vals[0]) for v in vals]
        denom = exps[0] + exps[1] + exps[2] + exps[3]
        idx_w = jnp.zeros((n, LANES), _F32)
        wt_w = jnp.zeros((n, LANES), _F32)
        for k in range(TOP_K):
            idx_w = jnp.where(lane == k, idxs[k], idx_w)
            wt_w = jnp.where(lane == k, exps[k] / denom, wt_w)
        idx_ref[rows, :] = idx_w[:, :TOP_K].astype(jnp.int32)
        wt_ref[rows, :] = wt_w[:, :TOP_K]
        chosen = functools.reduce(jnp.add, [jnp.where(lane_f == ik, 1.0, 0.0) for ik in idxs])
        cnt_acc[...] = cnt_acc[...] + jnp.sum(chosen, axis=0, keepdims=True)

    _interleave(row_group(g) for g in range(groups))
    cnt_ref[...] = cnt_acc[...].astype(jnp.int32)


def _post_attn(x2d, out_a, out_b, w_out, g1, b1, kv, w_q, w_o, g2, b2, w_r, b_r, seq, mem_len, *, tm):
    t = x2d.shape[0]
    per_b = seq // tm
    full = lambda shape: pl.BlockSpec(shape, lambda i: (0,) * len(shape))
    vec = full((1, D_MODEL))
    return pl.pallas_call(
        functools.partial(_post_attn_kernel, tm=tm, groups=max(1, tm // MOBA_BLOCK)),
        out_shape=(jax.ShapeDtypeStruct((t * ROW_TILE, LANES), _F32),
                   jax.ShapeDtypeStruct((t, TOP_K), jnp.int32),
                   jax.ShapeDtypeStruct((t, TOP_K), _F32),
                   jax.ShapeDtypeStruct((1, LANES), jnp.int32)),
        grid=(t // tm,),
        in_specs=[pl.BlockSpec((tm, D_MODEL), lambda i: (i, 0)),
                  pl.BlockSpec((tm, DIFF_WIDTH), lambda i: (i, 0)),
                  pl.BlockSpec((tm, MOBA_WIDTH), lambda i: (i, 0)),
                  full((D_MODEL, D_MODEL)), vec, vec,
                  pl.BlockSpec((mem_len, 2 * D_MODEL), lambda i: (i // per_b, 0)),
                  full((D_MODEL, D_MODEL)), full((D_MODEL, D_MODEL)), vec, vec,
                  full((D_MODEL, LANES)), full((1, LANES))],
        out_specs=(pl.BlockSpec((tm * ROW_TILE, LANES), lambda i: (i, 0)),
                   pl.BlockSpec((tm, TOP_K), lambda i: (i, 0)),
                   pl.BlockSpec((tm, TOP_K), lambda i: (i, 0)),
                   pl.BlockSpec((1, LANES), lambda i: (0, 0))),
        scratch_shapes=[pltpu.VMEM((1, LANES), _F32)],
        compiler_params=_cparams(("arbitrary",)),
        name="post_attn",
    )(x2d, out_a, out_b, w_out, g1, b1, kv, w_q, w_o, g2, b2, w_r, b_r)


def _rank_kernel(idx_ref, off_ref, pos_ref, next_ref, *, tm):
    @pl.when(pl.program_id(0) == 0)
    def _init():
        next_ref[...] = off_ref[...]

    idx = idx_ref[...]
    lane = lax.broadcasted_iota(jnp.int32, (tm, LANES), 1)
    hits = [lane == idx[:, k:k + 1] for k in range(TOP_K)]
    sel = jnp.zeros((tm, LANES), _F32)
    for hit in hits:
        sel = sel + jnp.where(hit, 1.0, 0.0)
    row = lax.broadcasted_iota(jnp.int32, (tm, tm), 0)
    col = lax.broadcasted_iota(jnp.int32, (tm, tm), 1)
    earlier = jnp.where(col < row, 1.0, 0.0).astype(jnp.bfloat16)
    dense = _dot(earlier, sel.astype(jnp.bfloat16)) + next_ref[...]
    pos_w = jnp.zeros((tm, LANES), _F32)
    for k, hit in enumerate(hits):
        pk = jnp.sum(jnp.where(hit, dense, 0.0), axis=1, keepdims=True)
        pos_w = jnp.where(lane == k, pk, pos_w)
    pos_ref[...] = pos_w[:, :TOP_K].astype(jnp.int32)
    next_ref[...] = next_ref[...] + jnp.sum(sel, axis=0, keepdims=True)


def _rank(idx, offsets, *, tm):
    t = idx.shape[0]
    return pl.pallas_call(
        functools.partial(_rank_kernel, tm=tm),
        out_shape=jax.ShapeDtypeStruct((t, TOP_K), jnp.int32),
        grid=(t // tm,),
        in_specs=[pl.BlockSpec((tm, TOP_K), lambda i: (i, 0)), pl.BlockSpec((1, LANES), lambda i: (0, 0))],
        out_specs=pl.BlockSpec((tm, TOP_K), lambda i: (i, 0)),
        scratch_shapes=[pltpu.VMEM((1, LANES), _F32)],
        compiler_params=_cparams(("arbitrary",)),
        name="rank",
    )(idx, offsets)


_DMA_UNROLL = 8


def _for_each_row(n_rows, fn):
    tokens_per_group = _DMA_UNROLL // TOP_K

    def group(g, c):
        for u in range(_DMA_UNROLL):
            fn(g * _DMA_UNROLL + u, g * tokens_per_group + u // TOP_K, u % TOP_K)
        return c

    lax.fori_loop(0, n_rows // _DMA_UNROLL, group, 0)


def _dispatch_kernel(meta_ref, h_ref, pos_ref, xs_ref, zero_ref, sem, zsem, *, tm):
    n_rows = tm * TOP_K

    def zero_copy(r):
        return pltpu.make_async_copy(zero_ref, xs_ref.at[r], zsem)

    @pl.when(pl.program_id(0) == 0)
    def _zero_fill():
        zero_ref[...] = jnp.zeros_like(zero_ref)

        def per_expert(e, c):
            lo = meta_ref[e] + meta_ref[N_EXPERTS + e]
            hi = meta_ref[e] + meta_ref[2 * N_EXPERTS + e]
            lax.fori_loop(lo, hi, lambda r, c2: (zero_copy(r).start(), c2)[1], 0)
            lax.fori_loop(lo, hi, lambda r, c2: (zero_copy(r).wait(), c2)[1], 0)
            return c

        lax.fori_loop(0, N_EXPERTS, per_expert, 0)

    def row_copy(r, t, k):
        return pltpu.make_async_copy(h_ref.at[t], xs_ref.at[pos_ref[r]], sem)

    _for_each_row(n_rows, lambda r, t, k: row_copy(r, t, k).start())
    _for_each_row(n_rows, lambda r, t, k: row_copy(r, t, k).wait())


def _dispatch(meta, h2t, pos_flat, n_sorted_rows, *, tm):
    t = h2t.shape[0]
    return pl.pallas_call(
        functools.partial(_dispatch_kernel, tm=tm),
        out_shape=jax.ShapeDtypeStruct((n_sorted_rows, ROW_TILE, LANES), _F32),
        grid_spec=pltpu.PrefetchScalarGridSpec(
            num_scalar_prefetch=1,
            grid=(t // tm,),
            in_specs=[pl.BlockSpec((tm, ROW_TILE, LANES), lambda i, meta: (i, 0, 0)),
                      pl.BlockSpec((tm * TOP_K,), lambda i, meta: (i,), memory_space=pltpu.SMEM)],
            out_specs=pl.BlockSpec(memory_space=pl.ANY),
            scratch_shapes=[pltpu.VMEM((ROW_TILE, LANES), _F32),
                            pltpu.SemaphoreType.DMA, pltpu.SemaphoreType.DMA]),
        compiler_params=_cparams(("arbitrary",)),
        name="dispatch",
    )(meta, h2t, pos_flat)


_DEINT = 2 * LANES


def _deinterleave_matrix():
    k = jnp.arange(_DEINT)[:, None]
    n = jnp.arange(_DEINT)[None, :]
    src = jnp.where(n < LANES, 2 * n, 2 * (n - LANES) + 1)
    return (k == src).astype(_CDT)


def _experts_kernel(te_ref, nv_ref, xs_ref, perm_ref, w1_ref, b1g_ref, b1l_ref, w2_ref, b2_ref, y_ref,
                    wg_ref, wl_ref, w2c_ref, *, tm):
    i = pl.program_id(0)
    live = i < nv_ref[0]
    new_expert = (i == 0) | (te_ref[i] != te_ref[jnp.maximum(i - 1, 0)])

    @pl.when(live & new_expert)
    def _stage_weights():
        for c in range(2 * D_FF // _DEINT):
            both = _dot(w1_ref[0, :, c * _DEINT:(c + 1) * _DEINT].astype(_CDT), perm_ref[...])
            wg_ref[:, c * LANES:(c + 1) * LANES] = both[:, :LANES].astype(_CDT)
            wl_ref[:, c * LANES:(c + 1) * LANES] = both[:, LANES:].astype(_CDT)
        w2c_ref[...] = w2_ref[0].astype(_CDT)

    @pl.when(live)
    def _mlp():
        x = _load_row_tiles(xs_ref, tm).astype(_CDT)
        glu = jnp.minimum(_dot(x, wg_ref[...]) + b1g_ref[0], SWIGLU_LIMIT)
        lin = jnp.clip(_dot(x, wl_ref[...]) + b1l_ref[0], -SWIGLU_LIMIT, SWIGLU_LIMIT)
        act = glu * jax.nn.sigmoid(SWIGLU_ALPHA * glu) * (lin + 1.0)
        _store_row_tiles(y_ref, _dot(act.astype(_CDT), w2c_ref[...]) + b2_ref[0])

    @pl.when(jnp.logical_not(live))
    def _unused_tile():
        y_ref[...] = jnp.zeros_like(y_ref)


def _experts(tile_expert, n_valid, xs, w1, b1g, b1l, w2, b2, *, tm):
    n_tiles = xs.shape[0] // (tm * ROW_TILE)
    by_expert = lambda shape: pl.BlockSpec((1,) + shape, lambda i, te, nv: (te[i], 0, 0))
    return pl.pallas_call(
        functools.partial(_experts_kernel, tm=tm),
        out_shape=jax.ShapeDtypeStruct(xs.shape, _F32),
        grid_spec=pltpu.PrefetchScalarGridSpec(
            num_scalar_prefetch=2,
            grid=(n_tiles,),
            in_specs=[pl.BlockSpec((tm * ROW_TILE, LANES), lambda i, te, nv: (jnp.minimum(i, nv[0] - 1), 0)),
                      pl.BlockSpec((_DEINT, _DEINT), lambda i, te, nv: (0, 0)),
                      by_expert((D_MODEL, 2 * D_FF)),
                      by_expert((1, D_FF)), by_expert((1, D_FF)),
                      by_expert((D_FF, D_MODEL)), by_expert((1, D_MODEL))],
            out_specs=pl.BlockSpec((tm * ROW_TILE, LANES), lambda i, te, nv: (i, 0)),
            scratch_shapes=[pltpu.VMEM((D_MODEL, D_FF), _CDT), pltpu.VMEM((D_MODEL, D_FF), _CDT),
                            pltpu.VMEM((D_FF, D_MODEL), _CDT)]),
        compiler_params=_cparams(("arbitrary",)),
        name="experts",
    )(tile_expert, n_valid, xs, _deinterleave_matrix(), w1, b1g, b1l, w2, b2)


def _combine_kernel(pos_ref, nxt_ref, wt_ref, h2t_ref, g_ref, b_ref, ys_ref, o_ref, buf_ref, sem, *, tm):
    n_rows = tm * TOP_K
    i = pl.program_id(0)
    slot = i & 1

    def row_copy(p_ref, s, r, t, k):
        tile_rows = pl.ds(pl.multiple_of(t * ROW_TILE, ROW_TILE), ROW_TILE)
        return pltpu.make_async_copy(ys_ref.at[p_ref[r]], buf_ref.at[s, k, tile_rows], sem.at[s])

    @pl.when(i == 0)
    def _first():
        _for_each_row(n_rows, lambda r, t, k: row_copy(pos_ref, 0, r, t, k).start())

    for r in range(n_rows):
        row_copy(pos_ref, slot, r, r // TOP_K, r % TOP_K).wait()
    for r in range(n_rows):
        row_copy(nxt_ref, 1 - slot, r, r // TOP_K, r % TOP_K).start()
    wt = wt_ref[...]
    ffn = wt[:, 0:1] * _load_row_tiles(buf_ref.at[slot, 0], tm)
    for k in range(1, TOP_K):
        ffn = ffn + wt[:, k:k + 1] * _load_row_tiles(buf_ref.at[slot, k], tm)
    o_ref[...] = _layer_norm(DEEPNORM_ALPHA * _load_row_tiles(h2t_ref, tm) + ffn, g_ref[...], b_ref[...])

    @pl.when(i + 1 == pl.num_programs(0))
    def _drain():
        _for_each_row(n_rows, lambda r, t, k: row_copy(nxt_ref, 1 - slot, r, t, k).wait())


def _combine(pos_flat, wts, h2t, g3, b3, ys, *, tm):
    t = h2t.shape[0] // ROW_TILE
    n_steps = t // tm
    vec = pl.BlockSpec((1, D_MODEL), lambda i: (0, 0))
    return pl.pallas_call(
        functools.partial(_combine_kernel, tm=tm),
        out_shape=jax.ShapeDtypeStruct((t, D_MODEL), _F32),
        grid=(n_steps,),
        in_specs=[pl.BlockSpec((tm * TOP_K,), lambda i: (i,), memory_space=pltpu.SMEM),
                  pl.BlockSpec((tm * TOP_K,), lambda i: (jnp.minimum(i + 1, n_steps - 1),), memory_space=pltpu.SMEM),
                  pl.BlockSpec((tm, TOP_K), lambda i: (i, 0)),
                  pl.BlockSpec((tm * ROW_TILE, LANES), lambda i: (i, 0)), vec, vec,
                  pl.BlockSpec(memory_space=pl.ANY)],
        out_specs=pl.BlockSpec((tm, D_MODEL), lambda i: (i, 0)),
        scratch_shapes=[pltpu.VMEM((2, TOP_K, tm * ROW_TILE, LANES), _F32), pltpu.SemaphoreType.DMA((2,))],
        compiler_params=_cparams(("arbitrary",)),
        name="combine",
    )(pos_flat, pos_flat, wts, h2t, g3, b3, ys)


def _tiles(seq):
    big = 512 if seq % 512 == 0 else MOBA_BLOCK
    post = 1024 if seq % 1024 == 0 else big
    return dict(inproj=big, attn_q=MOBA_BLOCK, post=post, rank=big, dispatch=big, expert=512, combine=big)


def kernel(x, mem, w_in, diff_lambda_q1, diff_lambda_k1, diff_lambda_q2, diff_lambda_k2, diff_subln_g,
           w_mix_out, ln1_g, ln1_b, mem_ln_g, mem_ln_b, w_mem_q, w_mem_kv, w_mem_o, ln2_g, ln2_b, w_router,
           b_router, w_mlp1, b_mlp1, w_mlp2, b_mlp2, ln3_g, ln3_b):
    batch, seq, d = x.shape
    mem_len = mem.shape[1]
    assert d == D_MODEL and seq % MOBA_BLOCK == 0 and w_in.shape[0] == 1
    t = batch * seq
    ts = _tiles(seq)
    row = lambda v: v.reshape(1, -1).astype(_F32)

    col = jnp.arange(PROJ_WIDTH)
    is_q = (col < DIFF_WIDTH) | ((col >= 3 * DIFF_WIDTH) & (col < 3 * DIFF_WIDTH + MOBA_WIDTH))
    w_in_c = (w_in[0] * jnp.where(is_q, HEAD_DIM ** -0.5 * math.log2(math.e), 1.0)[None, :]).astype(_CDT)
    lam_vecs = jnp.stack([diff_lambda_q1[0], diff_lambda_k1[0], diff_lambda_q2[0], diff_lambda_k2[0]]).astype(_F32)
    w_r = jnp.zeros((D_MODEL, LANES), _F32).at[:, :N_EXPERTS].set(w_router[0]).astype(_CDT)
    b_r = jnp.full((1, LANES), _NEG, _F32).at[0, :N_EXPERTS].set(b_router[0])
    b1g = b_mlp1[0][:, None, 0::2].astype(_F32)
    b1l = b_mlp1[0][:, None, 1::2].astype(_F32)
    b2 = b_mlp2[0][:, None, :].astype(_F32)

    x2d = x.reshape(t, d)
    proj, kmean = _inproj(x2d, w_in_c, seq, tm=ts["inproj"])
    kmean = kmean.reshape(batch, seq // MOBA_BLOCK, MOBA_WIDTH)
    out_a = _diff_attn(proj, lam_vecs, row(diff_subln_g[0]), batch, seq, tq=ts["attn_q"])
    out_b = _moba_attn(proj, kmean, batch, seq)
    kv = _mem_kv(mem.reshape(batch * mem_len, d), row(mem_ln_g), row(mem_ln_b), w_mem_kv[0].astype(_CDT),
                 tm=mem_len)
    h2t, top_idx, top_w, counts = _post_attn(
        x2d, out_a, out_b, w_mix_out[0].astype(_CDT), row(ln1_g[0]), row(ln1_b[0]), kv,
        w_mem_q[0].astype(_CDT), w_mem_o[0].astype(_CDT), row(ln2_g[0]), row(ln2_b[0]), w_r, b_r,
        seq, mem_len, tm=ts["post"])

    tmx = ts["expert"]
    cnt = counts[0, :N_EXPERTS]
    padded = (cnt + tmx - 1) // tmx * tmx
    ends = jnp.cumsum(padded)
    offsets = ends - padded
    meta = jnp.concatenate([offsets, cnt, padded]).astype(jnp.int32)
    n_sorted_rows = t * TOP_K + N_EXPERTS * tmx
    n_tiles = n_sorted_rows // tmx
    tile_start = jnp.arange(n_tiles, dtype=jnp.int32) * tmx
    tile_expert = jnp.minimum(jnp.sum(tile_start[:, None] >= ends[None, :], axis=1), N_EXPERTS - 1).astype(jnp.int32)
    n_valid = (ends[-1:] // tmx).astype(jnp.int32)
    offsets_row = jnp.zeros((1, LANES), _F32).at[0, :N_EXPERTS].set(offsets.astype(_F32))
    pos = _rank(top_idx, offsets_row, tm=ts["rank"]).reshape(-1)

    xs = _dispatch(meta, h2t.reshape(t, ROW_TILE, LANES), pos, n_sorted_rows, tm=ts["dispatch"])
    ys = _experts(tile_expert, n_valid, xs.reshape(n_sorted_rows * ROW_TILE, LANES), w_mlp1[0], b1g, b1l,
                  w_mlp2[0], b2, tm=tmx)
    out = _combine(pos, top_w, h2t, row(ln3_g[0]), row(ln3_b[0]),
                   ys.reshape(n_sorted_rows, ROW_TILE, LANES), tm=ts["combine"])
    return out.reshape(batch, seq, d)
```

```python
import functools
import math

import jax
import jax.numpy as jnp
from jax import lax
from jax.experimental import pallas as pl
from jax.experimental.pallas import tpu as pltpu

D_MODEL = 1024
DIFF_HEADS = 4
HEAD_DIM = 64
DIFF_WIDTH = DIFF_HEADS * 2 * HEAD_DIM
MOBA_HEADS = 8
MOBA_WIDTH = MOBA_HEADS * HEAD_DIM
MOBA_BLOCK = 256
MOBA_TOPK = 3
ROPE_THETA = 500000.0
ROT_DIM = HEAD_DIM // 4
MEM_HEADS = 4
MEM_HEAD_DIM = D_MODEL // MEM_HEADS
N_EXPERTS = 32
TOP_K = 4
D_FF = D_MODEL
SWIGLU_ALPHA = 1.702
SWIGLU_LIMIT = 7.0
LN_EPS = 1e-5
RMS_EPS = 1e-5
DEEPNORM_ALPHA = 2.0 ** 0.25
LAMBDA_INIT = 0.8 - 0.6 * math.exp(0.0)

LANES = 128
ROW_TILE = 8
V7X_VMEM_LIMIT = 56 * 1024 * 1024

_CDT = jnp.bfloat16
_F32 = jnp.float32
_NEG = -1e30

_CB_DQ, _CB_DK, _CB_DV = 0, 4, 8
_CB_MQ, _CB_MK, _CB_MV = 12, 16, 20
PROJ_WIDTH = 3 * (DIFF_WIDTH + MOBA_WIDTH)


def _cparams(sem, vmem=V7X_VMEM_LIMIT):
    return pltpu.CompilerParams(dimension_semantics=sem, vmem_limit_bytes=vmem)


def _dot(a, b):
    return jnp.dot(a, b, preferred_element_type=_F32)


def _dot_nt(a, b):
    return lax.dot_general(a, b, (((1,), (1,)), ((), ())), preferred_element_type=_F32)


def _layer_norm(x, g, b):
    mu = jnp.mean(x, axis=-1, keepdims=True)
    xc = x - mu
    var = jnp.mean(xc * xc, axis=-1, keepdims=True)
    return xc * lax.rsqrt(var + LN_EPS) * g + b


_ROPE_BLOCKS = tuple(range(_CB_DQ, _CB_DV)) + tuple(range(_CB_MQ, _CB_MV))


def _inproj_kernel(x_ref, w_ref, cos_ref, sa_ref, sb_ref, o_ref, km_ref, *, tm):
    x = x_ref[...].astype(_CDT)
    cos, sa, sb = cos_ref[...], sa_ref[...], sb_ref[...]
    seg_w = 4 * LANES
    for seg in range(PROJ_WIDTH // seg_w):
        pseg = _dot(x, w_ref[:, seg * seg_w:(seg + 1) * seg_w])
        for c in range(4):
            cb = seg * 4 + c
            cols = slice(cb * LANES, (cb + 1) * LANES)
            p = pseg[:, c * LANES:(c + 1) * LANES]
            if cb in _ROPE_BLOCKS:
                p = p * cos + pltpu.roll(p, 8, 1) * sa + pltpu.roll(p, LANES - 8, 1) * sb
            o_ref[:, cols] = p.astype(o_ref.dtype)
            if _CB_MK <= cb < _CB_MV:
                kc = slice((cb - _CB_MK) * LANES, (cb - _CB_MK + 1) * LANES)
                for blk in range(tm // MOBA_BLOCK):
                    rows = p[blk * MOBA_BLOCK:(blk + 1) * MOBA_BLOCK, :]
                    km_ref[blk, :, kc] = jnp.sum(rows, axis=0, keepdims=True) * (1.0 / MOBA_BLOCK)


def _rope_tables(seq):
    half = ROT_DIM // 2
    inv_freq = ROPE_THETA ** (-jnp.arange(0, ROT_DIM, 2, dtype=_F32) / ROT_DIM)
    ang = jnp.arange(seq, dtype=_F32)[:, None] * inv_freq[None, :]
    cos, sin = jnp.cos(ang), jnp.sin(ang)
    lane = jnp.arange(LANES) % HEAD_DIM
    first, second = lane < half, (lane >= half) & (lane < ROT_DIM)
    idx = jnp.where(first, lane, jnp.where(second, lane - half, 0))
    cos_t = jnp.where((first | second)[None, :], cos[:, idx], 1.0)
    sa_t = jnp.where(second[None, :], sin[:, idx], 0.0)
    sb_t = jnp.where(first[None, :], -sin[:, idx], 0.0)
    return cos_t, sa_t, sb_t


def _inproj(x2d, w_in, seq, *, tm):
    t = x2d.shape[0]
    cos_t, sa_t, sb_t = _rope_tables(seq)
    n_pos = seq // tm
    tab_spec = pl.BlockSpec((tm, LANES), lambda i: (i % n_pos, 0))
    return pl.pallas_call(
        functools.partial(_inproj_kernel, tm=tm),
        out_shape=(jax.ShapeDtypeStruct((t, PROJ_WIDTH), _CDT),
                   jax.ShapeDtypeStruct((t // MOBA_BLOCK, 1, MOBA_WIDTH), _F32)),
        grid=(t // tm,),
        in_specs=[pl.BlockSpec((tm, D_MODEL), lambda i: (i, 0)),
                  pl.BlockSpec((D_MODEL, PROJ_WIDTH), lambda i: (0, 0)),
                  tab_spec, tab_spec, tab_spec],
        out_specs=(pl.BlockSpec((tm, PROJ_WIDTH), lambda i: (i, 0)),
                   pl.BlockSpec((tm // MOBA_BLOCK, 1, MOBA_WIDTH), lambda i: (i, 0, 0))),
        compiler_params=_cparams(("parallel",)),
        name="inproj",
    )(x2d, w_in, cos_t, sa_t, sb_t)


def _exp2_parts(parts):
    chunks = [s[:, c:c + LANES] for s in parts for c in range(0, s.shape[1], LANES)]
    m = jnp.max(functools.reduce(jnp.maximum, chunks), axis=1, keepdims=True)
    return [jnp.exp2(s - m).astype(_CDT) for s in parts]


_SCORES_AHEAD = 1


def _causal_mask(tq):
    row = lax.broadcasted_iota(jnp.int32, (tq, tq), 0)
    col = lax.broadcasted_iota(jnp.int32, (tq, tq), 1)
    return col <= row


def _diff_attn_kernel(q_ref, k_ref, v_ref, lam_ref, g_ref, o_ref, *, seq, tq):
    lane = lax.broadcasted_iota(jnp.int32, (tq, LANES), 1)
    mask = _causal_mask(tq)
    lv = lam_ref[...]
    lam = (jnp.exp(jnp.sum(lv[0:1] * lv[1:2], axis=1, keepdims=True))
           - jnp.exp(jnp.sum(lv[2:3] * lv[3:4], axis=1, keepdims=True)) + LAMBDA_INIT)
    gain = g_ref[...] * (1.0 - LAMBDA_INIT)

    def with_ones(v):
        return jnp.concatenate([v, jnp.ones_like(v)], axis=1)

    def scores(i):
        rows = slice(i * tq, (i + 1) * tq)
        q = q_ref[rows, :]
        zero = jnp.zeros_like(q)
        q1 = jnp.where(lane < HEAD_DIM, q, zero)
        q2 = jnp.where(lane >= HEAD_DIM, q, zero)
        kd = k_ref[rows, :]
        s1 = [jnp.where(mask, _dot_nt(q1, kd), _NEG)]
        s2 = [jnp.where(mask, _dot_nt(q2, kd), _NEG)]
        if i:
            kp = k_ref[:i * tq, :]
            s1.append(_dot_nt(q1, kp))
            s2.append(_dot_nt(q2, kp))
        return s1, s2

    def finish(i, s1, s2):
        rows = slice(i * tq, (i + 1) * tq)
        v_all = with_ones(v_ref[:(i + 1) * tq, :])
        nums = []
        for parts in (s1, s2):
            es = _exp2_parts(parts)
            nl = _dot(jnp.concatenate(es[1:] + es[:1], axis=1), v_all)
            nums.append(nl[:, :LANES] / nl[:, LANES:LANES + 1])
        o = nums[0] - lam * nums[1]
        o = o * lax.rsqrt(jnp.mean(o * o, axis=1, keepdims=True) + RMS_EPS) * gain
        o_ref[rows, :] = o.astype(o_ref.dtype)

    n_tiles = seq // tq
    pending = [scores(i) for i in range(min(_SCORES_AHEAD, n_tiles))]
    for i in range(n_tiles):
        if i + _SCORES_AHEAD < n_tiles:
            pending.append(scores(i + _SCORES_AHEAD))
        finish(i, *pending.pop(0))


def _diff_attn(proj, lam_vecs, subln_g, batch, seq, *, tq):
    t = batch * seq
    blk = lambda cb: pl.BlockSpec((seq, LANES), lambda b, h: (b, cb + h))
    return pl.pallas_call(
        functools.partial(_diff_attn_kernel, seq=seq, tq=tq),
        out_shape=jax.ShapeDtypeStruct((t, DIFF_WIDTH), _CDT),
        grid=(batch, DIFF_HEADS),
        in_specs=[blk(_CB_DQ), blk(_CB_DK), blk(_CB_DV),
                  pl.BlockSpec((4, HEAD_DIM), lambda b, h: (0, 0)),
                  pl.BlockSpec((1, 2 * HEAD_DIM), lambda b, h: (0, 0))],
        out_specs=pl.BlockSpec((seq, LANES), lambda b, h: (b, h)),
        compiler_params=_cparams(("parallel", "parallel")),
        name="diff_attn",
    )(proj, proj, proj, lam_vecs, subln_g)


def _moba_selection(gate, n_past):
    blk = lax.broadcasted_iota(jnp.int32, gate.shape, 0)
    rank = jnp.zeros(gate.shape, _F32)
    for m_blk in range(n_past):
        gm = gate[m_blk:m_blk + 1, :]
        beats = (gm > gate) | ((gm == gate) & (m_blk < blk))
        rank = rank + jnp.where(beats, 1.0, 0.0)
    return jnp.where((blk < n_past) & (rank < MOBA_TOPK), 1.0, 0.0)


def _moba_kernel(q_ref, k_ref, v_ref, km_ref, o_ref, *, nb):
    tq = MOBA_BLOCK
    lane = lax.broadcasted_iota(jnp.int32, (tq, LANES), 1)
    mask = _causal_mask(tq)
    km = km_ref[...].astype(_CDT)
    in_head = [(lane >= hh * HEAD_DIM) & (lane < (hh + 1) * HEAD_DIM) for hh in range(2)]
    seq_lane = lax.broadcasted_iota(jnp.int32, (nb * tq, LANES), 1)
    v_ones = [jnp.where((seq_lane >= hh * HEAD_DIM) & (seq_lane < (hh + 1) * HEAD_DIM), v_ref[...],
                        jnp.ones((nb * tq, LANES), _CDT)) for hh in range(2)]

    def scores(i, hh):
        rows = slice(i * tq, (i + 1) * tq)
        q = q_ref[rows, :]
        qh = jnp.where(in_head[hh], q, jnp.zeros_like(q))
        parts = [jnp.where(mask, _dot_nt(qh, k_ref[rows, :]), _NEG)]
        gated = i > MOBA_TOPK
        if gated:
            sel = _moba_selection(_dot_nt(km, qh), i)
            sel_q = jnp.concatenate([sel, jnp.zeros((LANES - nb, tq), _F32)], axis=0).T
        for j in range(i):
            s = _dot_nt(qh, k_ref[j * tq:(j + 1) * tq, :])
            parts.append(jnp.where(sel_q[:, j:j + 1] > 0.5, s, _NEG) if gated else s)
        return parts

    def finish(i, hh, parts):
        es = _exp2_parts(parts)
        acc = _dot(jnp.concatenate(es[1:] + es[:1], axis=1), v_ones[hh][:(i + 1) * tq, :])
        sum_lane = (1 - hh) * HEAD_DIM
        return acc / acc[:, sum_lane:sum_lane + 1]

    units = [(i, hh) for i in range(nb) for hh in range(2)]
    pending = [scores(*u) for u in units[:_SCORES_AHEAD]]
    outs = {}
    for n, (i, hh) in enumerate(units):
        if n + _SCORES_AHEAD < len(units):
            pending.append(scores(*units[n + _SCORES_AHEAD]))
        outs[hh] = finish(i, hh, pending.pop(0))
        if hh == 1:
            o_ref[i * tq:(i + 1) * tq, :] = jnp.where(in_head[0], outs[0], outs[1]).astype(o_ref.dtype)


def _moba_attn(proj, kmean, batch, seq):
    nb = seq // MOBA_BLOCK
    t = batch * seq
    blk = lambda cb: pl.BlockSpec((seq, LANES), lambda b, p: (b, cb + p))
    return pl.pallas_call(
        functools.partial(_moba_kernel, nb=nb),
        out_shape=jax.ShapeDtypeStruct((t, MOBA_WIDTH), _CDT),
        grid=(batch, MOBA_HEADS // 2),
        in_specs=[blk(_CB_MQ), blk(_CB_MK), blk(_CB_MV),
                  pl.BlockSpec((None, nb, LANES), lambda b, p: (b, 0, p))],
        out_specs=pl.BlockSpec((seq, LANES), lambda b, p: (b, p)),
        compiler_params=_cparams(("parallel", "parallel")),
        name="moba_attn",
    )(proj, proj, proj, kmean)


def _mem_kv_kernel(mem_ref, g_ref, b_ref, w_ref, o_ref):
    mn = _layer_norm(mem_ref[...], g_ref[...], b_ref[...])
    o_ref[...] = _dot(mn.astype(_CDT), w_ref[...]).astype(o_ref.dtype)


def _mem_kv(mem2d, g, b, w_kv, *, tm):
    rows = mem2d.shape[0]
    vec = pl.BlockSpec((1, D_MODEL), lambda i: (0, 0))
    return pl.pallas_call(
        _mem_kv_kernel,
        out_shape=jax.ShapeDtypeStruct((rows, 2 * D_MODEL), _CDT),
        grid=(rows // tm,),
        in_specs=[pl.BlockSpec((tm, D_MODEL), lambda i: (i, 0)), vec, vec,
                  pl.BlockSpec((D_MODEL, 2 * D_MODEL), lambda i: (0, 0))],
        out_specs=pl.BlockSpec((tm, 2 * D_MODEL), lambda i: (i, 0)),
        compiler_params=_cparams(("parallel",)),
        name="mem_kv",
    )(mem2d, g, b, w_kv)


def _store_row_tiles(ref, val):
    n = val.shape[0]
    for c in range(ROW_TILE):
        ref[pl.ds(c, n, stride=ROW_TILE), :] = val[:, c * LANES:(c + 1) * LANES]


def _load_row_tiles(ref, n):
    return jnp.concatenate([ref[pl.ds(c, n, stride=ROW_TILE), :] for c in range(ROW_TILE)], axis=1)


_DONE = object()


def _interleave(stage_generators):
    live = list(stage_generators)
    while live:
        live = [g for g in live if next(g, _DONE) is not _DONE]


def _post_attn_kernel(x_ref, oa_ref, ob_ref, wout_ref, g1_ref, b1_ref, kv_ref, wq_ref, wo_ref,
                      g2_ref, b2_ref, wr_ref, br_ref, h2t_ref, idx_ref, wt_ref, cnt_ref, cnt_acc, *, tm, groups):
    @pl.when(pl.program_id(0) == 0)
    def _init_counts():
        cnt_acc[...] = jnp.zeros_like(cnt_acc)

    n = tm // groups
    lane = lax.broadcasted_iota(jnp.int32, (n, LANES), 1)
    lane_f = lane.astype(_F32)

    def row_group(g):
        rows = slice(g * n, (g + 1) * n)
        mix = _dot(oa_ref[rows, :], wout_ref[:DIFF_WIDTH, :]) + _dot(ob_ref[rows, :], wout_ref[DIFF_WIDTH:, :])
        yield
        h1 = _layer_norm(DEEPNORM_ALPHA * x_ref[rows, :] + mix, g1_ref[...], b1_ref[...])
        q = (_dot(h1.astype(_CDT), wq_ref[...]) * (MEM_HEAD_DIM ** -0.5)).astype(_CDT)
        yield
        heads = []
        for h in range(MEM_HEADS):
            c0 = h * MEM_HEAD_DIM
            kh = kv_ref[:, c0:c0 + MEM_HEAD_DIM]
            vh = kv_ref[:, D_MODEL + c0:D_MODEL + c0 + MEM_HEAD_DIM]
            s = _dot_nt(q[:, c0:c0 + MEM_HEAD_DIM], kh)
            p = jnp.exp(s - jnp.max(s, axis=1, keepdims=True))
            o = _dot(p.astype(_CDT), vh) / jnp.sum(p, axis=1, keepdims=True)
            heads.append(o.astype(_CDT))
        yield
        xatt = _dot(jnp.concatenate(heads, axis=1), wo_ref[...])
        yield
        h2 = _layer_norm(DEEPNORM_ALPHA * h1 + xatt, g2_ref[...], b2_ref[...])
        _store_row_tiles(h2t_ref.at[pl.ds(g * n * ROW_TILE, n * ROW_TILE)], h2)
        logits = _dot(h2.astype(_CDT), wr_ref[...]) + br_ref[...]
        yield
        vals, idxs = [], []
        work = logits
        for _ in range(TOP_K):
            m = jnp.max(work, axis=1, keepdims=True)
            ik = jnp.min(jnp.where(work == m, lane_f, float(LANES)), axis=1, keepdims=True)
            vals.append(m)
            idxs.append(ik)
            work = jnp.where(lane_f == ik, -jnp.inf, work)
        exps = [jnp.exp(v - vals[0]) for v in vals]
        denom = exps[0] + exps[1] + exps[2] + exps[3]
        idx_w = jnp.zeros((n, LANES), _F32)
        wt_w = jnp.zeros((n, LANES), _F32)
        for k in range(TOP_K):
            idx_w = jnp.where(lane == k, idxs[k], idx_w)
            wt_w = jnp.where(lane == k, exps[k] / denom, wt_w)
        idx_ref[rows, :] = idx_w[:, :TOP_K].astype(jnp.int32)
        wt_ref[rows, :] = wt_w[:, :TOP_K]
        chosen = functools.reduce(jnp.add, [jnp.where(lane_f == ik, 1.0, 0.0) for ik in idxs])
        cnt_acc[...] = cnt_acc[...] + jnp.sum(chosen, axis=0, keepdims=True)

    _interleave(row_group(g) for g in range(groups))
    cnt_ref[...] = cnt_acc[...].astype(jnp.int32)


def _post_attn(x2d, out_a, out_b, w_out, g1, b1, kv, w_q, w_o, g2, b2, w_r, b_r, seq, mem_len, *, tm):
    t = x2d.shape[0]
    per_b = seq // tm
    full = lambda shape: pl.BlockSpec(shape, lambda i: (0,) * len(shape))
    vec = full((1, D_MODEL))
    return pl.pallas_call(
        functools.partial(_post_attn_kernel, tm=tm, groups=max(1, tm // MOBA_BLOCK)),
        out_shape=(jax.ShapeDtypeStruct((t * ROW_TILE, LANES), _F32),
                   jax.ShapeDtypeStruct((t, TOP_K), jnp.int32),
                   jax.ShapeDtypeStruct((t, TOP_K), _F32),
                   jax.ShapeDtypeStruct((1, LANES), jnp.int32)),
        grid=(t // tm,),
        in_specs=[pl.BlockSpec((tm, D_MODEL), lambda i: (i, 0)),
                  pl.BlockSpec((tm, DIFF_WIDTH), lambda i: (i, 0)),
                  pl.BlockSpec((tm, MOBA_WIDTH), lambda i: (i, 0)),
                  full((D_MODEL, D_MODEL)), vec, vec,
                  pl.BlockSpec((mem_len, 2 * D_MODEL), lambda i: (i // per_b, 0)),
                  full((D_MODEL, D_MODEL)), full((D_MODEL, D_MODEL)), vec, vec,
                  full((D_MODEL, LANES)), full((1, LANES))],
        out_specs=(pl.BlockSpec((tm * ROW_TILE, LANES), lambda i: (i, 0)),
                   pl.BlockSpec((tm, TOP_K), lambda i: (i, 0)),
                   pl.BlockSpec((tm, TOP_K), lambda i: (i, 0)),
                   pl.BlockSpec((1, LANES), lambda i: (0, 0))),
        scratch_shapes=[pltpu.VMEM((1, LANES), _F32)],
        compiler_params=_cparams(("arbitrary",)),
        name="post_attn",
    )(x2d, out_a, out_b, w_out, g1, b1, kv, w_q, w_o, g2, b2, w_r, b_r)


def _rank_kernel(idx_ref, off_ref, pos_ref, next_ref, *, tm):
    @pl.when(pl.program_id(0) == 0)
    def _init():
        next_ref[...] = off_ref[...]

    idx = idx_ref[...]
    lane = lax.broadcasted_iota(jnp.int32, (tm, LANES), 1)
    hits = [lane == idx[:, k:k + 1] for k in range(TOP_K)]
    sel = jnp.zeros((tm, LANES), _F32)
    for hit in hits:
        sel = sel + jnp.where(hit, 1.0, 0.0)
    row = lax.broadcasted_iota(jnp.int32, (tm, tm), 0)
    col = lax.broadcasted_iota(jnp.int32, (tm, tm), 1)
    earlier = jnp.where(col < row, 1.0, 0.0).astype(jnp.bfloat16)
    dense = _dot(earlier, sel.astype(jnp.bfloat16)) + next_ref[...]
    pos_w = jnp.zeros((tm, LANES), _F32)
    for k, hit in enumerate(hits):
        pk = jnp.sum(jnp.where(hit, dense, 0.0), axis=1, keepdims=True)
        pos_w = jnp.where(lane == k, pk, pos_w)
    pos_ref[...] = pos_w[:, :TOP_K].astype(jnp.int32)
    next_ref[...] = next_ref[...] + jnp.sum(sel, axis=0, keepdims=True)


def _rank(idx, offsets, *, tm):
    t = idx.shape[0]
    return pl.pallas_call(
        functools.partial(_rank_kernel, tm=tm),
        out_shape=jax.ShapeDtypeStruct((t, TOP_K), jnp.int32),
        grid=(t // tm,),
        in_specs=[pl.BlockSpec((tm, TOP_K), lambda i: (i, 0)), pl.BlockSpec((1, LANES), lambda i: (0, 0))],
        out_specs=pl.BlockSpec((tm, TOP_K), lambda i: (i, 0)),
        scratch_shapes=[pltpu.VMEM((1, LANES), _F32)],
        compiler_params=_cparams(("arbitrary",)),
        name="rank",
    )(idx, offsets)


_DMA_UNROLL = 8
_DMA_THREADS = 2


def _for_each_row(n_rows, fn):
    tokens_per_group = _DMA_UNROLL // TOP_K

    def group(g, c):
        for u in range(_DMA_UNROLL):
            fn(g * _DMA_UNROLL + u, g * tokens_per_group + u // TOP_K, u % TOP_K)
        return c

    lax.fori_loop(0, n_rows // _DMA_UNROLL, group, 0)


def _dispatch_kernel(meta_ref, h_ref, pos_ref, xs_ref, zero_ref, sem, zsem, *, tm):
    n_rows = tm * TOP_K

    def zero_copy(r):
        return pltpu.make_async_copy(zero_ref, xs_ref.at[r], zsem)

    @pl.when(pl.program_id(0) == 0)
    def _zero_fill():
        zero_ref[...] = jnp.zeros_like(zero_ref)

        def per_expert(e, c):
            lo = meta_ref[e] + meta_ref[N_EXPERTS + e]
            hi = meta_ref[e] + meta_ref[2 * N_EXPERTS + e]
            lax.fori_loop(lo, hi, lambda r, c2: (zero_copy(r).start(), c2)[1], 0)
            lax.fori_loop(lo, hi, lambda r, c2: (zero_copy(r).wait(), c2)[1], 0)
            return c

        lax.fori_loop(0, N_EXPERTS, per_expert, 0)

    def row_copy(r, t, k):
        return pltpu.make_async_copy(h_ref.at[t], xs_ref.at[pos_ref[r]], sem)

    _for_each_row(n_rows, lambda r, t, k: row_copy(r, t, k).start(priority=k % _DMA_THREADS))
    _for_each_row(n_rows, lambda r, t, k: row_copy(r, t, k).wait())


def _dispatch(meta, h2t, pos_flat, n_sorted_rows, *, tm):
    t = h2t.shape[0]
    return pl.pallas_call(
        functools.partial(_dispatch_kernel, tm=tm),
        out_shape=jax.ShapeDtypeStruct((n_sorted_rows, ROW_TILE, LANES), _F32),
        grid_spec=pltpu.PrefetchScalarGridSpec(
            num_scalar_prefetch=1,
            grid=(t // tm,),
            in_specs=[pl.BlockSpec((tm, ROW_TILE, LANES), lambda i, meta: (i, 0, 0)),
                      pl.BlockSpec((tm * TOP_K,), lambda i, meta: (i,), memory_space=pltpu.SMEM)],
            out_specs=pl.BlockSpec(memory_space=pl.ANY),
            scratch_shapes=[pltpu.VMEM((ROW_TILE, LANES), _F32),
                            pltpu.SemaphoreType.DMA, pltpu.SemaphoreType.DMA]),
        compiler_params=_cparams(("arbitrary",)),
        name="dispatch",
    )(meta, h2t, pos_flat)


_DEINT = 2 * LANES


def _deinterleave_matrix():
    k = jnp.arange(_DEINT)[:, None]
    n = jnp.arange(_DEINT)[None, :]
    src = jnp.where(n < LANES, 2 * n, 2 * (n - LANES) + 1)
    return (k == src).astype(_CDT)


def _experts_kernel(te_ref, nv_ref, xs_ref, perm_ref, w1_ref, b1g_ref, b1l_ref, w2_ref, b2_ref, y_ref,
                    wg_ref, wl_ref, w2c_ref, *, tm):
    i = pl.program_id(0)
    live = i < nv_ref[0]
    new_expert = (i == 0) | (te_ref[i] != te_ref[jnp.maximum(i - 1, 0)])

    @pl.when(live & new_expert)
    def _stage_weights():
        for c in range(2 * D_FF // _DEINT):
            both = _dot(w1_ref[0, :, c * _DEINT:(c + 1) * _DEINT].astype(_CDT), perm_ref[...])
            wg_ref[:, c * LANES:(c + 1) * LANES] = both[:, :LANES].astype(_CDT)
            wl_ref[:, c * LANES:(c + 1) * LANES] = both[:, LANES:].astype(_CDT)
        w2c_ref[...] = w2_ref[0].astype(_CDT)

    @pl.when(live)
    def _mlp():
        x = _load_row_tiles(xs_ref, tm).astype(_CDT)
        glu = jnp.minimum(_dot(x, wg_ref[...]) + b1g_ref[0], SWIGLU_LIMIT)
        lin = jnp.clip(_dot(x, wl_ref[...]) + b1l_ref[0], -SWIGLU_LIMIT, SWIGLU_LIMIT)
        act = glu * jax.nn.sigmoid(SWIGLU_ALPHA * glu) * (lin + 1.0)
        _store_row_tiles(y_ref, _dot(act.astype(_CDT), w2c_ref[...]) + b2_ref[0])

    @pl.when(jnp.logical_not(live))
    def _unused_tile():
        y_ref[...] = jnp.zeros_like(y_ref)


def _experts(tile_expert, n_valid, xs, w1, b1g, b1l, w2, b2, *, tm):
    n_tiles = xs.shape[0] // (tm * ROW_TILE)
    by_expert = lambda shape: pl.BlockSpec((1,) + shape, lambda i, te, nv: (te[i], 0, 0))
    return pl.pallas_call(
        functools.partial(_experts_kernel, tm=tm),
        out_shape=jax.ShapeDtypeStruct(xs.shape, _F32),
        grid_spec=pltpu.PrefetchScalarGridSpec(
            num_scalar_prefetch=2,
            grid=(n_tiles,),
            in_specs=[pl.BlockSpec((tm * ROW_TILE, LANES), lambda i, te, nv: (jnp.minimum(i, nv[0] - 1), 0)),
                      pl.BlockSpec((_DEINT, _DEINT), lambda i, te, nv: (0, 0)),
                      by_expert((D_MODEL, 2 * D_FF)),
                      by_expert((1, D_FF)), by_expert((1, D_FF)),
                      by_expert((D_FF, D_MODEL)), by_expert((1, D_MODEL))],
            out_specs=pl.BlockSpec((tm * ROW_TILE, LANES), lambda i, te, nv: (i, 0)),
            scratch_shapes=[pltpu.VMEM((D_MODEL, D_FF), _CDT), pltpu.VMEM((D_MODEL, D_FF), _CDT),
                            pltpu.VMEM((D_FF, D_MODEL), _CDT)]),
        compiler_params=_cparams(("arbitrary",)),
        name="experts",
    )(tile_expert, n_valid, xs, _deinterleave_matrix(), w1, b1g, b1l, w2, b2)


def _combine_kernel(pos_ref, nxt_ref, wt_ref, h2t_ref, g_ref, b_ref, ys_ref, o_ref, buf_ref, sem, *, tm):
    n_rows = tm * TOP_K
    i = pl.program_id(0)
    slot = i & 1

    def row_copy(p_ref, s, r, t, k):
        tile_rows = pl.ds(pl.multiple_of(t * ROW_TILE, ROW_TILE), ROW_TILE)
        return pltpu.make_async_copy(ys_ref.at[p_ref[r]], buf_ref.at[s, k, tile_rows], sem.at[s])

    @pl.when(i == 0)
    def _first():
        _for_each_row(n_rows, lambda r, t, k: row_copy(pos_ref, 0, r, t, k).start(priority=k % _DMA_THREADS))

    for r in range(n_rows):
        row_copy(pos_ref, slot, r, r // TOP_K, r % TOP_K).wait()
    for r in range(n_rows):
        row_copy(nxt_ref, 1 - slot, r, r // TOP_K, r % TOP_K).start(priority=r % _DMA_THREADS)
    wt = wt_ref[...]
    ffn = wt[:, 0:1] * _load_row_tiles(buf_ref.at[slot, 0], tm)
    for k in range(1, TOP_K):
        ffn = ffn + wt[:, k:k + 1] * _load_row_tiles(buf_ref.at[slot, k], tm)
    o_ref[...] = _layer_norm(DEEPNORM_ALPHA * _load_row_tiles(h2t_ref, tm) + ffn, g_ref[...], b_ref[...])

    @pl.when(i + 1 == pl.num_programs(0))
    def _drain():
        _for_each_row(n_rows, lambda r, t, k: row_copy(nxt_ref, 1 - slot, r, t, k).wait())


def _combine(pos_flat, wts, h2t, g3, b3, ys, *, tm):
    t = h2t.shape[0] // ROW_TILE
    n_steps = t // tm
    vec = pl.BlockSpec((1, D_MODEL), lambda i: (0, 0))
    return pl.pallas_call(
        functools.partial(_combine_kernel, tm=tm),
        out_shape=jax.ShapeDtypeStruct((t, D_MODEL), _F32),
        grid=(n_steps,),
        in_specs=[pl.BlockSpec((tm * TOP_K,), lambda i: (i,), memory_space=pltpu.SMEM),
                  pl.BlockSpec((tm * TOP_K,), lambda i: (jnp.minimum(i + 1, n_steps - 1),), memory_space=pltpu.SMEM),
                  pl.BlockSpec((tm, TOP_K), lambda i: (i, 0)),
                  pl.BlockSpec((tm * ROW_TILE, LANES), lambda i: (i, 0)), vec, vec,
                  pl.BlockSpec(memory_space=pl.ANY)],
        out_specs=pl.BlockSpec((tm, D_MODEL), lambda i: (i, 0)),
        scratch_shapes=[pltpu.VMEM((2, TOP_K, tm * ROW_TILE, LANES), _F32), pltpu.SemaphoreType.DMA((2,))],
        compiler_params=_cparams(("arbitrary",)),
        name="combine",
    )(pos_flat, pos_flat, wts, h2t, g3, b3, ys)


def _tiles(seq):
    big = 512 if seq % 512 == 0 else MOBA_BLOCK
    post = 1024 if seq % 1024 == 0 else big
    return dict(inproj=big, attn_q=MOBA_BLOCK, post=post, rank=big, dispatch=big, expert=512, combine=big)


def kernel(x, mem, w_in, diff_lambda_q1, diff_lambda_k1, diff_lambda_q2, diff_lambda_k2, diff_subln_g,
           w_mix_out, ln1_g, ln1_b, mem_ln_g, mem_ln_b, w_mem_q, w_mem_kv, w_mem_o, ln2_g, ln2_b, w_router,
           b_router, w_mlp1, b_mlp1, w_mlp2, b_mlp2, ln3_g, ln3_b):
    batch, seq, d = x.shape
    mem_len = mem.shape[1]
    assert d == D_MODEL and seq % MOBA_BLOCK == 0 and w_in.shape[0] == 1
    t = batch * seq
    ts = _tiles(seq)
    row = lambda v: v.reshape(1, -1).astype(_F32)

    col = jnp.arange(PROJ_WIDTH)
    is_q = (col < DIFF_WIDTH) | ((col >= 3 * DIFF_WIDTH) & (col < 3 * DIFF_WIDTH + MOBA_WIDTH))
    w_in_c = (w_in[0] * jnp.where(is_q, HEAD_DIM ** -0.5 * math.log2(math.e), 1.0)[None, :]).astype(_CDT)
    lam_vecs = jnp.stack([diff_lambda_q1[0], diff_lambda_k1[0], diff_lambda_q2[0], diff_lambda_k2[0]]).astype(_F32)
    w_r = jnp.zeros((D_MODEL, LANES), _F32).at[:, :N_EXPERTS].set(w_router[0]).astype(_CDT)
    b_r = jnp.full((1, LANES), _NEG, _F32).at[0, :N_EXPERTS].set(b_router[0])
    b1g = b_mlp1[0][:, None, 0::2].astype(_F32)
    b1l = b_mlp1[0][:, None, 1::2].astype(_F32)
    b2 = b_mlp2[0][:, None, :].astype(_F32)

    x2d = x.reshape(t, d)
    proj, kmean = _inproj(x2d, w_in_c, seq, tm=ts["inproj"])
    kmean = kmean.reshape(batch, seq // MOBA_BLOCK, MOBA_WIDTH)
    out_a = _diff_attn(proj, lam_vecs, row(diff_subln_g[0]), batch, seq, tq=ts["attn_q"])
    out_b = _moba_attn(proj, kmean, batch, seq)
    kv = _mem_kv(mem.reshape(batch * mem_len, d), row(mem_ln_g), row(mem_ln_b), w_mem_kv[0].astype(_CDT),
                 tm=mem_len)
    h2t, top_idx, top_w, counts = _post_attn(
        x2d, out_a, out_b, w_mix_out[0].astype(_CDT), row(ln1_g[0]), row(ln1_b[0]), kv,
        w_mem_q[0].astype(_CDT), w_mem_o[0].astype(_CDT), row(ln2_g[0]), row(ln2_b[0]), w_r, b_r,
        seq, mem_len, tm=ts["post"])

    tmx = ts["expert"]
    cnt = counts[0, :N_EXPERTS]
    padded = (cnt + tmx - 1) // tmx * tmx
    ends = jnp.cumsum(padded)
    offsets = ends - padded
    meta = jnp.concatenate([offsets, cnt, padded]).astype(jnp.int32)
    n_sorted_rows = t * TOP_K + N_EXPERTS * tmx
    n_tiles = n_sorted_rows // tmx
    tile_start = jnp.arange(n_tiles, dtype=jnp.int32) * tmx
    tile_expert = jnp.minimum(jnp.sum(tile_start[:, None] >= ends[None, :], axis=1), N_EXPERTS - 1).astype(jnp.int32)
    n_valid = (ends[-1:] // tmx).astype(jnp.int32)
    offsets_row = jnp.zeros((1, LANES), _F32).at[0, :N_EXPERTS].set(offsets.astype(_F32))
    pos = _rank(top_idx, offsets_row, tm=ts["rank"]).reshape(-1)

    xs = _dispatch(meta, h2t.reshape(t, ROW_TILE, LANES), pos, n_sorted_rows, tm=ts["dispatch"])
    ys = _experts(tile_expert, n_valid, xs.reshape(n_sorted_rows * ROW_TILE, LANES), w_mlp1[0], b1g, b1l,
                  w_mlp2[0], b2, tm=tmx)
    out = _combine(pos, top_w, h2t, row(ln3_g[0]), row(ln3_b[0]),
                   ys.reshape(n_sorted_rows, ROW_TILE, LANES), tm=ts["combine"])
    return out.reshape(batch, seq, d)
```

```python
import functools
import math

import jax
import jax.numpy as jnp
from jax import lax
from jax.experimental import pallas as pl
from jax.experimental.pallas import tpu as pltpu

D_MODEL = 1024
DIFF_HEADS = 4
HEAD_DIM = 64
DIFF_WIDTH = DIFF_HEADS * 2 * HEAD_DIM
MOBA_HEADS = 8
MOBA_WIDTH = MOBA_HEADS * HEAD_DIM
MOBA_BLOCK = 256
MOBA_TOPK = 3
ROPE_THETA = 500000.0
ROT_DIM = HEAD_DIM // 4
MEM_HEADS = 4
MEM_HEAD_DIM = D_MODEL // MEM_HEADS
N_EXPERTS = 32
TOP_K = 4
D_FF = D_MODEL
SWIGLU_ALPHA = 1.702
SWIGLU_LIMIT = 7.0
LN_EPS = 1e-5
RMS_EPS = 1e-5
DEEPNORM_ALPHA = 2.0 ** 0.25
LAMBDA_INIT = 0.8 - 0.6 * math.exp(0.0)

LANES = 128
ROW_TILE = 8
V7X_VMEM_LIMIT = 56 * 1024 * 1024

_CDT = jnp.bfloat16
_F32 = jnp.float32
_NEG = -1e30

_CB_DQ, _CB_DK, _CB_DV = 0, 4, 8
_CB_MQ, _CB_MK, _CB_MV = 12, 16, 20
PROJ_WIDTH = 3 * (DIFF_WIDTH + MOBA_WIDTH)


def _cparams(sem, vmem=V7X_VMEM_LIMIT):
    return pltpu.CompilerParams(dimension_semantics=sem, vmem_limit_bytes=vmem)


def _dot(a, b):
    return jnp.dot(a, b, preferred_element_type=_F32)


def _dot_nt(a, b):
    return lax.dot_general(a, b, (((1,), (1,)), ((), ())), preferred_element_type=_F32)


def _layer_norm(x, g, b):
    mu = jnp.mean(x, axis=-1, keepdims=True)
    xc = x - mu
    var = jnp.mean(xc * xc, axis=-1, keepdims=True)
    return xc * lax.rsqrt(var + LN_EPS) * g + b


_ROPE_BLOCKS = tuple(range(_CB_DQ, _CB_DV)) + tuple(range(_CB_MQ, _CB_MV))


def _inproj_kernel(x_ref, w_ref, cos_ref, sa_ref, sb_ref, o_ref, km_ref, *, tm):
    x = x_ref[...].astype(_CDT)
    cos, sa, sb = cos_ref[...], sa_ref[...], sb_ref[...]
    seg_w = 4 * LANES
    for seg in range(PROJ_WIDTH // seg_w):
        pseg = _dot(x, w_ref[:, seg * seg_w:(seg + 1) * seg_w])
        for c in range(4):
            cb = seg * 4 + c
            cols = slice(cb * LANES, (cb + 1) * LANES)
            p = pseg[:, c * LANES:(c + 1) * LANES]
            if cb in _ROPE_BLOCKS:
                p = p * cos + pltpu.roll(p, 8, 1) * sa + pltpu.roll(p, LANES - 8, 1) * sb
            o_ref[:, cols] = p.astype(o_ref.dtype)
            if _CB_MK <= cb < _CB_MV:
                kc = slice((cb - _CB_MK) * LANES, (cb - _CB_MK + 1) * LANES)
                for blk in range(tm // MOBA_BLOCK):
                    rows = p[blk * MOBA_BLOCK:(blk + 1) * MOBA_BLOCK, :]
                    km_ref[blk, :, kc] = jnp.sum(rows, axis=0, keepdims=True) * (1.0 / MOBA_BLOCK)


def _rope_tables(seq):
    half = ROT_DIM // 2
    inv_freq = ROPE_THETA ** (-jnp.arange(0, ROT_DIM, 2, dtype=_F32) / ROT_DIM)
    ang = jnp.arange(seq, dtype=_F32)[:, None] * inv_freq[None, :]
    cos, sin = jnp.cos(ang), jnp.sin(ang)
    lane = jnp.arange(LANES) % HEAD_DIM
    first, second = lane < half, (lane >= half) & (lane < ROT_DIM)
    idx = jnp.where(first, lane, jnp.where(second, lane - half, 0))
    cos_t = jnp.where((first | second)[None, :], cos[:, idx], 1.0)
    sa_t = jnp.where(second[None, :], sin[:, idx], 0.0)
    sb_t = jnp.where(first[None, :], -sin[:, idx], 0.0)
    return cos_t, sa_t, sb_t


def _inproj(x2d, w_in, seq, *, tm):
    t = x2d.shape[0]
    cos_t, sa_t, sb_t = _rope_tables(seq)
    n_pos = seq // tm
    tab_spec = pl.BlockSpec((tm, LANES), lambda i: (i % n_pos, 0))
    return pl.pallas_call(
        functools.partial(_inproj_kernel, tm=tm),
        out_shape=(jax.ShapeDtypeStruct((t, PROJ_WIDTH), _CDT),
                   jax.ShapeDtypeStruct((t // MOBA_BLOCK, 1, MOBA_WIDTH), _F32)),
        grid=(t // tm,),
        in_specs=[pl.BlockSpec((tm, D_MODEL), lambda i: (i, 0)),
                  pl.BlockSpec((D_MODEL, PROJ_WIDTH), lambda i: (0, 0)),
                  tab_spec, tab_spec, tab_spec],
        out_specs=(pl.BlockSpec((tm, PROJ_WIDTH), lambda i: (i, 0)),
                   pl.BlockSpec((tm // MOBA_BLOCK, 1, MOBA_WIDTH), lambda i: (i, 0, 0))),
        compiler_params=_cparams(("parallel",)),
        name="inproj",
    )(x2d, w_in, cos_t, sa_t, sb_t)


def _exp2_parts(parts):
    chunks = [s[:, c:c + LANES] for s in parts for c in range(0, s.shape[1], LANES)]
    m = jnp.max(functools.reduce(jnp.maximum, chunks), axis=1, keepdims=True)
    return [jnp.exp2(s - m).astype(_CDT) for s in parts]


_SCORES_AHEAD = 1


def _causal_mask(tq):
    row = lax.broadcasted_iota(jnp.int32, (tq, tq), 0)
    col = lax.broadcasted_iota(jnp.int32, (tq, tq), 1)
    return col <= row


def _diff_attn_kernel(q_ref, k_ref, v_ref, lam_ref, g_ref, o_ref, *, seq, tq):
    lane = lax.broadcasted_iota(jnp.int32, (tq, LANES), 1)
    mask = _causal_mask(tq)
    lv = lam_ref[...]
    lam = (jnp.exp(jnp.sum(lv[0:1] * lv[1:2], axis=1, keepdims=True))
           - jnp.exp(jnp.sum(lv[2:3] * lv[3:4], axis=1, keepdims=True)) + LAMBDA_INIT)
    gain = g_ref[...] * (1.0 - LAMBDA_INIT)

    def with_ones(v):
        return jnp.concatenate([v, jnp.ones_like(v)], axis=1)

    def scores(i):
        rows = slice(i * tq, (i + 1) * tq)
        q = q_ref[rows, :]
        zero = jnp.zeros_like(q)
        q1 = jnp.where(lane < HEAD_DIM, q, zero)
        q2 = jnp.where(lane >= HEAD_DIM, q, zero)
        kd = k_ref[rows, :]
        s1 = [jnp.where(mask, _dot_nt(q1, kd), _NEG)]
        s2 = [jnp.where(mask, _dot_nt(q2, kd), _NEG)]
        if i:
            kp = k_ref[:i * tq, :]
            s1.append(_dot_nt(q1, kp))
            s2.append(_dot_nt(q2, kp))
        return s1, s2

    def finish(i, s1, s2):
        rows = slice(i * tq, (i + 1) * tq)
        v_all = with_ones(v_ref[:(i + 1) * tq, :])
        nums = []
        for parts in (s1, s2):
            es = _exp2_parts(parts)
            nl = _dot(jnp.concatenate(es[1:] + es[:1], axis=1), v_all)
            nums.append(nl[:, :LANES] / nl[:, LANES:LANES + 1])
        o = nums[0] - lam * nums[1]
        o = o * lax.rsqrt(jnp.mean(o * o, axis=1, keepdims=True) + RMS_EPS) * gain
        o_ref[rows, :] = o.astype(o_ref.dtype)

    n_tiles = seq // tq
    pending = [scores(i) for i in range(min(_SCORES_AHEAD, n_tiles))]
    for i in range(n_tiles):
        if i + _SCORES_AHEAD < n_tiles:
            pending.append(scores(i + _SCORES_AHEAD))
        finish(i, *pending.pop(0))


def _diff_attn(proj, lam_vecs, subln_g, batch, seq, *, tq):
    t = batch * seq
    blk = lambda cb: pl.BlockSpec((seq, LANES), lambda b, h: (b, cb + h))
    return pl.pallas_call(
        functools.partial(_diff_attn_kernel, seq=seq, tq=tq),
        out_shape=jax.ShapeDtypeStruct((t, DIFF_WIDTH), _CDT),
        grid=(batch, DIFF_HEADS),
        in_specs=[blk(_CB_DQ), blk(_CB_DK), blk(_CB_DV),
                  pl.BlockSpec((4, HEAD_DIM), lambda b, h: (0, 0)),
                  pl.BlockSpec((1, 2 * HEAD_DIM), lambda b, h: (0, 0))],
        out_specs=pl.BlockSpec((seq, LANES), lambda b, h: (b, h)),
        compiler_params=_cparams(("parallel", "parallel")),
        name="diff_attn",
    )(proj, proj, proj, lam_vecs, subln_g)


def _moba_selection(gate, n_past):
    blk = lax.broadcasted_iota(jnp.int32, gate.shape, 0)
    rank = jnp.zeros(gate.shape, _F32)
    for m_blk in range(n_past):
        gm = gate[m_blk:m_blk + 1, :]
        beats = (gm > gate) | ((gm == gate) & (m_blk < blk))
        rank = rank + jnp.where(beats, 1.0, 0.0)
    return jnp.where((blk < n_past) & (rank < MOBA_TOPK), 1.0, 0.0)


def _moba_kernel(q_ref, k_ref, v_ref, km_ref, o_ref, *, nb):
    tq = MOBA_BLOCK
    lane = lax.broadcasted_iota(jnp.int32, (tq, LANES), 1)
    mask = _causal_mask(tq)
    km = km_ref[...].astype(_CDT)
    in_head = [(lane >= hh * HEAD_DIM) & (lane < (hh + 1) * HEAD_DIM) for hh in range(2)]
    seq_lane = lax.broadcasted_iota(jnp.int32, (nb * tq, LANES), 1)
    v_ones = [jnp.where((seq_lane >= hh * HEAD_DIM) & (seq_lane < (hh + 1) * HEAD_DIM), v_ref[...],
                        jnp.ones((nb * tq, LANES), _CDT)) for hh in range(2)]

    def scores(i, hh):
        rows = slice(i * tq, (i + 1) * tq)
        q = q_ref[rows, :]
        qh = jnp.where(in_head[hh], q, jnp.zeros_like(q))
        parts = [jnp.where(mask, _dot_nt(qh, k_ref[rows, :]), _NEG)]
        gated = i > MOBA_TOPK
        if gated:
            sel = _moba_selection(_dot_nt(km, qh), i)
            sel_q = jnp.concatenate([sel, jnp.zeros((LANES - nb, tq), _F32)], axis=0).T
        for j in range(i):
            s = _dot_nt(qh, k_ref[j * tq:(j + 1) * tq, :])
            parts.append(jnp.where(sel_q[:, j:j + 1] > 0.5, s, _NEG) if gated else s)
        return parts

    def finish(i, hh, parts):
        es = _exp2_parts(parts)
        acc = _dot(jnp.concatenate(es[1:] + es[:1], axis=1), v_ones[hh][:(i + 1) * tq, :])
        sum_lane = (1 - hh) * HEAD_DIM
        return acc / acc[:, sum_lane:sum_lane + 1]

    units = [(i, hh) for i in range(nb) for hh in range(2)]
    pending = [scores(*u) for u in units[:_SCORES_AHEAD]]
    outs = {}
    for n, (i, hh) in enumerate(units):
        if n + _SCORES_AHEAD < len(units):
            pending.append(scores(*units[n + _SCORES_AHEAD]))
        outs[hh] = finish(i, hh, pending.pop(0))
        if hh == 1:
            o_ref[i * tq:(i + 1) * tq, :] = jnp.where(in_head[0], outs[0], outs[1]).astype(o_ref.dtype)


def _moba_attn(proj, kmean, batch, seq):
    nb = seq // MOBA_BLOCK
    t = batch * seq
    blk = lambda cb: pl.BlockSpec((seq, LANES), lambda b, p: (b, cb + p))
    return pl.pallas_call(
        functools.partial(_moba_kernel, nb=nb),
        out_shape=jax.ShapeDtypeStruct((t, MOBA_WIDTH), _CDT),
        grid=(batch, MOBA_HEADS // 2),
        in_specs=[blk(_CB_MQ), blk(_CB_MK), blk(_CB_MV),
                  pl.BlockSpec((None, nb, LANES), lambda b, p: (b, 0, p))],
        out_specs=pl.BlockSpec((seq, LANES), lambda b, p: (b, p)),
        compiler_params=_cparams(("parallel", "parallel")),
        name="moba_attn",
    )(proj, proj, proj, kmean)


def _mem_kv_kernel(mem_ref, g_ref, b_ref, w_ref, o_ref):
    mn = _layer_norm(mem_ref[...], g_ref[...], b_ref[...])
    o_ref[...] = _dot(mn.astype(_CDT), w_ref[...]).astype(o_ref.dtype)


def _mem_kv(mem2d, g, b, w_kv, *, tm):
    rows = mem2d.shape[0]
    vec = pl.BlockSpec((1, D_MODEL), lambda i: (0, 0))
    return pl.pallas_call(
        _mem_kv_kernel,
        out_shape=jax.ShapeDtypeStruct((rows, 2 * D_MODEL), _CDT),
        grid=(rows // tm,),
        in_specs=[pl.BlockSpec((tm, D_MODEL), lambda i: (i, 0)), vec, vec,
                  pl.BlockSpec((D_MODEL, 2 * D_MODEL), lambda i: (0, 0))],
        out_specs=pl.BlockSpec((tm, 2 * D_MODEL), lambda i: (i, 0)),
        compiler_params=_cparams(("parallel",)),
        name="mem_kv",
    )(mem2d, g, b, w_kv)


def _store_row_tiles(ref, val):
    n = val.shape[0]
    for c in range(ROW_TILE):
        ref[pl.ds(c, n, stride=ROW_TILE), :] = val[:, c * LANES:(c + 1) * LANES]


def _load_row_tiles(ref, n, pitch=ROW_TILE):
    return jnp.concatenate([ref[pl.ds(c, n, stride=pitch), :] for c in range(ROW_TILE)], axis=1)


_DONE = object()


def _interleave(stage_generators):
    live = list(stage_generators)
    while live:
        live = [g for g in live if next(g, _DONE) is not _DONE]


def _post_attn_kernel(x_ref, oa_ref, ob_ref, wout_ref, g1_ref, b1_ref, kv_ref, wq_ref, wo_ref,
                      g2_ref, b2_ref, wr_ref, br_ref, h2t_ref, idx_ref, wt_ref, cnt_ref, cnt_acc, *, tm, groups):
    @pl.when(pl.program_id(0) == 0)
    def _init_counts():
        cnt_acc[...] = jnp.zeros_like(cnt_acc)

    n = tm // groups
    lane = lax.broadcasted_iota(jnp.int32, (n, LANES), 1)
    lane_f = lane.astype(_F32)

    def row_group(g):
        rows = slice(g * n, (g + 1) * n)
        mix = _dot(oa_ref[rows, :], wout_ref[:DIFF_WIDTH, :]) + _dot(ob_ref[rows, :], wout_ref[DIFF_WIDTH:, :])
        yield
        h1 = _layer_norm(DEEPNORM_ALPHA * x_ref[rows, :] + mix, g1_ref[...], b1_ref[...])
        q = (_dot(h1.astype(_CDT), wq_ref[...]) * (MEM_HEAD_DIM ** -0.5)).astype(_CDT)
        yield
        heads = []
        for h in range(MEM_HEADS):
            c0 = h * MEM_HEAD_DIM
            kh = kv_ref[:, c0:c0 + MEM_HEAD_DIM]
            vh = kv_ref[:, D_MODEL + c0:D_MODEL + c0 + MEM_HEAD_DIM]
            s = _dot_nt(q[:, c0:c0 + MEM_HEAD_DIM], kh)
            p = jnp.exp(s - jnp.max(s, axis=1, keepdims=True))
            o = _dot(p.astype(_CDT), vh) / jnp.sum(p, axis=1, keepdims=True)
            heads.append(o.astype(_CDT))
        yield
        xatt = _dot(jnp.concatenate(heads, axis=1), wo_ref[...])
        yield
        h2 = _layer_norm(DEEPNORM_ALPHA * h1 + xatt, g2_ref[...], b2_ref[...])
        _store_row_tiles(h2t_ref.at[pl.ds(g * n * ROW_TILE, n * ROW_TILE)], h2)
        logits = _dot(h2.astype(_CDT), wr_ref[...]) + br_ref[...]
        yield
        vals, idxs = [], []
        work = logits
        for _ in range(TOP_K):
            m = jnp.max(work, axis=1, keepdims=True)
            ik = jnp.min(jnp.where(work == m, lane_f, float(LANES)), axis=1, keepdims=True)
            vals.append(m)
            idxs.append(ik)
            work = jnp.where(lane_f == ik, -jnp.inf, work)
        exps = [jnp.exp(v - vals[0]) for v in vals]
        denom = exps[0] + exps[1] + exps[2] + exps[3]
        idx_w = jnp.zeros((n, LANES), _F32)
        wt_w = jnp.zeros((n, LANES), _F32)
        for k in range(TOP_K):
            idx_w = jnp.where(lane == k, idxs[k], idx_w)
            wt_w = jnp.where(lane == k, exps[k] / denom, wt_w)
        idx_ref[rows, :] = idx_w[:, :TOP_K].astype(jnp.int32)
        wt_ref[rows, :] = wt_w[:, :TOP_K]
        chosen = functools.reduce(jnp.add, [jnp.where(lane_f == ik, 1.0, 0.0) for ik in idxs])
        cnt_acc[...] = cnt_acc[...] + jnp.sum(chosen, axis=0, keepdims=True)

    _interleave(row_group(g) for g in range(groups))
    cnt_ref[...] = cnt_acc[...].astype(jnp.int32)


def _post_attn(x2d, out_a, out_b, w_out, g1, b1, kv, w_q, w_o, g2, b2, w_r, b_r, seq, mem_len, *, tm):
    t = x2d.shape[0]
    per_b = seq // tm
    full = lambda shape: pl.BlockSpec(shape, lambda i: (0,) * len(shape))
    vec = full((1, D_MODEL))
    return pl.pallas_call(
        functools.partial(_post_attn_kernel, tm=tm, groups=max(1, tm // MOBA_BLOCK)),
        out_shape=(jax.ShapeDtypeStruct((t * ROW_TILE, LANES), _F32),
                   jax.ShapeDtypeStruct((t, TOP_K), jnp.int32),
                   jax.ShapeDtypeStruct((t, TOP_K), _F32),
                   jax.ShapeDtypeStruct((1, LANES), jnp.int32)),
        grid=(t // tm,),
        in_specs=[pl.BlockSpec((tm, D_MODEL), lambda i: (i, 0)),
                  pl.BlockSpec((tm, DIFF_WIDTH), lambda i: (i, 0)),
                  pl.BlockSpec((tm, MOBA_WIDTH), lambda i: (i, 0)),
                  full((D_MODEL, D_MODEL)), vec, vec,
                  pl.BlockSpec((mem_len, 2 * D_MODEL), lambda i: (i // per_b, 0)),
                  full((D_MODEL, D_MODEL)), full((D_MODEL, D_MODEL)), vec, vec,
                  full((D_MODEL, LANES)), full((1, LANES))],
        out_specs=(pl.BlockSpec((tm * ROW_TILE, LANES), lambda i: (i, 0)),
                   pl.BlockSpec((tm, TOP_K), lambda i: (i, 0)),
                   pl.BlockSpec((tm, TOP_K), lambda i: (i, 0)),
                   pl.BlockSpec((1, LANES), lambda i: (0, 0))),
        scratch_shapes=[pltpu.VMEM((1, LANES), _F32)],
        compiler_params=_cparams(("arbitrary",)),
        name="post_attn",
    )(x2d, out_a, out_b, w_out, g1, b1, kv, w_q, w_o, g2, b2, w_r, b_r)


def _rank_kernel(idx_ref, off_ref, pos_ref, next_ref, *, tm):
    @pl.when(pl.program_id(0) == 0)
    def _init():
        next_ref[...] = off_ref[...]

    idx = idx_ref[...]
    lane = lax.broadcasted_iota(jnp.int32, (tm, LANES), 1)
    hits = [lane == idx[:, k:k + 1] for k in range(TOP_K)]
    sel = jnp.zeros((tm, LANES), _F32)
    for hit in hits:
        sel = sel + jnp.where(hit, 1.0, 0.0)
    row = lax.broadcasted_iota(jnp.int32, (tm, tm), 0)
    col = lax.broadcasted_iota(jnp.int32, (tm, tm), 1)
    earlier = jnp.where(col < row, 1.0, 0.0).astype(jnp.bfloat16)
    dense = _dot(earlier, sel.astype(jnp.bfloat16)) + next_ref[...]
    pos_w = jnp.zeros((tm, LANES), _F32)
    for k, hit in enumerate(hits):
        pk = jnp.sum(jnp.where(hit, dense, 0.0), axis=1, keepdims=True)
        pos_w = jnp.where(lane == k, pk, pos_w)
    pos_ref[...] = pos_w[:, :TOP_K].astype(jnp.int32)
    next_ref[...] = next_ref[...] + jnp.sum(sel, axis=0, keepdims=True)


def _rank(idx, offsets, *, tm):
    t = idx.shape[0]
    return pl.pallas_call(
        functools.partial(_rank_kernel, tm=tm),
        out_shape=jax.ShapeDtypeStruct((t, TOP_K), jnp.int32),
        grid=(t // tm,),
        in_specs=[pl.BlockSpec((tm, TOP_K), lambda i: (i, 0)), pl.BlockSpec((1, LANES), lambda i: (0, 0))],
        out_specs=pl.BlockSpec((tm, TOP_K), lambda i: (i, 0)),
        scratch_shapes=[pltpu.VMEM((1, LANES), _F32)],
        compiler_params=_cparams(("arbitrary",)),
        name="rank",
    )(idx, offsets)


_DMA_UNROLL = 8
_DMA_THREADS = 2


def _for_each_row(n_rows, fn):
    tokens_per_group = _DMA_UNROLL // TOP_K

    def group(g, c):
        for u in range(_DMA_UNROLL):
            fn(g * _DMA_UNROLL + u, g * tokens_per_group + u // TOP_K, u % TOP_K)
        return c

    lax.fori_loop(0, n_rows // _DMA_UNROLL, group, 0)


def _dispatch_kernel(meta_ref, h_ref, pos_ref, xs_ref, zero_ref, sem, zsem, *, tm):
    n_rows = tm * TOP_K

    def zero_copy(r):
        return pltpu.make_async_copy(zero_ref, xs_ref.at[r], zsem)

    @pl.when(pl.program_id(0) == 0)
    def _zero_fill():
        zero_ref[...] = jnp.zeros_like(zero_ref)

        def per_expert(e, c):
            lo = meta_ref[e] + meta_ref[N_EXPERTS + e]
            hi = meta_ref[e] + meta_ref[2 * N_EXPERTS + e]
            lax.fori_loop(lo, hi, lambda r, c2: (zero_copy(r).start(), c2)[1], 0)
            lax.fori_loop(lo, hi, lambda r, c2: (zero_copy(r).wait(), c2)[1], 0)
            return c

        lax.fori_loop(0, N_EXPERTS, per_expert, 0)

    def row_copy(r, t, k):
        return pltpu.make_async_copy(h_ref.at[t], xs_ref.at[pos_ref[r]], sem)

    _for_each_row(n_rows, lambda r, t, k: row_copy(r, t, k).start(priority=k % _DMA_THREADS))
    _for_each_row(n_rows, lambda r, t, k: row_copy(r, t, k).wait())


def _dispatch(meta, h2t, pos_flat, n_sorted_rows, *, tm):
    t = h2t.shape[0]
    return pl.pallas_call(
        functools.partial(_dispatch_kernel, tm=tm),
        out_shape=jax.ShapeDtypeStruct((n_sorted_rows, ROW_TILE, LANES), _F32),
        grid_spec=pltpu.PrefetchScalarGridSpec(
            num_scalar_prefetch=1,
            grid=(t // tm,),
            in_specs=[pl.BlockSpec((tm, ROW_TILE, LANES), lambda i, meta: (i, 0, 0)),
                      pl.BlockSpec((tm * TOP_K,), lambda i, meta: (i,), memory_space=pltpu.SMEM)],
            out_specs=pl.BlockSpec(memory_space=pl.ANY),
            scratch_shapes=[pltpu.VMEM((ROW_TILE, LANES), _F32),
                            pltpu.SemaphoreType.DMA, pltpu.SemaphoreType.DMA]),
        compiler_params=_cparams(("arbitrary",)),
        name="dispatch",
    )(meta, h2t, pos_flat)


_DEINT = 2 * LANES


def _deinterleave_matrix():
    k = jnp.arange(_DEINT)[:, None]
    n = jnp.arange(_DEINT)[None, :]
    src = jnp.where(n < LANES, 2 * n, 2 * (n - LANES) + 1)
    return (k == src).astype(_CDT)


def _experts_kernel(te_ref, nv_ref, xs_ref, perm_ref, w1_ref, b1g_ref, b1l_ref, w2_ref, b2_ref, y_ref,
                    wg_ref, wl_ref, w2c_ref, *, tm):
    i = pl.program_id(0)
    live = i < nv_ref[0]
    new_expert = (i == 0) | (te_ref[i] != te_ref[jnp.maximum(i - 1, 0)])

    @pl.when(live & new_expert)
    def _stage_weights():
        for c in range(2 * D_FF // _DEINT):
            both = _dot(w1_ref[0, :, c * _DEINT:(c + 1) * _DEINT].astype(_CDT), perm_ref[...])
            wg_ref[:, c * LANES:(c + 1) * LANES] = both[:, :LANES].astype(_CDT)
            wl_ref[:, c * LANES:(c + 1) * LANES] = both[:, LANES:].astype(_CDT)
        w2c_ref[...] = w2_ref[0].astype(_CDT)

    @pl.when(live)
    def _mlp():
        x = _load_row_tiles(xs_ref, tm).astype(_CDT)
        glu = jnp.minimum(_dot(x, wg_ref[...]) + b1g_ref[0], SWIGLU_LIMIT)
        lin = jnp.clip(_dot(x, wl_ref[...]) + b1l_ref[0], -SWIGLU_LIMIT, SWIGLU_LIMIT)
        act = glu * jax.nn.sigmoid(SWIGLU_ALPHA * glu) * (lin + 1.0)
        _store_row_tiles(y_ref, _dot(act.astype(_CDT), w2c_ref[...]) + b2_ref[0])

    @pl.when(jnp.logical_not(live))
    def _unused_tile():
        y_ref[...] = jnp.zeros_like(y_ref)


def _experts(tile_expert, n_valid, xs, w1, b1g, b1l, w2, b2, *, tm):
    n_tiles = xs.shape[0] // (tm * ROW_TILE)
    by_expert = lambda shape: pl.BlockSpec((1,) + shape, lambda i, te, nv: (te[i], 0, 0))
    return pl.pallas_call(
        functools.partial(_experts_kernel, tm=tm),
        out_shape=jax.ShapeDtypeStruct(xs.shape, _F32),
        grid_spec=pltpu.PrefetchScalarGridSpec(
            num_scalar_prefetch=2,
            grid=(n_tiles,),
            in_specs=[pl.BlockSpec((tm * ROW_TILE, LANES), lambda i, te, nv: (jnp.minimum(i, nv[0] - 1), 0)),
                      pl.BlockSpec((_DEINT, _DEINT), lambda i, te, nv: (0, 0)),
                      by_expert((D_MODEL, 2 * D_FF)),
                      by_expert((1, D_FF)), by_expert((1, D_FF)),
                      by_expert((D_FF, D_MODEL)), by_expert((1, D_MODEL))],
            out_specs=pl.BlockSpec((tm * ROW_TILE, LANES), lambda i, te, nv: (i, 0)),
            scratch_shapes=[pltpu.VMEM((D_MODEL, D_FF), _CDT), pltpu.VMEM((D_MODEL, D_FF), _CDT),
                            pltpu.VMEM((D_FF, D_MODEL), _CDT)]),
        compiler_params=_cparams(("arbitrary",)),
        name="experts",
    )(tile_expert, n_valid, xs, _deinterleave_matrix(), w1, b1g, b1l, w2, b2)


_GATHER_PITCH = 9


def _combine_kernel(pos_ref, nxt_ref, wt_ref, h2t_ref, g_ref, b_ref, ys_ref, o_ref, buf_ref, sem, *, tm):
    n_rows = tm * TOP_K
    i = pl.program_id(0)
    slot = i & 1

    def row_copy(p_ref, s, r, t, k):
        tile_rows = pl.ds(t * _GATHER_PITCH, ROW_TILE)
        return pltpu.make_async_copy(ys_ref.at[p_ref[r]], buf_ref.at[s, k, tile_rows], sem.at[s])

    @pl.when(i == 0)
    def _first():
        _for_each_row(n_rows, lambda r, t, k: row_copy(pos_ref, 0, r, t, k).start(priority=k % _DMA_THREADS))

    for r in range(n_rows):
        row_copy(pos_ref, slot, r, r // TOP_K, r % TOP_K).wait()
    for r in range(n_rows):
        row_copy(nxt_ref, 1 - slot, r, r // TOP_K, r % TOP_K).start(priority=r % _DMA_THREADS)
    wt = wt_ref[...]
    ffn = wt[:, 0:1] * _load_row_tiles(buf_ref.at[slot, 0], tm, _GATHER_PITCH)
    for k in range(1, TOP_K):
        ffn = ffn + wt[:, k:k + 1] * _load_row_tiles(buf_ref.at[slot, k], tm, _GATHER_PITCH)
    o_ref[...] = _layer_norm(DEEPNORM_ALPHA * _load_row_tiles(h2t_ref, tm) + ffn, g_ref[...], b_ref[...])

    @pl.when(i + 1 == pl.num_programs(0))
    def _drain():
        _for_each_row(n_rows, lambda r, t, k: row_copy(nxt_ref, 1 - slot, r, t, k).wait())


def _combine(pos_flat, wts, h2t, g3, b3, ys, *, tm):
    t = h2t.shape[0] // ROW_TILE
    n_steps = t // tm
    vec = pl.BlockSpec((1, D_MODEL), lambda i: (0, 0))
    return pl.pallas_call(
        functools.partial(_combine_kernel, tm=tm),
        out_shape=jax.ShapeDtypeStruct((t, D_MODEL), _F32),
        grid=(n_steps,),
        in_specs=[pl.BlockSpec((tm * TOP_K,), lambda i: (i,), memory_space=pltpu.SMEM),
                  pl.BlockSpec((tm * TOP_K,), lambda i: (jnp.minimum(i + 1, n_steps - 1),), memory_space=pltpu.SMEM),
                  pl.BlockSpec((tm, TOP_K), lambda i: (i, 0)),
                  pl.BlockSpec((tm * ROW_TILE, LANES), lambda i: (i, 0)), vec, vec,
                  pl.BlockSpec(memory_space=pl.ANY)],
        out_specs=pl.BlockSpec((tm, D_MODEL), lambda i: (i, 0)),
        scratch_shapes=[pltpu.VMEM((2, TOP_K, tm * _GATHER_PITCH, LANES), _F32), pltpu.SemaphoreType.DMA((2,))],
        compiler_params=_cparams(("arbitrary",)),
        name="combine",
    )(pos_flat, pos_flat, wts, h2t, g3, b3, ys)


def _tiles(seq):
    big = 512 if seq % 512 == 0 else MOBA_BLOCK
    post = 1024 if seq % 1024 == 0 else big
    return dict(inproj=big, attn_q=MOBA_BLOCK, post=post, rank=big, dispatch=big, expert=512, combine=big)


def kernel(x, mem, w_in, diff_lambda_q1, diff_lambda_k1, diff_lambda_q2, diff_lambda_k2, diff_subln_g,
           w_mix_out, ln1_g, ln1_b, mem_ln_g, mem_ln_b, w_mem_q, w_mem_kv, w_mem_o, ln2_g, ln2_b, w_router,
           b_router, w_mlp1, b_mlp1, w_mlp2, b_mlp2, ln3_g, ln3_b):
    batch, seq, d = x.shape
    mem_len = mem.shape[1]
    assert d == D_MODEL and seq % MOBA_BLOCK == 0 and w_in.shape[0] == 1
    t = batch * seq
    ts = _tiles(seq)
    row = lambda v: v.reshape(1, -1).astype(_F32)

    col = jnp.arange(PROJ_WIDTH)
    is_q = (col < DIFF_WIDTH) | ((col >= 3 * DIFF_WIDTH) & (col < 3 * DIFF_WIDTH + MOBA_WIDTH))
    w_in_c = (w_in[0] * jnp.where(is_q, HEAD_DIM ** -0.5 * math.log2(math.e), 1.0)[None, :]).astype(_CDT)
    lam_vecs = jnp.stack([diff_lambda_q1[0], diff_lambda_k1[0], diff_lambda_q2[0], diff_lambda_k2[0]]).astype(_F32)
    w_r = jnp.zeros((D_MODEL, LANES), _F32).at[:, :N_EXPERTS].set(w_router[0]).astype(_CDT)
    b_r = jnp.full((1, LANES), _NEG, _F32).at[0, :N_EXPERTS].set(b_router[0])
    b1g = b_mlp1[0][:, None, 0::2].astype(_F32)
    b1l = b_mlp1[0][:, None, 1::2].astype(_F32)
    b2 = b_mlp2[0][:, None, :].astype(_F32)

    x2d = x.reshape(t, d)
    proj, kmean = _inproj(x2d, w_in_c, seq, tm=ts["inproj"])
    kmean = kmean.reshape(batch, seq // MOBA_BLOCK, MOBA_WIDTH)
    out_a = _diff_attn(proj, lam_vecs, row(diff_subln_g[0]), batch, seq, tq=ts["attn_q"])
    out_b = _moba_attn(proj, kmean, batch, seq)
    kv = _mem_kv(mem.reshape(batch * mem_len, d), row(mem_ln_g), row(mem_ln_b), w_mem_kv[0].astype(_CDT),
                 tm=mem_len)
    h2t, top_idx, top_w, counts = _post_attn(
        x2d, out_a, out_b, w_mix_out[0].astype(_CDT), row(ln1_g[0]), row(ln1_b[0]), kv,
        w_mem_q[0].astype(_CDT), w_mem_o[0].astype(_CDT), row(ln2_g[0]), row(ln2_b[0]), w_r, b_r,
        seq, mem_len, tm=ts["post"])

    tmx = ts["expert"]
    cnt = counts[0, :N_EXPERTS]
    padded = (cnt + tmx - 1) // tmx * tmx
    ends = jnp.cumsum(padded)
    offsets = ends - padded
    meta = jnp.concatenate([offsets, cnt, padded]).astype(jnp.int32)
    n_sorted_rows = t * TOP_K + N_EXPERTS * tmx
    n_tiles = n_sorted_rows // tmx
    tile_start = jnp.arange(n_tiles, dtype=jnp.int32) * tmx
    tile_expert = jnp.minimum(jnp.sum(tile_start[:, None] >= ends[None, :], axis=1), N_EXPERTS - 1).astype(jnp.int32)
    n_valid = (ends[-1:] // tmx).astype(jnp.int32)
    offsets_row = jnp.zeros((1, LANES), _F32).at[0, :N_EXPERTS].set(offsets.astype(_F32))
    pos = _rank(top_idx, offsets_row, tm=ts["rank"]).reshape(-1)

    xs = _dispatch(meta, h2t.reshape(t, ROW_TILE, LANES), pos, n_sorted_rows, tm=ts["dispatch"])
    ys = _experts(tile_expert, n_valid, xs.reshape(n_sorted_rows * ROW_TILE, LANES), w_mlp1[0], b1g, b1l,
                  w_mlp2[0], b2, tm=tmx)
    out = _combine(pos, top_w, h2t, row(ln3_g[0]), row(ln3_b[0]),
                   ys.reshape(n_sorted_rows, ROW_TILE, LANES), tm=ts["combine"])
    return out.reshape(batch, seq, d)
```

```python
import functools
import math

import jax
import jax.numpy as jnp
from jax import lax
from jax.experimental import pallas as pl
from jax.experimental.pallas import tpu as pltpu

D_MODEL = 1024
DIFF_HEADS = 4
HEAD_DIM = 64
DIFF_WIDTH = DIFF_HEADS * 2 * HEAD_DIM
MOBA_HEADS = 8
MOBA_WIDTH = MOBA_HEADS * HEAD_DIM
MOBA_BLOCK = 256
MOBA_TOPK = 3
ROPE_THETA = 500000.0
ROT_DIM = HEAD_DIM // 4
MEM_HEADS = 4
MEM_HEAD_DIM = D_MODEL // MEM_HEADS
N_EXPERTS = 32
TOP_K = 4
D_FF = D_MODEL
SWIGLU_ALPHA = 1.702
SWIGLU_LIMIT = 7.0
LN_EPS = 1e-5
RMS_EPS = 1e-5
DEEPNORM_ALPHA = 2.0 ** 0.25
LAMBDA_INIT = 0.8 - 0.6 * math.exp(0.0)

LANES = 128
ROW_TILE = 8
ROW_PITCH = ROW_TILE + 1
V7X_VMEM_LIMIT = 56 * 1024 * 1024

_CDT = jnp.bfloat16
_F32 = jnp.float32
_NEG = -1e30

_CB_DQ, _CB_DK, _CB_DV = 0, 4, 8
_CB_MQ, _CB_MK, _CB_MV = 12, 16, 20
PROJ_WIDTH = 3 * (DIFF_WIDTH + MOBA_WIDTH)


def _cparams(sem, vmem=V7X_VMEM_LIMIT):
    return pltpu.CompilerParams(dimension_semantics=sem, vmem_limit_bytes=vmem)


def _dot(a, b):
    return jnp.dot(a, b, preferred_element_type=_F32)


def _dot_nt(a, b):
    return lax.dot_general(a, b, (((1,), (1,)), ((), ())), preferred_element_type=_F32)


def _layer_norm(x, g, b):
    mu = jnp.mean(x, axis=-1, keepdims=True)
    xc = x - mu
    var = jnp.mean(xc * xc, axis=-1, keepdims=True)
    return xc * lax.rsqrt(var + LN_EPS) * g + b


_ROPE_BLOCKS = tuple(range(_CB_DQ, _CB_DV)) + tuple(range(_CB_MQ, _CB_MV))


def _inproj_kernel(x_ref, w_ref, cos_ref, sa_ref, sb_ref, o_ref, km_ref, *, tm):
    x = x_ref[...].astype(_CDT)
    cos, sa, sb = cos_ref[...], sa_ref[...], sb_ref[...]
    seg_w = 4 * LANES
    for seg in range(PROJ_WIDTH // seg_w):
        pseg = _dot(x, w_ref[:, seg * seg_w:(seg + 1) * seg_w])
        for c in range(4):
            cb = seg * 4 + c
            cols = slice(cb * LANES, (cb + 1) * LANES)
            p = pseg[:, c * LANES:(c + 1) * LANES]
            if cb in _ROPE_BLOCKS:
                p = p * cos + pltpu.roll(p, 8, 1) * sa + pltpu.roll(p, LANES - 8, 1) * sb
            o_ref[:, cols] = p.astype(o_ref.dtype)
            if _CB_MK <= cb < _CB_MV:
                kc = slice((cb - _CB_MK) * LANES, (cb - _CB_MK + 1) * LANES)
                for blk in range(tm // MOBA_BLOCK):
                    rows = p[blk * MOBA_BLOCK:(blk + 1) * MOBA_BLOCK, :]
                    km_ref[blk, :, kc] = jnp.sum(rows, axis=0, keepdims=True) * (1.0 / MOBA_BLOCK)


def _rope_tables(seq):
    half = ROT_DIM // 2
    inv_freq = ROPE_THETA ** (-jnp.arange(0, ROT_DIM, 2, dtype=_F32) / ROT_DIM)
    ang = jnp.arange(seq, dtype=_F32)[:, None] * inv_freq[None, :]
    cos, sin = jnp.cos(ang), jnp.sin(ang)
    lane = jnp.arange(LANES) % HEAD_DIM
    first, second = lane < half, (lane >= half) & (lane < ROT_DIM)
    idx = jnp.where(first, lane, jnp.where(second, lane - half, 0))
    cos_t = jnp.where((first | second)[None, :], cos[:, idx], 1.0)
    sa_t = jnp.where(second[None, :], sin[:, idx], 0.0)
    sb_t = jnp.where(first[None, :], -sin[:, idx], 0.0)
    return cos_t, sa_t, sb_t


def _inproj(x2d, w_in, seq, *, tm):
    t = x2d.shape[0]
    cos_t, sa_t, sb_t = _rope_tables(seq)
    n_pos = seq // tm
    tab_spec = pl.BlockSpec((tm, LANES), lambda i: (i % n_pos, 0))
    return pl.pallas_call(
        functools.partial(_inproj_kernel, tm=tm),
        out_shape=(jax.ShapeDtypeStruct((t, PROJ_WIDTH), _CDT),
                   jax.ShapeDtypeStruct((t // MOBA_BLOCK, 1, MOBA_WIDTH), _F32)),
        grid=(t // tm,),
        in_specs=[pl.BlockSpec((tm, D_MODEL), lambda i: (i, 0)),
                  pl.BlockSpec((D_MODEL, PROJ_WIDTH), lambda i: (0, 0)),
                  tab_spec, tab_spec, tab_spec],
        out_specs=(pl.BlockSpec((tm, PROJ_WIDTH), lambda i: (i, 0)),
                   pl.BlockSpec((tm // MOBA_BLOCK, 1, MOBA_WIDTH), lambda i: (i, 0, 0))),
        compiler_params=_cparams(("parallel",)),
        name="inproj",
    )(x2d, w_in, cos_t, sa_t, sb_t)


def _exp2_parts(parts):
    chunks = [s[:, c:c + LANES] for s in parts for c in range(0, s.shape[1], LANES)]
    m = jnp.max(functools.reduce(jnp.maximum, chunks), axis=1, keepdims=True)
    return [jnp.exp2(s - m).astype(_CDT) for s in parts]


_SCORES_AHEAD = 1


def _causal_mask(tq):
    row = lax.broadcasted_iota(jnp.int32, (tq, tq), 0)
    col = lax.broadcasted_iota(jnp.int32, (tq, tq), 1)
    return col <= row


def _diff_attn_kernel(q_ref, k_ref, v_ref, lam_ref, g_ref, o_ref, *, seq, tq):
    lane = lax.broadcasted_iota(jnp.int32, (tq, LANES), 1)
    mask = _causal_mask(tq)
    lv = lam_ref[...]
    lam = (jnp.exp(jnp.sum(lv[0:1] * lv[1:2], axis=1, keepdims=True))
           - jnp.exp(jnp.sum(lv[2:3] * lv[3:4], axis=1, keepdims=True)) + LAMBDA_INIT)
    gain = g_ref[...] * (1.0 - LAMBDA_INIT)

    def with_ones(v):
        return jnp.concatenate([v, jnp.ones_like(v)], axis=1)

    def scores(i):
        rows = slice(i * tq, (i + 1) * tq)
        q = q_ref[rows, :]
        zero = jnp.zeros_like(q)
        q1 = jnp.where(lane < HEAD_DIM, q, zero)
        q2 = jnp.where(lane >= HEAD_DIM, q, zero)
        kd = k_ref[rows, :]
        s1 = [jnp.where(mask, _dot_nt(q1, kd), _NEG)]
        s2 = [jnp.where(mask, _dot_nt(q2, kd), _NEG)]
        if i:
            kp = k_ref[:i * tq, :]
            s1.append(_dot_nt(q1, kp))
            s2.append(_dot_nt(q2, kp))
        return s1, s2

    def finish(i, s1, s2):
        rows = slice(i * tq, (i + 1) * tq)
        v_all = with_ones(v_ref[:(i + 1) * tq, :])
        nums = []
        for parts in (s1, s2):
            es = _exp2_parts(parts)
            nl = _dot(jnp.concatenate(es[1:] + es[:1], axis=1), v_all)
            nums.append(nl[:, :LANES] / nl[:, LANES:LANES + 1])
        o = nums[0] - lam * nums[1]
        o = o * lax.rsqrt(jnp.mean(o * o, axis=1, keepdims=True) + RMS_EPS) * gain
        o_ref[rows, :] = o.astype(o_ref.dtype)

    n_tiles = seq // tq
    pending = [scores(i) for i in range(min(_SCORES_AHEAD, n_tiles))]
    for i in range(n_tiles):
        if i + _SCORES_AHEAD < n_tiles:
            pending.append(scores(i + _SCORES_AHEAD))
        finish(i, *pending.pop(0))


def _diff_attn(proj, lam_vecs, subln_g, batch, seq, *, tq):
    t = batch * seq
    blk = lambda cb: pl.BlockSpec((seq, LANES), lambda b, h: (b, cb + h))
    return pl.pallas_call(
        functools.partial(_diff_attn_kernel, seq=seq, tq=tq),
        out_shape=jax.ShapeDtypeStruct((t, DIFF_WIDTH), _CDT),
        grid=(batch, DIFF_HEADS),
        in_specs=[blk(_CB_DQ), blk(_CB_DK), blk(_CB_DV),
                  pl.BlockSpec((4, HEAD_DIM), lambda b, h: (0, 0)),
                  pl.BlockSpec((1, 2 * HEAD_DIM), lambda b, h: (0, 0))],
        out_specs=pl.BlockSpec((seq, LANES), lambda b, h: (b, h)),
        compiler_params=_cparams(("parallel", "parallel")),
        name="diff_attn",
    )(proj, proj, proj, lam_vecs, subln_g)


def _moba_selection(gate, n_past):
    blk = lax.broadcasted_iota(jnp.int32, gate.shape, 0)
    rank = jnp.zeros(gate.shape, _F32)
    for m_blk in range(n_past):
        gm = gate[m_blk:m_blk + 1, :]
        beats = (gm > gate) | ((gm == gate) & (m_blk < blk))
        rank = rank + jnp.where(beats, 1.0, 0.0)
    return jnp.where((blk < n_past) & (rank < MOBA_TOPK), 1.0, 0.0)


def _moba_kernel(q_ref, k_ref, v_ref, km_ref, o_ref, *, nb):
    tq = MOBA_BLOCK
    lane = lax.broadcasted_iota(jnp.int32, (tq, LANES), 1)
    mask = _causal_mask(tq)
    km = km_ref[...].astype(_CDT)
    in_head = [(lane >= hh * HEAD_DIM) & (lane < (hh + 1) * HEAD_DIM) for hh in range(2)]
    seq_lane = lax.broadcasted_iota(jnp.int32, (nb * tq, LANES), 1)
    v_ones = [jnp.where((seq_lane >= hh * HEAD_DIM) & (seq_lane < (hh + 1) * HEAD_DIM), v_ref[...],
                        jnp.ones((nb * tq, LANES), _CDT)) for hh in range(2)]

    def scores(i, hh):
        rows = slice(i * tq, (i + 1) * tq)
        q = q_ref[rows, :]
        qh = jnp.where(in_head[hh], q, jnp.zeros_like(q))
        parts = [jnp.where(mask, _dot_nt(qh, k_ref[rows, :]), _NEG)]
        gated = i > MOBA_TOPK
        if gated:
            sel = _moba_selection(_dot_nt(km, qh), i)
            sel_q = jnp.concatenate([sel, jnp.zeros((LANES - nb, tq), _F32)], axis=0).T
        for j in range(i):
            s = _dot_nt(qh, k_ref[j * tq:(j + 1) * tq, :])
            parts.append(jnp.where(sel_q[:, j:j + 1] > 0.5, s, _NEG) if gated else s)
        return parts

    def finish(i, hh, parts):
        es = _exp2_parts(parts)
        acc = _dot(jnp.concatenate(es[1:] + es[:1], axis=1), v_ones[hh][:(i + 1) * tq, :])
        sum_lane = (1 - hh) * HEAD_DIM
        return acc / acc[:, sum_lane:sum_lane + 1]

    units = [(i, hh) for i in range(nb) for hh in range(2)]
    pending = [scores(*u) for u in units[:_SCORES_AHEAD]]
    outs = {}
    for n, (i, hh) in enumerate(units):
        if n + _SCORES_AHEAD < len(units):
            pending.append(scores(*units[n + _SCORES_AHEAD]))
        outs[hh] = finish(i, hh, pending.pop(0))
        if hh == 1:
            o_ref[i * tq:(i + 1) * tq, :] = jnp.where(in_head[0], outs[0], outs[1]).astype(o_ref.dtype)


def _moba_attn(proj, kmean, batch, seq):
    nb = seq // MOBA_BLOCK
    t = batch * seq
    blk = lambda cb: pl.BlockSpec((seq, LANES), lambda b, p: (b, cb + p))
    return pl.pallas_call(
        functools.partial(_moba_kernel, nb=nb),
        out_shape=jax.ShapeDtypeStruct((t, MOBA_WIDTH), _CDT),
        grid=(batch, MOBA_HEADS // 2),
        in_specs=[blk(_CB_MQ), blk(_CB_MK), blk(_CB_MV),
                  pl.BlockSpec((None, nb, LANES), lambda b, p: (b, 0, p))],
        out_specs=pl.BlockSpec((seq, LANES), lambda b, p: (b, p)),
        compiler_params=_cparams(("parallel", "parallel")),
        name="moba_attn",
    )(proj, proj, proj, kmean)


def _mem_kv_kernel(mem_ref, g_ref, b_ref, w_ref, o_ref):
    mn = _layer_norm(mem_ref[...], g_ref[...], b_ref[...])
    o_ref[...] = _dot(mn.astype(_CDT), w_ref[...]).astype(o_ref.dtype)


def _mem_kv(mem2d, g, b, w_kv, *, tm):
    rows = mem2d.shape[0]
    vec = pl.BlockSpec((1, D_MODEL), lambda i: (0, 0))
    return pl.pallas_call(
        _mem_kv_kernel,
        out_shape=jax.ShapeDtypeStruct((rows, 2 * D_MODEL), _CDT),
        grid=(rows // tm,),
        in_specs=[pl.BlockSpec((tm, D_MODEL), lambda i: (i, 0)), vec, vec,
                  pl.BlockSpec((D_MODEL, 2 * D_MODEL), lambda i: (0, 0))],
        out_specs=pl.BlockSpec((tm, 2 * D_MODEL), lambda i: (i, 0)),
        compiler_params=_cparams(("parallel",)),
        name="mem_kv",
    )(mem2d, g, b, w_kv)


def _store_row_tiles(ref, val):
    n = val.shape[0]
    for c in range(ROW_TILE):
        ref[pl.ds(c, n, stride=ROW_PITCH), :] = val[:, c * LANES:(c + 1) * LANES]
    ref[pl.ds(ROW_TILE, n, stride=ROW_PITCH), :] = jnp.zeros((n, LANES), val.dtype)


def _load_row_tiles(ref, n):
    return jnp.concatenate([ref[pl.ds(c, n, stride=ROW_PITCH), :] for c in range(ROW_TILE)], axis=1)


def _slab(ref, i):
    return ref.at[pl.ds(i * ROW_PITCH, ROW_PITCH)]


_DONE = object()


def _interleave(stage_generators):
    live = list(stage_generators)
    while live:
        live = [g for g in live if next(g, _DONE) is not _DONE]


def _post_attn_kernel(x_ref, oa_ref, ob_ref, wout_ref, g1_ref, b1_ref, kv_ref, wq_ref, wo_ref,
                      g2_ref, b2_ref, wr_ref, br_ref, h2t_ref, idx_ref, wt_ref, cnt_ref, cnt_acc, *, tm, groups):
    @pl.when(pl.program_id(0) == 0)
    def _init_counts():
        cnt_acc[...] = jnp.zeros_like(cnt_acc)

    n = tm // groups
    lane = lax.broadcasted_iota(jnp.int32, (n, LANES), 1)
    lane_f = lane.astype(_F32)

    def row_group(g):
        rows = slice(g * n, (g + 1) * n)
        mix = _dot(oa_ref[rows, :], wout_ref[:DIFF_WIDTH, :]) + _dot(ob_ref[rows, :], wout_ref[DIFF_WIDTH:, :])
        yield
        h1 = _layer_norm(DEEPNORM_ALPHA * x_ref[rows, :] + mix, g1_ref[...], b1_ref[...])
        q = (_dot(h1.astype(_CDT), wq_ref[...]) * (MEM_HEAD_DIM ** -0.5)).astype(_CDT)
        yield
        heads = []
        for h in range(MEM_HEADS):
            c0 = h * MEM_HEAD_DIM
            kh = kv_ref[:, c0:c0 + MEM_HEAD_DIM]
            vh = kv_ref[:, D_MODEL + c0:D_MODEL + c0 + MEM_HEAD_DIM]
            s = _dot_nt(q[:, c0:c0 + MEM_HEAD_DIM], kh)
            p = jnp.exp(s - jnp.max(s, axis=1, keepdims=True))
            o = _dot(p.astype(_CDT), vh) / jnp.sum(p, axis=1, keepdims=True)
            heads.append(o.astype(_CDT))
        yield
        xatt = _dot(jnp.concatenate(heads, axis=1), wo_ref[...])
        yield
        h2 = _layer_norm(DEEPNORM_ALPHA * h1 + xatt, g2_ref[...], b2_ref[...])
        _store_row_tiles(h2t_ref.at[pl.ds(g * n * ROW_PITCH, n * ROW_PITCH)], h2)
        logits = _dot(h2.astype(_CDT), wr_ref[...]) + br_ref[...]
        yield
        vals, idxs = [], []
        work = logits
        for _ in range(TOP_K):
            m = jnp.max(work, axis=1, keepdims=True)
            ik = jnp.min(jnp.where(work == m, lane_f, float(LANES)), axis=1, keepdims=True)
            vals.append(m)
            idxs.append(ik)
            work = jnp.where(lane_f == ik, -jnp.inf, work)
        exps = [jnp.exp(v - vals[0]) for v in vals]
        denom = exps[0] + exps[1] + exps[2] + exps[3]
        idx_w = jnp.zeros((n, LANES), _F32)
        wt_w = jnp.zeros((n, LANES), _F32)
        for k in range(TOP_K):
            idx_w = jnp.where(lane == k, idxs[k], idx_w)
            wt_w = jnp.where(lane == k, exps[k] / denom, wt_w)
        idx_ref[rows, :] = idx_w[:, :TOP_K].astype(jnp.int32)
        wt_ref[rows, :] = wt_w[:, :TOP_K]
        chosen = functools.reduce(jnp.add, [jnp.where(lane_f == ik, 1.0, 0.0) for ik in idxs])
        cnt_acc[...] = cnt_acc[...] + jnp.sum(chosen, axis=0, keepdims=True)

    _interleave(row_group(g) for g in range(groups))
    cnt_ref[...] = cnt_acc[...].astype(jnp.int32)


def _post_attn(x2d, out_a, out_b, w_out, g1, b1, kv, w_q, w_o, g2, b2, w_r, b_r, seq, mem_len, *, tm):
    t = x2d.shape[0]
    per_b = seq // tm
    full = lambda shape: pl.BlockSpec(shape, lambda i: (0,) * len(shape))
    vec = full((1, D_MODEL))
    return pl.pallas_call(
        functools.partial(_post_attn_kernel, tm=tm, groups=max(1, tm // MOBA_BLOCK)),
        out_shape=(jax.ShapeDtypeStruct((t * ROW_PITCH, LANES), _F32),
                   jax.ShapeDtypeStruct((t, TOP_K), jnp.int32),
                   jax.ShapeDtypeStruct((t, TOP_K), _F32),
                   jax.ShapeDtypeStruct((1, LANES), jnp.int32)),
        grid=(t // tm,),
        in_specs=[pl.BlockSpec((tm, D_MODEL), lambda i: (i, 0)),
                  pl.BlockSpec((tm, DIFF_WIDTH), lambda i: (i, 0)),
                  pl.BlockSpec((tm, MOBA_WIDTH), lambda i: (i, 0)),
                  full((D_MODEL, D_MODEL)), vec, vec,
                  pl.BlockSpec((mem_len, 2 * D_MODEL), lambda i: (i // per_b, 0)),
                  full((D_MODEL, D_MODEL)), full((D_MODEL, D_MODEL)), vec, vec,
                  full((D_MODEL, LANES)), full((1, LANES))],
        out_specs=(pl.BlockSpec((tm * ROW_PITCH, LANES), lambda i: (i, 0)),
                   pl.BlockSpec((tm, TOP_K), lambda i: (i, 0)),
                   pl.BlockSpec((tm, TOP_K), lambda i: (i, 0)),
                   pl.BlockSpec((1, LANES), lambda i: (0, 0))),
        scratch_shapes=[pltpu.VMEM((1, LANES), _F32)],
        compiler_params=_cparams(("arbitrary",)),
        name="post_attn",
    )(x2d, out_a, out_b, w_out, g1, b1, kv, w_q, w_o, g2, b2, w_r, b_r)


def _rank_kernel(idx_ref, off_ref, pos_ref, next_ref, *, tm):
    @pl.when(pl.program_id(0) == 0)
    def _init():
        next_ref[...] = off_ref[...]

    idx = idx_ref[...]
    lane = lax.broadcasted_iota(jnp.int32, (tm, LANES), 1)
    hits = [lane == idx[:, k:k + 1] for k in range(TOP_K)]
    sel = jnp.zeros((tm, LANES), _F32)
    for hit in hits:
        sel = sel + jnp.where(hit, 1.0, 0.0)
    row = lax.broadcasted_iota(jnp.int32, (tm, tm), 0)
    col = lax.broadcasted_iota(jnp.int32, (tm, tm), 1)
    earlier = jnp.where(col < row, 1.0, 0.0).astype(jnp.bfloat16)
    dense = _dot(earlier, sel.astype(jnp.bfloat16)) + next_ref[...]
    pos_w = jnp.zeros((tm, LANES), _F32)
    for k, hit in enumerate(hits):
        pk = jnp.sum(jnp.where(hit, dense, 0.0), axis=1, keepdims=True)
        pos_w = jnp.where(lane == k, pk, pos_w)
    pos_ref[...] = pos_w[:, :TOP_K].astype(jnp.int32)
    next_ref[...] = next_ref[...] + jnp.sum(sel, axis=0, keepdims=True)


def _rank(idx, offsets, *, tm):
    t = idx.shape[0]
    return pl.pallas_call(
        functools.partial(_rank_kernel, tm=tm),
        out_shape=jax.ShapeDtypeStruct((t, TOP_K), jnp.int32),
        grid=(t // tm,),
        in_specs=[pl.BlockSpec((tm, TOP_K), lambda i: (i, 0)), pl.BlockSpec((1, LANES), lambda i: (0, 0))],
        out_specs=pl.BlockSpec((tm, TOP_K), lambda i: (i, 0)),
        scratch_shapes=[pltpu.VMEM((1, LANES), _F32)],
        compiler_params=_cparams(("arbitrary",)),
        name="rank",
    )(idx, offsets)


_DMA_UNROLL = 8
_DMA_THREADS = 2


def _for_each_row(n_rows, fn):
    tokens_per_group = _DMA_UNROLL // TOP_K

    def group(g, c):
        for u in range(_DMA_UNROLL):
            fn(g * _DMA_UNROLL + u, g * tokens_per_group + u // TOP_K, u % TOP_K)
        return c

    lax.fori_loop(0, n_rows // _DMA_UNROLL, group, 0)


def _dispatch_kernel(meta_ref, h_ref, pos_ref, xs_ref, zero_ref, sem, zsem, *, tm):
    n_rows = tm * TOP_K

    def zero_copy(r):
        return pltpu.make_async_copy(zero_ref, _slab(xs_ref, r), zsem)

    @pl.when(pl.program_id(0) == 0)
    def _zero_fill():
        zero_ref[...] = jnp.zeros_like(zero_ref)

        def per_expert(e, c):
            lo = meta_ref[e] + meta_ref[N_EXPERTS + e]
            hi = meta_ref[e] + meta_ref[2 * N_EXPERTS + e]
            lax.fori_loop(lo, hi, lambda r, c2: (zero_copy(r).start(), c2)[1], 0)
            lax.fori_loop(lo, hi, lambda r, c2: (zero_copy(r).wait(), c2)[1], 0)
            return c

        lax.fori_loop(0, N_EXPERTS, per_expert, 0)

    def row_copy(r, t, k):
        return pltpu.make_async_copy(_slab(h_ref, t), _slab(xs_ref, pos_ref[r]), sem)

    _for_each_row(n_rows, lambda r, t, k: row_copy(r, t, k).start(priority=k % _DMA_THREADS))
    _for_each_row(n_rows, lambda r, t, k: row_copy(r, t, k).wait())


def _dispatch(meta, h2t, pos_flat, n_sorted_rows, *, tm):
    t = h2t.shape[0] // ROW_PITCH
    return pl.pallas_call(
        functools.partial(_dispatch_kernel, tm=tm),
        out_shape=jax.ShapeDtypeStruct((n_sorted_rows * ROW_PITCH, LANES), _F32),
        grid_spec=pltpu.PrefetchScalarGridSpec(
            num_scalar_prefetch=1,
            grid=(t // tm,),
            in_specs=[pl.BlockSpec((tm * ROW_PITCH, LANES), lambda i, meta: (i, 0)),
                      pl.BlockSpec((tm * TOP_K,), lambda i, meta: (i,), memory_space=pltpu.SMEM)],
            out_specs=pl.BlockSpec(memory_space=pl.ANY),
            scratch_shapes=[pltpu.VMEM((ROW_PITCH, LANES), _F32),
                            pltpu.SemaphoreType.DMA, pltpu.SemaphoreType.DMA]),
        compiler_params=_cparams(("arbitrary",)),
        name="dispatch",
    )(meta, h2t, pos_flat)


_DEINT = 2 * LANES


def _deinterleave_matrix():
    k = jnp.arange(_DEINT)[:, None]
    n = jnp.arange(_DEINT)[None, :]
    src = jnp.where(n < LANES, 2 * n, 2 * (n - LANES) + 1)
    return (k == src).astype(_CDT)


def _experts_kernel(te_ref, nv_ref, xs_ref, perm_ref, w1_ref, b1g_ref, b1l_ref, w2_ref, b2_ref, y_ref,
                    wg_ref, wl_ref, w2c_ref, *, tm):
    i = pl.program_id(0)
    live = i < nv_ref[0]
    new_expert = (i == 0) | (te_ref[i] != te_ref[jnp.maximum(i - 1, 0)])

    @pl.when(live & new_expert)
    def _stage_weights():
        for c in range(2 * D_FF // _DEINT):
            both = _dot(w1_ref[0, :, c * _DEINT:(c + 1) * _DEINT].astype(_CDT), perm_ref[...])
            wg_ref[:, c * LANES:(c + 1) * LANES] = both[:, :LANES].astype(_CDT)
            wl_ref[:, c * LANES:(c + 1) * LANES] = both[:, LANES:].astype(_CDT)
        w2c_ref[...] = w2_ref[0].astype(_CDT)

    @pl.when(live)
    def _mlp():
        x = _load_row_tiles(xs_ref, tm).astype(_CDT)
        glu = jnp.minimum(_dot(x, wg_ref[...]) + b1g_ref[0], SWIGLU_LIMIT)
        lin = jnp.clip(_dot(x, wl_ref[...]) + b1l_ref[0], -SWIGLU_LIMIT, SWIGLU_LIMIT)
        act = glu * jax.nn.sigmoid(SWIGLU_ALPHA * glu) * (lin + 1.0)
        _store_row_tiles(y_ref, _dot(act.astype(_CDT), w2c_ref[...]) + b2_ref[0])

    @pl.when(jnp.logical_not(live))
    def _unused_tile():
        y_ref[...] = jnp.zeros_like(y_ref)


def _experts(tile_expert, n_valid, xs, w1, b1g, b1l, w2, b2, *, tm):
    n_tiles = xs.shape[0] // (tm * ROW_PITCH)
    by_expert = lambda shape: pl.BlockSpec((1,) + shape, lambda i, te, nv: (te[i], 0, 0))
    return pl.pallas_call(
        functools.partial(_experts_kernel, tm=tm),
        out_shape=jax.ShapeDtypeStruct(xs.shape, _F32),
        grid_spec=pltpu.PrefetchScalarGridSpec(
            num_scalar_prefetch=2,
            grid=(n_tiles,),
            in_specs=[pl.BlockSpec((tm * ROW_PITCH, LANES), lambda i, te, nv: (jnp.minimum(i, nv[0] - 1), 0)),
                      pl.BlockSpec((_DEINT, _DEINT), lambda i, te, nv: (0, 0)),
                      by_expert((D_MODEL, 2 * D_FF)),
                      by_expert((1, D_FF)), by_expert((1, D_FF)),
                      by_expert((D_FF, D_MODEL)), by_expert((1, D_MODEL))],
            out_specs=pl.BlockSpec((tm * ROW_PITCH, LANES), lambda i, te, nv: (i, 0)),
            scratch_shapes=[pltpu.VMEM((D_MODEL, D_FF), _CDT), pltpu.VMEM((D_MODEL, D_FF), _CDT),
                            pltpu.VMEM((D_FF, D_MODEL), _CDT)]),
        compiler_params=_cparams(("arbitrary",)),
        name="experts",
    )(tile_expert, n_valid, xs, _deinterleave_matrix(), w1, b1g, b1l, w2, b2)


def _combine_kernel(pos_ref, nxt_ref, wt_ref, h2t_ref, g_ref, b_ref, ys_ref, o_ref, buf_ref, sem, *, tm):
    n_rows = tm * TOP_K
    i = pl.program_id(0)
    slot = i & 1

    def row_copy(p_ref, s, r, t, k):
        return pltpu.make_async_copy(_slab(ys_ref, p_ref[r]), _slab(buf_ref.at[s, k], t), sem.at[s])

    @pl.when(i == 0)
    def _first():
        _for_each_row(n_rows, lambda r, t, k: row_copy(pos_ref, 0, r, t, k).start(priority=k % _DMA_THREADS))

    for r in range(n_rows):
        row_copy(pos_ref, slot, r, r // TOP_K, r % TOP_K).wait()
    for r in range(n_rows):
        row_copy(nxt_ref, 1 - slot, r, r // TOP_K, r % TOP_K).start(priority=r % _DMA_THREADS)
    wt = wt_ref[...]
    ffn = wt[:, 0:1] * _load_row_tiles(buf_ref.at[slot, 0], tm)
    for k in range(1, TOP_K):
        ffn = ffn + wt[:, k:k + 1] * _load_row_tiles(buf_ref.at[slot, k], tm)
    o_ref[...] = _layer_norm(DEEPNORM_ALPHA * _load_row_tiles(h2t_ref, tm) + ffn, g_ref[...], b_ref[...])

    @pl.when(i + 1 == pl.num_programs(0))
    def _drain():
        _for_each_row(n_rows, lambda r, t, k: row_copy(nxt_ref, 1 - slot, r, t, k).wait())


def _combine(pos_flat, wts, h2t, g3, b3, ys, *, tm):
    t = h2t.shape[0] // ROW_PITCH
    n_steps = t // tm
    vec = pl.BlockSpec((1, D_MODEL), lambda i: (0, 0))
    return pl.pallas_call(
        functools.partial(_combine_kernel, tm=tm),
        out_shape=jax.ShapeDtypeStruct((t, D_MODEL), _F32),
        grid=(n_steps,),
        in_specs=[pl.BlockSpec((tm * TOP_K,), lambda i: (i,), memory_space=pltpu.SMEM),
                  pl.BlockSpec((tm * TOP_K,), lambda i: (jnp.minimum(i + 1, n_steps - 1),), memory_space=pltpu.SMEM),
                  pl.BlockSpec((tm, TOP_K), lambda i: (i, 0)),
                  pl.BlockSpec((tm * ROW_PITCH, LANES), lambda i: (i, 0)), vec, vec,
                  pl.BlockSpec(memory_space=pl.ANY)],
        out_specs=pl.BlockSpec((tm, D_MODEL), lambda i: (i, 0)),
        scratch_shapes=[pltpu.VMEM((2, TOP_K, tm * ROW_PITCH, LANES), _F32), pltpu.SemaphoreType.DMA((2,))],
        compiler_params=_cparams(("arbitrary",)),
        name="combine",
    )(pos_flat, pos_flat, wts, h2t, g3, b3, ys)


def _tiles(seq):
    big = 512 if seq % 512 == 0 else MOBA_BLOCK
    post = 1024 if seq % 1024 == 0 else big
    return dict(inproj=big, attn_q=MOBA_BLOCK, post=post, rank=big, dispatch=big, expert=512, combine=big)


def kernel(x, mem, w_in, diff_lambda_q1, diff_lambda_k1, diff_lambda_q2, diff_lambda_k2, diff_subln_g,
           w_mix_out, ln1_g, ln1_b, mem_ln_g, mem_ln_b, w_mem_q, w_mem_kv, w_mem_o, ln2_g, ln2_b, w_router,
           b_router, w_mlp1, b_mlp1, w_mlp2, b_mlp2, ln3_g, ln3_b):
    batch, seq, d = x.shape
    mem_len = mem.shape[1]
    assert d == D_MODEL and seq % MOBA_BLOCK == 0 and w_in.shape[0] == 1
    t = batch * seq
    ts = _tiles(seq)
    row = lambda v: v.reshape(1, -1).astype(_F32)

    col = jnp.arange(PROJ_WIDTH)
    is_q = (col < DIFF_WIDTH) | ((col >= 3 * DIFF_WIDTH) & (col < 3 * DIFF_WIDTH + MOBA_WIDTH))
    w_in_c = (w_in[0] * jnp.where(is_q, HEAD_DIM ** -0.5 * math.log2(math.e), 1.0)[None, :]).astype(_CDT)
    lam_vecs = jnp.stack([diff_lambda_q1[0], diff_lambda_k1[0], diff_lambda_q2[0], diff_lambda_k2[0]]).astype(_F32)
    w_r = jnp.zeros((D_MODEL, LANES), _F32).at[:, :N_EXPERTS].set(w_router[0]).astype(_CDT)
    b_r = jnp.full((1, LANES), _NEG, _F32).at[0, :N_EXPERTS].set(b_router[0])
    b1g = b_mlp1[0][:, None, 0::2].astype(_F32)
    b1l = b_mlp1[0][:, None, 1::2].astype(_F32)
    b2 = b_mlp2[0][:, None, :].astype(_F32)

    x2d = x.reshape(t, d)
    proj, kmean = _inproj(x2d, w_in_c, seq, tm=ts["inproj"])
    kmean = kmean.reshape(batch, seq // MOBA_BLOCK, MOBA_WIDTH)
    out_a = _diff_attn(proj, lam_vecs, row(diff_subln_g[0]), batch, seq, tq=ts["attn_q"])
    out_b = _moba_attn(proj, kmean, batch, seq)
    kv = _mem_kv(mem.reshape(batch * mem_len, d), row(mem_ln_g), row(mem_ln_b), w_mem_kv[0].astype(_CDT),
                 tm=mem_len)
    h2t, top_idx, top_w, counts = _post_attn(
        x2d, out_a, out_b, w_mix_out[0].astype(_CDT), row(ln1_g[0]), row(ln1_b[0]), kv,
        w_mem_q[0].astype(_CDT), w_mem_o[0].astype(_CDT), row(ln2_g[0]), row(ln2_b[0]), w_r, b_r,
        seq, mem_len, tm=ts["post"])

    tmx = ts["expert"]
    cnt = counts[0, :N_EXPERTS]
    padded = (cnt + tmx - 1) // tmx * tmx
    ends = jnp.cumsum(padded)
    offsets = ends - padded
    meta = jnp.concatenate([offsets, cnt, padded]).astype(jnp.int32)
    n_sorted_rows = t * TOP_K + N_EXPERTS * tmx
    n_tiles = n_sorted_rows // tmx
    tile_start = jnp.arange(n_tiles, dtype=jnp.int32) * tmx
    tile_expert = jnp.minimum(jnp.sum(tile_start[:, None] >= ends[None, :], axis=1), N_EXPERTS - 1).astype(jnp.int32)
    n_valid = (ends[-1:] // tmx).astype(jnp.int32)
    offsets_row = jnp.zeros((1, LANES), _F32).at[0, :N_EXPERTS].set(offsets.astype(_F32))
    pos = _rank(top_idx, offsets_row, tm=ts["rank"]).reshape(-1)

    xs = _dispatch(meta, h2t, pos, n_sorted_rows, tm=ts["dispatch"])
    ys = _experts(tile_expert, n_valid, xs, w_mlp1[0], b1g, b1l, w_mlp2[0], b2, tm=tmx)
    out = _combine(pos, top_w, h2t, row(ln3_g[0]), row(ln3_b[0]), ys, tm=ts["combine"])
    return out.reshape(batch, seq, d)
```

```python
import functools
import math

import jax
import jax.numpy as jnp
from jax import lax
from jax.experimental import pallas as pl
from jax.experimental.pallas import tpu as pltpu

D_MODEL = 1024
DIFF_HEADS = 4
HEAD_DIM = 64
DIFF_WIDTH = DIFF_HEADS * 2 * HEAD_DIM
MOBA_HEADS = 8
MOBA_WIDTH = MOBA_HEADS * HEAD_DIM
MOBA_BLOCK = 256
MOBA_TOPK = 3
ROPE_THETA = 500000.0
ROT_DIM = HEAD_DIM // 4
MEM_HEADS = 4
MEM_HEAD_DIM = D_MODEL // MEM_HEADS
N_EXPERTS = 32
TOP_K = 4
D_FF = D_MODEL
SWIGLU_ALPHA = 1.702
SWIGLU_LIMIT = 7.0
LN_EPS = 1e-5
RMS_EPS = 1e-5
DEEPNORM_ALPHA = 2.0 ** 0.25
LAMBDA_INIT = 0.8 - 0.6 * math.exp(0.0)

LANES = 128
ROW_TILE = 8
ROW_PITCH = ROW_TILE + 1
V7X_VMEM_LIMIT = 56 * 1024 * 1024

_CDT = jnp.bfloat16
_F32 = jnp.float32
_NEG = -1e30

_CB_DQ, _CB_DK, _CB_DV = 0, 4, 8
_CB_MQ, _CB_MK, _CB_MV = 12, 16, 20
PROJ_WIDTH = 3 * (DIFF_WIDTH + MOBA_WIDTH)


def _cparams(sem, vmem=V7X_VMEM_LIMIT):
    return pltpu.CompilerParams(dimension_semantics=sem, vmem_limit_bytes=vmem)


def _dot(a, b):
    return jnp.dot(a, b, preferred_element_type=_F32)


def _dot_nt(a, b):
    return lax.dot_general(a, b, (((1,), (1,)), ((), ())), preferred_element_type=_F32)


def _layer_norm(x, g, b):
    mu = jnp.mean(x, axis=-1, keepdims=True)
    xc = x - mu
    var = jnp.mean(xc * xc, axis=-1, keepdims=True)
    return xc * lax.rsqrt(var + LN_EPS) * g + b


_ROPE_BLOCKS = tuple(range(_CB_DQ, _CB_DV)) + tuple(range(_CB_MQ, _CB_MV))


def _inproj_kernel(x_ref, w_ref, cos_ref, sa_ref, sb_ref, o_ref, km_ref, *, tm):
    x = x_ref[...].astype(_CDT)
    cos, sa, sb = cos_ref[...], sa_ref[...], sb_ref[...]
    seg_w = 4 * LANES
    for seg in range(PROJ_WIDTH // seg_w):
        pseg = _dot(x, w_ref[:, seg * seg_w:(seg + 1) * seg_w])
        for c in range(4):
            cb = seg * 4 + c
            cols = slice(cb * LANES, (cb + 1) * LANES)
            p = pseg[:, c * LANES:(c + 1) * LANES]
            if cb in _ROPE_BLOCKS:
                p = p * cos + pltpu.roll(p, 8, 1) * sa + pltpu.roll(p, LANES - 8, 1) * sb
            o_ref[:, cols] = p.astype(o_ref.dtype)
            if _CB_MK <= cb < _CB_MV:
                kc = slice((cb - _CB_MK) * LANES, (cb - _CB_MK + 1) * LANES)
                for blk in range(tm // MOBA_BLOCK):
                    rows = p[blk * MOBA_BLOCK:(blk + 1) * MOBA_BLOCK, :]
                    km_ref[blk, :, kc] = jnp.sum(rows, axis=0, keepdims=True) * (1.0 / MOBA_BLOCK)


def _rope_tables(seq):
    half = ROT_DIM // 2
    inv_freq = ROPE_THETA ** (-jnp.arange(0, ROT_DIM, 2, dtype=_F32) / ROT_DIM)
    ang = jnp.arange(seq, dtype=_F32)[:, None] * inv_freq[None, :]
    cos, sin = jnp.cos(ang), jnp.sin(ang)
    lane = jnp.arange(LANES) % HEAD_DIM
    first, second = lane < half, (lane >= half) & (lane < ROT_DIM)
    idx = jnp.where(first, lane, jnp.where(second, lane - half, 0))
    cos_t = jnp.where((first | second)[None, :], cos[:, idx], 1.0)
    sa_t = jnp.where(second[None, :], sin[:, idx], 0.0)
    sb_t = jnp.where(first[None, :], -sin[:, idx], 0.0)
    return cos_t, sa_t, sb_t


def _inproj(x2d, w_in, seq, *, tm):
    t = x2d.shape[0]
    cos_t, sa_t, sb_t = _rope_tables(seq)
    n_pos = seq // tm
    tab_spec = pl.BlockSpec((tm, LANES), lambda i: (i % n_pos, 0))
    return pl.pallas_call(
        functools.partial(_inproj_kernel, tm=tm),
        out_shape=(jax.ShapeDtypeStruct((t, PROJ_WIDTH), _CDT),
                   jax.ShapeDtypeStruct((t // MOBA_BLOCK, 1, MOBA_WIDTH), _F32)),
        grid=(t // tm,),
        in_specs=[pl.BlockSpec((tm, D_MODEL), lambda i: (i, 0)),
                  pl.BlockSpec((D_MODEL, PROJ_WIDTH), lambda i: (0, 0)),
                  tab_spec, tab_spec, tab_spec],
        out_specs=(pl.BlockSpec((tm, PROJ_WIDTH), lambda i: (i, 0)),
                   pl.BlockSpec((tm // MOBA_BLOCK, 1, MOBA_WIDTH), lambda i: (i, 0, 0))),
        compiler_params=_cparams(("parallel",)),
        name="inproj",
    )(x2d, w_in, cos_t, sa_t, sb_t)


def _exp2_parts(parts):
    chunks = [s[:, c:c + LANES] for s in parts for c in range(0, s.shape[1], LANES)]
    m = jnp.max(functools.reduce(jnp.maximum, chunks), axis=1, keepdims=True)
    return [jnp.exp2(s - m).astype(_CDT) for s in parts]


_SCORES_AHEAD = 1


def _causal_mask(tq):
    row = lax.broadcasted_iota(jnp.int32, (tq, tq), 0)
    col = lax.broadcasted_iota(jnp.int32, (tq, tq), 1)
    return col <= row


def _diff_attn_kernel(q_ref, k_ref, v_ref, lam_ref, g_ref, o_ref, *, seq, tq):
    lane = lax.broadcasted_iota(jnp.int32, (tq, LANES), 1)
    mask = _causal_mask(tq)
    lv = lam_ref[...]
    lam = (jnp.exp(jnp.sum(lv[0:1] * lv[1:2], axis=1, keepdims=True))
           - jnp.exp(jnp.sum(lv[2:3] * lv[3:4], axis=1, keepdims=True)) + LAMBDA_INIT)
    gain = g_ref[...] * (1.0 - LAMBDA_INIT)

    def with_ones(v):
        return jnp.concatenate([v, jnp.ones_like(v)], axis=1)

    def scores(i):
        rows = slice(i * tq, (i + 1) * tq)
        q = q_ref[rows, :]
        zero = jnp.zeros_like(q)
        q1 = jnp.where(lane < HEAD_DIM, q, zero)
        q2 = jnp.where(lane >= HEAD_DIM, q, zero)
        kd = k_ref[rows, :]
        s1 = [jnp.where(mask, _dot_nt(q1, kd), _NEG)]
        s2 = [jnp.where(mask, _dot_nt(q2, kd), _NEG)]
        if i:
            kp = k_ref[:i * tq, :]
            s1.append(_dot_nt(q1, kp))
            s2.append(_dot_nt(q2, kp))
        return s1, s2

    def finish(i, s1, s2):
        rows = slice(i * tq, (i + 1) * tq)
        v_all = with_ones(v_ref[:(i + 1) * tq, :])
        nums = []
        for parts in (s1, s2):
            es = _exp2_parts(parts)
            nl = _dot(jnp.concatenate(es[1:] + es[:1], axis=1), v_all)
            nums.append(nl[:, :LANES] / nl[:, LANES:LANES + 1])
        o = nums[0] - lam * nums[1]
        o = o * lax.rsqrt(jnp.mean(o * o, axis=1, keepdims=True) + RMS_EPS) * gain
        o_ref[rows, :] = o.astype(o_ref.dtype)

    n_tiles = seq // tq
    pending = [scores(i) for i in range(min(_SCORES_AHEAD, n_tiles))]
    for i in range(n_tiles):
        if i + _SCORES_AHEAD < n_tiles:
            pending.append(scores(i + _SCORES_AHEAD))
        finish(i, *pending.pop(0))


def _diff_attn(proj, lam_vecs, subln_g, batch, seq, *, tq):
    t = batch * seq
    blk = lambda cb: pl.BlockSpec((seq, LANES), lambda b, h: (b, cb + h))
    return pl.pallas_call(
        functools.partial(_diff_attn_kernel, seq=seq, tq=tq),
        out_shape=jax.ShapeDtypeStruct((t, DIFF_WIDTH), _CDT),
        grid=(batch, DIFF_HEADS),
        in_specs=[blk(_CB_DQ), blk(_CB_DK), blk(_CB_DV),
                  pl.BlockSpec((4, HEAD_DIM), lambda b, h: (0, 0)),
                  pl.BlockSpec((1, 2 * HEAD_DIM), lambda b, h: (0, 0))],
        out_specs=pl.BlockSpec((seq, LANES), lambda b, h: (b, h)),
        compiler_params=_cparams(("parallel", "parallel")),
        name="diff_attn",
    )(proj, proj, proj, lam_vecs, subln_g)


def _moba_selection(gate, n_past):
    blk = lax.broadcasted_iota(jnp.int32, gate.shape, 0)
    rank = jnp.zeros(gate.shape, _F32)
    for m_blk in range(n_past):
        gm = gate[m_blk:m_blk + 1, :]
        beats = (gm > gate) | ((gm == gate) & (m_blk < blk))
        rank = rank + jnp.where(beats, 1.0, 0.0)
    return jnp.where((blk < n_past) & (rank < MOBA_TOPK), 1.0, 0.0)


def _moba_kernel(q_ref, k_ref, v_ref, km_ref, o_ref, *, nb):
    tq = MOBA_BLOCK
    lane = lax.broadcasted_iota(jnp.int32, (tq, LANES), 1)
    mask = _causal_mask(tq)
    km = km_ref[...].astype(_CDT)
    in_head = [(lane >= hh * HEAD_DIM) & (lane < (hh + 1) * HEAD_DIM) for hh in range(2)]
    seq_lane = lax.broadcasted_iota(jnp.int32, (nb * tq, LANES), 1)
    v_ones = [jnp.where((seq_lane >= hh * HEAD_DIM) & (seq_lane < (hh + 1) * HEAD_DIM), v_ref[...],
                        jnp.ones((nb * tq, LANES), _CDT)) for hh in range(2)]

    def scores(i, hh):
        rows = slice(i * tq, (i + 1) * tq)
        q = q_ref[rows, :]
        qh = jnp.where(in_head[hh], q, jnp.zeros_like(q))
        parts = [jnp.where(mask, _dot_nt(qh, k_ref[rows, :]), _NEG)]
        gated = i > MOBA_TOPK
        if gated:
            sel = _moba_selection(_dot_nt(km, qh), i)
            sel_q = jnp.concatenate([sel, jnp.zeros((LANES - nb, tq), _F32)], axis=0).T
        for j in range(i):
            s = _dot_nt(qh, k_ref[j * tq:(j + 1) * tq, :])
            parts.append(jnp.where(sel_q[:, j:j + 1] > 0.5, s, _NEG) if gated else s)
        return parts

    def finish(i, hh, parts):
        es = _exp2_parts(parts)
        acc = _dot(jnp.concatenate(es[1:] + es[:1], axis=1), v_ones[hh][:(i + 1) * tq, :])
        sum_lane = (1 - hh) * HEAD_DIM
        return acc / acc[:, sum_lane:sum_lane + 1]

    units = [(i, hh) for i in range(nb) for hh in range(2)]
    pending = [scores(*u) for u in units[:_SCORES_AHEAD]]
    outs = {}
    for n, (i, hh) in enumerate(units):
        if n + _SCORES_AHEAD < len(units):
            pending.append(scores(*units[n + _SCORES_AHEAD]))
        outs[hh] = finish(i, hh, pending.pop(0))
        if hh == 1:
            o_ref[i * tq:(i + 1) * tq, :] = jnp.where(in_head[0], outs[0], outs[1]).astype(o_ref.dtype)


def _moba_attn(proj, kmean, batch, seq):
    nb = seq // MOBA_BLOCK
    t = batch * seq
    blk = lambda cb: pl.BlockSpec((seq, LANES), lambda b, p: (b, cb + p))
    return pl.pallas_call(
        functools.partial(_moba_kernel, nb=nb),
        out_shape=jax.ShapeDtypeStruct((t, MOBA_WIDTH), _CDT),
        grid=(batch, MOBA_HEADS // 2),
        in_specs=[blk(_CB_MQ), blk(_CB_MK), blk(_CB_MV),
                  pl.BlockSpec((None, nb, LANES), lambda b, p: (b, 0, p))],
        out_specs=pl.BlockSpec((seq, LANES), lambda b, p: (b, p)),
        compiler_params=_cparams(("parallel", "parallel")),
        name="moba_attn",
    )(proj, proj, proj, kmean)


def _mem_kv_kernel(mem_ref, g_ref, b_ref, w_ref, o_ref):
    mn = _layer_norm(mem_ref[...], g_ref[...], b_ref[...])
    o_ref[...] = _dot(mn.astype(_CDT), w_ref[...]).astype(o_ref.dtype)


def _mem_kv(mem2d, g, b, w_kv, *, tm):
    rows = mem2d.shape[0]
    vec = pl.BlockSpec((1, D_MODEL), lambda i: (0, 0))
    return pl.pallas_call(
        _mem_kv_kernel,
        out_shape=jax.ShapeDtypeStruct((rows, 2 * D_MODEL), _CDT),
        grid=(rows // tm,),
        in_specs=[pl.BlockSpec((tm, D_MODEL), lambda i: (i, 0)), vec, vec,
                  pl.BlockSpec((D_MODEL, 2 * D_MODEL), lambda i: (0, 0))],
        out_specs=pl.BlockSpec((tm, 2 * D_MODEL), lambda i: (i, 0)),
        compiler_params=_cparams(("parallel",)),
        name="mem_kv",
    )(mem2d, g, b, w_kv)


def _store_row_tiles(ref, val):
    n = val.shape[0]
    for c in range(ROW_TILE):
        ref[pl.ds(c, n, stride=ROW_PITCH), :] = val[:, c * LANES:(c + 1) * LANES]
    ref[pl.ds(ROW_TILE, n, stride=ROW_PITCH), :] = jnp.zeros((n, LANES), val.dtype)


def _load_row_tiles(ref, n, pitch=ROW_PITCH):
    return jnp.concatenate([ref[pl.ds(c, n, stride=pitch), :] for c in range(ROW_TILE)], axis=1)


def _slab(ref, i):
    return ref.at[pl.ds(i * ROW_PITCH, ROW_PITCH)]


def _tile(ref, i, pitch):
    start = pl.multiple_of(i * ROW_TILE, ROW_TILE) if pitch == ROW_TILE else i * pitch
    return ref.at[pl.ds(start, ROW_TILE)]


_DONE = object()


def _interleave(stage_generators):
    live = list(stage_generators)
    while live:
        live = [g for g in live if next(g, _DONE) is not _DONE]


def _post_attn_kernel(x_ref, oa_ref, ob_ref, wout_ref, g1_ref, b1_ref, kv_ref, wq_ref, wo_ref,
                      g2_ref, b2_ref, wr_ref, br_ref, h2t_ref, idx_ref, wt_ref, cnt_ref, cnt_acc, *, tm, groups):
    @pl.when(pl.program_id(0) == 0)
    def _init_counts():
        cnt_acc[...] = jnp.zeros_like(cnt_acc)

    n = tm // groups
    lane = lax.broadcasted_iota(jnp.int32, (n, LANES), 1)
    lane_f = lane.astype(_F32)

    def row_group(g):
        rows = slice(g * n, (g + 1) * n)
        mix = _dot(oa_ref[rows, :], wout_ref[:DIFF_WIDTH, :]) + _dot(ob_ref[rows, :], wout_ref[DIFF_WIDTH:, :])
        yield
        h1 = _layer_norm(DEEPNORM_ALPHA * x_ref[rows, :] + mix, g1_ref[...], b1_ref[...])
        q = (_dot(h1.astype(_CDT), wq_ref[...]) * (MEM_HEAD_DIM ** -0.5)).astype(_CDT)
        yield
        heads = []
        for h in range(MEM_HEADS):
            c0 = h * MEM_HEAD_DIM
            kh = kv_ref[:, c0:c0 + MEM_HEAD_DIM]
            vh = kv_ref[:, D_MODEL + c0:D_MODEL + c0 + MEM_HEAD_DIM]
            s = _dot_nt(q[:, c0:c0 + MEM_HEAD_DIM], kh)
            p = jnp.exp(s - jnp.max(s, axis=1, keepdims=True))
            o = _dot(p.astype(_CDT), vh) / jnp.sum(p, axis=1, keepdims=True)
            heads.append(o.astype(_CDT))
        yield
        xatt = _dot(jnp.concatenate(heads, axis=1), wo_ref[...])
        yield
        h2 = _layer_norm(DEEPNORM_ALPHA * h1 + xatt, g2_ref[...], b2_ref[...])
        _store_row_tiles(h2t_ref.at[pl.ds(g * n * ROW_PITCH, n * ROW_PITCH)], h2)
        logits = _dot(h2.astype(_CDT), wr_ref[...]) + br_ref[...]
        yield
        vals, idxs = [], []
        work = logits
        for _ in range(TOP_K):
            m = jnp.max(work, axis=1, keepdims=True)
            ik = jnp.min(jnp.where(work == m, lane_f, float(LANES)), axis=1, keepdims=True)
            vals.append(m)
            idxs.append(ik)
            work = jnp.where(lane_f == ik, -jnp.inf, work)
        exps = [jnp.exp(v - vals[0]) for v in vals]
        denom = exps[0] + exps[1] + exps[2] + exps[3]
        idx_w = jnp.zeros((n, LANES), _F32)
        wt_w = jnp.zeros((n, LANES), _F32)
        for k in range(TOP_K):
            idx_w = jnp.where(lane == k, idxs[k], idx_w)
            wt_w = jnp.where(lane == k, exps[k] / denom, wt_w)
        idx_ref[rows, :] = idx_w[:, :TOP_K].astype(jnp.int32)
        wt_ref[rows, :] = wt_w[:, :TOP_K]
        chosen = functools.reduce(jnp.add, [jnp.where(lane_f == ik, 1.0, 0.0) for ik in idxs])
        cnt_acc[...] = cnt_acc[...] + jnp.sum(chosen, axis=0, keepdims=True)

    _interleave(row_group(g) for g in range(groups))
    cnt_ref[...] = cnt_acc[...].astype(jnp.int32)


def _post_attn(x2d, out_a, out_b, w_out, g1, b1, kv, w_q, w_o, g2, b2, w_r, b_r, seq, mem_len, *, tm):
    t = x2d.shape[0]
    per_b = seq // tm
    full = lambda shape: pl.BlockSpec(shape, lambda i: (0,) * len(shape))
    vec = full((1, D_MODEL))
    return pl.pallas_call(
        functools.partial(_post_attn_kernel, tm=tm, groups=max(1, tm // MOBA_BLOCK)),
        out_shape=(jax.ShapeDtypeStruct((t * ROW_PITCH, LANES), _F32),
                   jax.ShapeDtypeStruct((t, TOP_K), jnp.int32),
                   jax.ShapeDtypeStruct((t, TOP_K), _F32),
                   jax.ShapeDtypeStruct((1, LANES), jnp.int32)),
        grid=(t // tm,),
        in_specs=[pl.BlockSpec((tm, D_MODEL), lambda i: (i, 0)),
                  pl.BlockSpec((tm, DIFF_WIDTH), lambda i: (i, 0)),
                  pl.BlockSpec((tm, MOBA_WIDTH), lambda i: (i, 0)),
                  full((D_MODEL, D_MODEL)), vec, vec,
                  pl.BlockSpec((mem_len, 2 * D_MODEL), lambda i: (i // per_b, 0)),
                  full((D_MODEL, D_MODEL)), full((D_MODEL, D_MODEL)), vec, vec,
                  full((D_MODEL, LANES)), full((1, LANES))],
        out_specs=(pl.BlockSpec((tm * ROW_PITCH, LANES), lambda i: (i, 0)),
                   pl.BlockSpec((tm, TOP_K), lambda i: (i, 0)),
                   pl.BlockSpec((tm, TOP_K), lambda i: (i, 0)),
                   pl.BlockSpec((1, LANES), lambda i: (0, 0))),
        scratch_shapes=[pltpu.VMEM((1, LANES), _F32)],
        compiler_params=_cparams(("arbitrary",)),
        name="post_attn",
    )(x2d, out_a, out_b, w_out, g1, b1, kv, w_q, w_o, g2, b2, w_r, b_r)


def _rank_kernel(idx_ref, off_ref, pos_ref, next_ref, *, tm):
    @pl.when(pl.program_id(0) == 0)
    def _init():
        next_ref[...] = off_ref[...]

    idx = idx_ref[...]
    lane = lax.broadcasted_iota(jnp.int32, (tm, LANES), 1)
    hits = [lane == idx[:, k:k + 1] for k in range(TOP_K)]
    sel = jnp.zeros((tm, LANES), _F32)
    for hit in hits:
        sel = sel + jnp.where(hit, 1.0, 0.0)
    row = lax.broadcasted_iota(jnp.int32, (tm, tm), 0)
    col = lax.broadcasted_iota(jnp.int32, (tm, tm), 1)
    earlier = jnp.where(col < row, 1.0, 0.0).astype(jnp.bfloat16)
    dense = _dot(earlier, sel.astype(jnp.bfloat16)) + next_ref[...]
    pos_w = jnp.zeros((tm, LANES), _F32)
    for k, hit in enumerate(hits):
        pk = jnp.sum(jnp.where(hit, dense, 0.0), axis=1, keepdims=True)
        pos_w = jnp.where(lane == k, pk, pos_w)
    pos_ref[...] = pos_w[:, :TOP_K].astype(jnp.int32)
    next_ref[...] = next_ref[...] + jnp.sum(sel, axis=0, keepdims=True)


def _rank(idx, offsets, *, tm):
    t = idx.shape[0]
    return pl.pallas_call(
        functools.partial(_rank_kernel, tm=tm),
        out_shape=jax.ShapeDtypeStruct((t, TOP_K), jnp.int32),
        grid=(t // tm,),
        in_specs=[pl.BlockSpec((tm, TOP_K), lambda i: (i, 0)), pl.BlockSpec((1, LANES), lambda i: (0, 0))],
        out_specs=pl.BlockSpec((tm, TOP_K), lambda i: (i, 0)),
        scratch_shapes=[pltpu.VMEM((1, LANES), _F32)],
        compiler_params=_cparams(("arbitrary",)),
        name="rank",
    )(idx, offsets)


_DMA_UNROLL = 8
_DMA_THREADS = 2


def _for_each_row(n_rows, fn):
    tokens_per_group = _DMA_UNROLL // TOP_K

    def group(g, c):
        for u in range(_DMA_UNROLL):
            fn(g * _DMA_UNROLL + u, g * tokens_per_group + u // TOP_K, u % TOP_K)
        return c

    lax.fori_loop(0, n_rows // _DMA_UNROLL, group, 0)


def _dispatch_kernel(meta_ref, h_ref, pos_ref, xs_ref, zero_ref, sem, zsem, *, tm):
    n_rows = tm * TOP_K

    def zero_copy(r):
        return pltpu.make_async_copy(zero_ref, _tile(xs_ref, r, ROW_TILE), zsem)

    @pl.when(pl.program_id(0) == 0)
    def _zero_fill():
        zero_ref[...] = jnp.zeros_like(zero_ref)

        def per_expert(e, c):
            lo = meta_ref[e] + meta_ref[N_EXPERTS + e]
            hi = meta_ref[e] + meta_ref[2 * N_EXPERTS + e]
            lax.fori_loop(lo, hi, lambda r, c2: (zero_copy(r).start(), c2)[1], 0)
            lax.fori_loop(lo, hi, lambda r, c2: (zero_copy(r).wait(), c2)[1], 0)
            return c

        lax.fori_loop(0, N_EXPERTS, per_expert, 0)

    def row_copy(r, t, k):
        return pltpu.make_async_copy(_tile(h_ref, t, ROW_PITCH), _tile(xs_ref, pos_ref[r], ROW_TILE), sem)

    _for_each_row(n_rows, lambda r, t, k: row_copy(r, t, k).start(priority=k % _DMA_THREADS))
    _for_each_row(n_rows, lambda r, t, k: row_copy(r, t, k).wait())


def _dispatch(meta, h2t, pos_flat, n_sorted_rows, *, tm):
    t = h2t.shape[0] // ROW_PITCH
    return pl.pallas_call(
        functools.partial(_dispatch_kernel, tm=tm),
        out_shape=jax.ShapeDtypeStruct((n_sorted_rows * ROW_TILE, LANES), _F32),
        grid_spec=pltpu.PrefetchScalarGridSpec(
            num_scalar_prefetch=1,
            grid=(t // tm,),
            in_specs=[pl.BlockSpec((tm * ROW_PITCH, LANES), lambda i, meta: (i, 0)),
                      pl.BlockSpec((tm * TOP_K,), lambda i, meta: (i,), memory_space=pltpu.SMEM)],
            out_specs=pl.BlockSpec(memory_space=pl.ANY),
            scratch_shapes=[pltpu.VMEM((ROW_TILE, LANES), _F32),
                            pltpu.SemaphoreType.DMA, pltpu.SemaphoreType.DMA]),
        compiler_params=_cparams(("arbitrary",)),
        name="dispatch",
    )(meta, h2t, pos_flat)


_DEINT = 2 * LANES


def _deinterleave_matrix():
    k = jnp.arange(_DEINT)[:, None]
    n = jnp.arange(_DEINT)[None, :]
    src = jnp.where(n < LANES, 2 * n, 2 * (n - LANES) + 1)
    return (k == src).astype(_CDT)


def _experts_kernel(te_ref, nv_ref, xs_ref, perm_ref, w1_ref, b1g_ref, b1l_ref, w2_ref, b2_ref, y_ref,
                    wg_ref, wl_ref, w2c_ref, *, tm):
    i = pl.program_id(0)
    live = i < nv_ref[0]
    new_expert = (i == 0) | (te_ref[i] != te_ref[jnp.maximum(i - 1, 0)])

    @pl.when(live & new_expert)
    def _stage_weights():
        for c in range(2 * D_FF // _DEINT):
            both = _dot(w1_ref[0, :, c * _DEINT:(c + 1) * _DEINT].astype(_CDT), perm_ref[...])
            wg_ref[:, c * LANES:(c + 1) * LANES] = both[:, :LANES].astype(_CDT)
            wl_ref[:, c * LANES:(c + 1) * LANES] = both[:, LANES:].astype(_CDT)
        w2c_ref[...] = w2_ref[0].astype(_CDT)

    @pl.when(live)
    def _mlp():
        x = _load_row_tiles(xs_ref, tm, ROW_TILE).astype(_CDT)
        glu = jnp.minimum(_dot(x, wg_ref[...]) + b1g_ref[0], SWIGLU_LIMIT)
        lin = jnp.clip(_dot(x, wl_ref[...]) + b1l_ref[0], -SWIGLU_LIMIT, SWIGLU_LIMIT)
        act = glu * jax.nn.sigmoid(SWIGLU_ALPHA * glu) * (lin + 1.0)
        _store_row_tiles(y_ref, _dot(act.astype(_CDT), w2c_ref[...]) + b2_ref[0])

    @pl.when(jnp.logical_not(live))
    def _unused_tile():
        y_ref[...] = jnp.zeros_like(y_ref)


def _experts(tile_expert, n_valid, xs, w1, b1g, b1l, w2, b2, *, tm):
    n_tiles = xs.shape[0] // (tm * ROW_TILE)
    by_expert = lambda shape: pl.BlockSpec((1,) + shape, lambda i, te, nv: (te[i], 0, 0))
    return pl.pallas_call(
        functools.partial(_experts_kernel, tm=tm),
        out_shape=jax.ShapeDtypeStruct((n_tiles * tm * ROW_PITCH, LANES), _F32),
        grid_spec=pltpu.PrefetchScalarGridSpec(
            num_scalar_prefetch=2,
            grid=(n_tiles,),
            in_specs=[pl.BlockSpec((tm * ROW_TILE, LANES), lambda i, te, nv: (jnp.minimum(i, nv[0] - 1), 0)),
                      pl.BlockSpec((_DEINT, _DEINT), lambda i, te, nv: (0, 0)),
                      by_expert((D_MODEL, 2 * D_FF)),
                      by_expert((1, D_FF)), by_expert((1, D_FF)),
                      by_expert((D_FF, D_MODEL)), by_expert((1, D_MODEL))],
            out_specs=pl.BlockSpec((tm * ROW_PITCH, LANES), lambda i, te, nv: (i, 0)),
            scratch_shapes=[pltpu.VMEM((D_MODEL, D_FF), _CDT), pltpu.VMEM((D_MODEL, D_FF), _CDT),
                            pltpu.VMEM((D_FF, D_MODEL), _CDT)]),
        compiler_params=_cparams(("arbitrary",)),
        name="experts",
    )(tile_expert, n_valid, xs, _deinterleave_matrix(), w1, b1g, b1l, w2, b2)


def _combine_kernel(pos_ref, nxt_ref, wt_ref, h2t_ref, g_ref, b_ref, ys_ref, o_ref, buf_ref, sem, *, tm):
    n_rows = tm * TOP_K
    i = pl.program_id(0)
    slot = i & 1

    def row_copy(p_ref, s, r, t, k):
        return pltpu.make_async_copy(_slab(ys_ref, p_ref[r]), _slab(buf_ref.at[s, k], t), sem.at[s])

    @pl.when(i == 0)
    def _first():
        _for_each_row(n_rows, lambda r, t, k: row_copy(pos_ref, 0, r, t, k).start(priority=k % _DMA_THREADS))

    for r in range(n_rows):
        row_copy(pos_ref, slot, r, r // TOP_K, r % TOP_K).wait()
    for r in range(n_rows):
        row_copy(nxt_ref, 1 - slot, r, r // TOP_K, r % TOP_K).start(priority=r % _DMA_THREADS)
    wt = wt_ref[...]
    ffn = wt[:, 0:1] * _load_row_tiles(buf_ref.at[slot, 0], tm)
    for k in range(1, TOP_K):
        ffn = ffn + wt[:, k:k + 1] * _load_row_tiles(buf_ref.at[slot, k], tm)
    o_ref[...] = _layer_norm(DEEPNORM_ALPHA * _load_row_tiles(h2t_ref, tm) + ffn, g_ref[...], b_ref[...])

    @pl.when(i + 1 == pl.num_programs(0))
    def _drain():
        _for_each_row(n_rows, lambda r, t, k: row_copy(nxt_ref, 1 - slot, r, t, k).wait())


def _combine(pos_flat, wts, h2t, g3, b3, ys, *, tm):
    t = h2t.shape[0] // ROW_PITCH
    n_steps = t // tm
    vec = pl.BlockSpec((1, D_MODEL), lambda i: (0, 0))
    return pl.pallas_call(
        functools.partial(_combine_kernel, tm=tm),
        out_shape=jax.ShapeDtypeStruct((t, D_MODEL), _F32),
        grid=(n_steps,),
        in_specs=[pl.BlockSpec((tm * TOP_K,), lambda i: (i,), memory_space=pltpu.SMEM),
                  pl.BlockSpec((tm * TOP_K,), lambda i: (jnp.minimum(i + 1, n_steps - 1),), memory_space=pltpu.SMEM),
                  pl.BlockSpec((tm, TOP_K), lambda i: (i, 0)),
                  pl.BlockSpec((tm * ROW_PITCH, LANES), lambda i: (i, 0)), vec, vec,
                  pl.BlockSpec(memory_space=pl.ANY)],
        out_specs=pl.BlockSpec((tm, D_MODEL), lambda i: (i, 0)),
        scratch_shapes=[pltpu.VMEM((2, TOP_K, tm * ROW_PITCH, LANES), _F32), pltpu.SemaphoreType.DMA((2,))],
        compiler_params=_cparams(("arbitrary",)),
        name="combine",
    )(pos_flat, pos_flat, wts, h2t, g3, b3, ys)


def _tiles(seq):
    big = 512 if seq % 512 == 0 else MOBA_BLOCK
    post = 1024 if seq % 1024 == 0 else big
    return dict(inproj=big, attn_q=MOBA_BLOCK, post=post, rank=big, dispatch=big, expert=512, combine=big)


def kernel(x, mem, w_in, diff_lambda_q1, diff_lambda_k1, diff_lambda_q2, diff_lambda_k2, diff_subln_g,
           w_mix_out, ln1_g, ln1_b, mem_ln_g, mem_ln_b, w_mem_q, w_mem_kv, w_mem_o, ln2_g, ln2_b, w_router,
           b_router, w_mlp1, b_mlp1, w_mlp2, b_mlp2, ln3_g, ln3_b):
    batch, seq, d = x.shape
    mem_len = mem.shape[1]
    assert d == D_MODEL and seq % MOBA_BLOCK == 0 and w_in.shape[0] == 1
    t = batch * seq
    ts = _tiles(seq)
    row = lambda v: v.reshape(1, -1).astype(_F32)

    col = jnp.arange(PROJ_WIDTH)
    is_q = (col < DIFF_WIDTH) | ((col >= 3 * DIFF_WIDTH) & (col < 3 * DIFF_WIDTH + MOBA_WIDTH))
    w_in_c = (w_in[0] * jnp.where(is_q, HEAD_DIM ** -0.5 * math.log2(math.e), 1.0)[None, :]).astype(_CDT)
    lam_vecs = jnp.stack([diff_lambda_q1[0], diff_lambda_k1[0], diff_lambda_q2[0], diff_lambda_k2[0]]).astype(_F32)
    w_r = jnp.zeros((D_MODEL, LANES), _F32).at[:, :N_EXPERTS].set(w_router[0]).astype(_CDT)
    b_r = jnp.full((1, LANES), _NEG, _F32).at[0, :N_EXPERTS].set(b_router[0])
    b1g = b_mlp1[0][:, None, 0::2].astype(_F32)
    b1l = b_mlp1[0][:, None, 1::2].astype(_F32)
    b2 = b_mlp2[0][:, None, :].astype(_F32)

    x2d = x.reshape(t, d)
    proj, kmean = _inproj(x2d, w_in_c, seq, tm=ts["inproj"])
    kmean = kmean.reshape(batch, seq // MOBA_BLOCK, MOBA_WIDTH)
    out_a = _diff_attn(proj, lam_vecs, row(diff_subln_g[0]), batch, seq, tq=ts["attn_q"])
    out_b = _moba_attn(proj, kmean, batch, seq)
    kv = _mem_kv(mem.reshape(batch * mem_len, d), row(mem_ln_g), row(mem_ln_b), w_mem_kv[0].astype(_CDT),
                 tm=mem_len)
    h2t, top_idx, top_w, counts = _post_attn(
        x2d, out_a, out_b, w_mix_out[0].astype(_CDT), row(ln1_g[0]), row(ln1_b[0]), kv,
        w_mem_q[0].astype(_CDT), w_mem_o[0].astype(_CDT), row(ln2_g[0]), row(ln2_b[0]), w_r, b_r,
        seq, mem_len, tm=ts["post"])

    tmx = ts["expert"]
    cnt = counts[0, :N_EXPERTS]
    padded = (cnt + tmx - 1) // tmx * tmx
    ends = jnp.cumsum(padded)
    offsets = ends - padded
    meta = jnp.concatenate([offsets, cnt, padded]).astype(jnp.int32)
    n_sorted_rows = t * TOP_K + N_EXPERTS * tmx
    n_tiles = n_sorted_rows // tmx
    tile_start = jnp.arange(n_tiles, dtype=jnp.int32) * tmx
    tile_expert = jnp.minimum(jnp.sum(tile_start[:, None] >= ends[None, :], axis=1), N_EXPERTS - 1).astype(jnp.int32)
    n_valid = (ends[-1:] // tmx).astype(jnp.int32)
    offsets_row = jnp.zeros((1, LANES), _F32).at[0, :N_EXPERTS].set(offsets.astype(_F32))
    pos = _rank(top_idx, offsets_row, tm=ts["rank"]).reshape(-1)

    xs = _dispatch(meta, h2t, pos, n_sorted_rows, tm=ts["dispatch"])
    ys = _experts(tile_expert, n_valid, xs, w_mlp1[0], b1g, b1l, w_mlp2[0], b2, tm=tmx)
    out = _combine(pos, top_w, h2t, row(ln3_g[0]), row(ln3_b[0]), ys, tm=ts["combine"])
    return out.reshape(batch, seq, d)
```

```python
import functools
import math

import jax
import jax.numpy as jnp
from jax import lax
from jax.experimental import pallas as pl
from jax.experimental.pallas import tpu as pltpu

D_MODEL = 1024
DIFF_HEADS = 4
HEAD_DIM = 64
DIFF_WIDTH = DIFF_HEADS * 2 * HEAD_DIM
MOBA_HEADS = 8
MOBA_WIDTH = MOBA_HEADS * HEAD_DIM
MOBA_BLOCK = 256
MOBA_TOPK = 3
ROPE_THETA = 500000.0
ROT_DIM = HEAD_DIM // 4
MEM_HEADS = 4
MEM_HEAD_DIM = D_MODEL // MEM_HEADS
N_EXPERTS = 32
TOP_K = 4
D_FF = D_MODEL
SWIGLU_ALPHA = 1.702
SWIGLU_LIMIT = 7.0
LN_EPS = 1e-5
RMS_EPS = 1e-5
DEEPNORM_ALPHA = 2.0 ** 0.25
LAMBDA_INIT = 0.8 - 0.6 * math.exp(0.0)

LANES = 128
ROW_TILE = 8
ROW_PITCH = ROW_TILE + 1
V7X_VMEM_LIMIT = 56 * 1024 * 1024

_CDT = jnp.bfloat16
_F32 = jnp.float32
_NEG = -1e30

_CB_DQ, _CB_DK, _CB_DV = 0, 4, 8
_CB_MQ, _CB_MK, _CB_MV = 12, 16, 20
PROJ_WIDTH = 3 * (DIFF_WIDTH + MOBA_WIDTH)


def _cparams(sem, vmem=V7X_VMEM_LIMIT):
    return pltpu.CompilerParams(dimension_semantics=sem, vmem_limit_bytes=vmem)


def _dot(a, b):
    return jnp.dot(a, b, preferred_element_type=_F32)


def _dot_nt(a, b):
    return lax.dot_general(a, b, (((1,), (1,)), ((), ())), preferred_element_type=_F32)


def _layer_norm(x, g, b):
    mu = jnp.mean(x, axis=-1, keepdims=True)
    xc = x - mu
    var = jnp.mean(xc * xc, axis=-1, keepdims=True)
    return xc * lax.rsqrt(var + LN_EPS) * g + b


_ROPE_BLOCKS = tuple(range(_CB_DQ, _CB_DV)) + tuple(range(_CB_MQ, _CB_MV))


def _inproj_kernel(x_ref, w_ref, cos_ref, sa_ref, sb_ref, o_ref, km_ref, *, tm):
    x = x_ref[...].astype(_CDT)
    cos, sa, sb = cos_ref[...], sa_ref[...], sb_ref[...]
    seg_w = 4 * LANES
    for seg in range(PROJ_WIDTH // seg_w):
        pseg = _dot(x, w_ref[:, seg * seg_w:(seg + 1) * seg_w])
        for c in range(4):
            cb = seg * 4 + c
            cols = slice(cb * LANES, (cb + 1) * LANES)
            p = pseg[:, c * LANES:(c + 1) * LANES]
            if cb in _ROPE_BLOCKS:
                p = p * cos + pltpu.roll(p, 8, 1) * sa + pltpu.roll(p, LANES - 8, 1) * sb
            o_ref[:, cols] = p.astype(o_ref.dtype)
            if _CB_MK <= cb < _CB_MV:
                kc = slice((cb - _CB_MK) * LANES, (cb - _CB_MK + 1) * LANES)
                for blk in range(tm // MOBA_BLOCK):
                    rows = p[blk * MOBA_BLOCK:(blk + 1) * MOBA_BLOCK, :]
                    km_ref[blk, :, kc] = jnp.sum(rows, axis=0, keepdims=True) * (1.0 / MOBA_BLOCK)


def _rope_tables(seq):
    half = ROT_DIM // 2
    inv_freq = ROPE_THETA ** (-jnp.arange(0, ROT_DIM, 2, dtype=_F32) / ROT_DIM)
    ang = jnp.arange(seq, dtype=_F32)[:, None] * inv_freq[None, :]
    cos, sin = jnp.cos(ang), jnp.sin(ang)
    lane = jnp.arange(LANES) % HEAD_DIM
    first, second = lane < half, (lane >= half) & (lane < ROT_DIM)
    idx = jnp.where(first, lane, jnp.where(second, lane - half, 0))
    cos_t = jnp.where((first | second)[None, :], cos[:, idx], 1.0)
    sa_t = jnp.where(second[None, :], sin[:, idx], 0.0)
    sb_t = jnp.where(first[None, :], -sin[:, idx], 0.0)
    return cos_t, sa_t, sb_t


def _inproj(x2d, w_in, seq, *, tm):
    t = x2d.shape[0]
    cos_t, sa_t, sb_t = _rope_tables(seq)
    n_pos = seq // tm
    tab_spec = pl.BlockSpec((tm, LANES), lambda i: (i % n_pos, 0))
    return pl.pallas_call(
        functools.partial(_inproj_kernel, tm=tm),
        out_shape=(jax.ShapeDtypeStruct((t, PROJ_WIDTH), _CDT),
                   jax.ShapeDtypeStruct((t // MOBA_BLOCK, 1, MOBA_WIDTH), _F32)),
        grid=(t // tm,),
        in_specs=[pl.BlockSpec((tm, D_MODEL), lambda i: (i, 0)),
                  pl.BlockSpec((D_MODEL, PROJ_WIDTH), lambda i: (0, 0)),
                  tab_spec, tab_spec, tab_spec],
        out_specs=(pl.BlockSpec((tm, PROJ_WIDTH), lambda i: (i, 0)),
                   pl.BlockSpec((tm // MOBA_BLOCK, 1, MOBA_WIDTH), lambda i: (i, 0, 0))),
        compiler_params=_cparams(("parallel",)),
        name="inproj",
    )(x2d, w_in, cos_t, sa_t, sb_t)


def _exp2_parts(parts):
    chunks = [s[:, c:c + LANES] for s in parts for c in range(0, s.shape[1], LANES)]
    m = jnp.max(functools.reduce(jnp.maximum, chunks), axis=1, keepdims=True)
    return [jnp.exp2(s - m).astype(_CDT) for s in parts]


_SCORES_AHEAD = 1


def _causal_mask(tq):
    row = lax.broadcasted_iota(jnp.int32, (tq, tq), 0)
    col = lax.broadcasted_iota(jnp.int32, (tq, tq), 1)
    return col <= row


def _diff_attn_kernel(q_ref, k_ref, v_ref, lam_ref, g_ref, o_ref, *, seq, tq):
    lane = lax.broadcasted_iota(jnp.int32, (tq, LANES), 1)
    mask = _causal_mask(tq)
    lv = lam_ref[...]
    lam = (jnp.exp(jnp.sum(lv[0:1] * lv[1:2], axis=1, keepdims=True))
           - jnp.exp(jnp.sum(lv[2:3] * lv[3:4], axis=1, keepdims=True)) + LAMBDA_INIT)
    gain = g_ref[...] * (1.0 - LAMBDA_INIT)

    def with_ones(v):
        return jnp.concatenate([v, jnp.ones_like(v)], axis=1)

    def scores(i):
        rows = slice(i * tq, (i + 1) * tq)
        q = q_ref[rows, :]
        zero = jnp.zeros_like(q)
        q1 = jnp.where(lane < HEAD_DIM, q, zero)
        q2 = jnp.where(lane >= HEAD_DIM, q, zero)
        kd = k_ref[rows, :]
        s1 = [jnp.where(mask, _dot_nt(q1, kd), _NEG)]
        s2 = [jnp.where(mask, _dot_nt(q2, kd), _NEG)]
        if i:
            kp = k_ref[:i * tq, :]
            s1.append(_dot_nt(q1, kp))
            s2.append(_dot_nt(q2, kp))
        return s1, s2

    def finish(i, s1, s2):
        rows = slice(i * tq, (i + 1) * tq)
        v_all = with_ones(v_ref[:(i + 1) * tq, :])
        nums = []
        for parts in (s1, s2):
            es = _exp2_parts(parts)
            nl = _dot(jnp.concatenate(es[1:] + es[:1], axis=1), v_all)
            nums.append(nl[:, :LANES] / nl[:, LANES:LANES + 1])
        o = nums[0] - lam * nums[1]
        o = o * lax.rsqrt(jnp.mean(o * o, axis=1, keepdims=True) + RMS_EPS) * gain
        o_ref[rows, :] = o.astype(o_ref.dtype)

    n_tiles = seq // tq
    pending = [scores(i) for i in range(min(_SCORES_AHEAD, n_tiles))]
    for i in range(n_tiles):
        if i + _SCORES_AHEAD < n_tiles:
            pending.append(scores(i + _SCORES_AHEAD))
        finish(i, *pending.pop(0))


def _diff_attn(proj, lam_vecs, subln_g, batch, seq, *, tq):
    t = batch * seq
    blk = lambda cb: pl.BlockSpec((seq, LANES), lambda b, h: (b, cb + h))
    return pl.pallas_call(
        functools.partial(_diff_attn_kernel, seq=seq, tq=tq),
        out_shape=jax.ShapeDtypeStruct((t, DIFF_WIDTH), _CDT),
        grid=(batch, DIFF_HEADS),
        in_specs=[blk(_CB_DQ), blk(_CB_DK), blk(_CB_DV),
                  pl.BlockSpec((4, HEAD_DIM), lambda b, h: (0, 0)),
                  pl.BlockSpec((1, 2 * HEAD_DIM), lambda b, h: (0, 0))],
        out_specs=pl.BlockSpec((seq, LANES), lambda b, h: (b, h)),
        compiler_params=_cparams(("parallel", "parallel")),
        name="diff_attn",
    )(proj, proj, proj, lam_vecs, subln_g)


def _moba_selection(gate, n_past):
    blk = lax.broadcasted_iota(jnp.int32, gate.shape, 0)
    rank = jnp.zeros(gate.shape, _F32)
    for m_blk in range(n_past):
        gm = gate[m_blk:m_blk + 1, :]
        beats = (gm > gate) | ((gm == gate) & (m_blk < blk))
        rank = rank + jnp.where(beats, 1.0, 0.0)
    return jnp.where((blk < n_past) & (rank < MOBA_TOPK), 1.0, 0.0)


def _moba_kernel(q_ref, k_ref, v_ref, km_ref, o_ref, *, nb):
    tq = MOBA_BLOCK
    lane = lax.broadcasted_iota(jnp.int32, (tq, LANES), 1)
    mask = _causal_mask(tq)
    km = km_ref[...].astype(_CDT)
    in_head = [(lane >= hh * HEAD_DIM) & (lane < (hh + 1) * HEAD_DIM) for hh in range(2)]
    seq_lane = lax.broadcasted_iota(jnp.int32, (nb * tq, LANES), 1)
    v_ones = [jnp.where((seq_lane >= hh * HEAD_DIM) & (seq_lane < (hh + 1) * HEAD_DIM), v_ref[...],
                        jnp.ones((nb * tq, LANES), _CDT)) for hh in range(2)]

    def scores(i, hh):
        rows = slice(i * tq, (i + 1) * tq)
        q = q_ref[rows, :]
        qh = jnp.where(in_head[hh], q, jnp.zeros_like(q))
        parts = [jnp.where(mask, _dot_nt(qh, k_ref[rows, :]), _NEG)]
        gated = i > MOBA_TOPK
        if gated:
            sel = _moba_selection(_dot_nt(km, qh), i)
            sel_q = jnp.concatenate([sel, jnp.zeros((LANES - nb, tq), _F32)], axis=0).T
        for j in range(i):
            s = _dot_nt(qh, k_ref[j * tq:(j + 1) * tq, :])
            parts.append(jnp.where(sel_q[:, j:j + 1] > 0.5, s, _NEG) if gated else s)
        return parts

    def finish(i, hh, parts):
        es = _exp2_parts(parts)
        acc = _dot(jnp.concatenate(es[1:] + es[:1], axis=1), v_ones[hh][:(i + 1) * tq, :])
        sum_lane = (1 - hh) * HEAD_DIM
        return acc / acc[:, sum_lane:sum_lane + 1]

    units = [(i, hh) for i in range(nb) for hh in range(2)]
    pending = [scores(*u) for u in units[:_SCORES_AHEAD]]
    outs = {}
    for n, (i, hh) in enumerate(units):
        if n + _SCORES_AHEAD < len(units):
            pending.append(scores(*units[n + _SCORES_AHEAD]))
        outs[hh] = finish(i, hh, pending.pop(0))
        if hh == 1:
            o_ref[i * tq:(i + 1) * tq, :] = jnp.where(in_head[0], outs[0], outs[1]).astype(o_ref.dtype)


def _moba_attn(proj, kmean, batch, seq):
    nb = seq // MOBA_BLOCK
    t = batch * seq
    blk = lambda cb: pl.BlockSpec((seq, LANES), lambda b, p: (b, cb + p))
    return pl.pallas_call(
        functools.partial(_moba_kernel, nb=nb),
        out_shape=jax.ShapeDtypeStruct((t, MOBA_WIDTH), _CDT),
        grid=(batch, MOBA_HEADS // 2),
        in_specs=[blk(_CB_MQ), blk(_CB_MK), blk(_CB_MV),
                  pl.BlockSpec((None, nb, LANES), lambda b, p: (b, 0, p))],
        out_specs=pl.BlockSpec((seq, LANES), lambda b, p: (b, p)),
        compiler_params=_cparams(("parallel", "parallel")),
        name="moba_attn",
    )(proj, proj, proj, kmean)


def _mem_kv_kernel(mem_ref, g_ref, b_ref, w_ref, o_ref):
    mn = _layer_norm(mem_ref[...], g_ref[...], b_ref[...])
    o_ref[...] = _dot(mn.astype(_CDT), w_ref[...]).astype(o_ref.dtype)


def _mem_kv(mem2d, g, b, w_kv, *, tm):
    rows = mem2d.shape[0]
    vec = pl.BlockSpec((1, D_MODEL), lambda i: (0, 0))
    return pl.pallas_call(
        _mem_kv_kernel,
        out_shape=jax.ShapeDtypeStruct((rows, 2 * D_MODEL), _CDT),
        grid=(rows // tm,),
        in_specs=[pl.BlockSpec((tm, D_MODEL), lambda i: (i, 0)), vec, vec,
                  pl.BlockSpec((D_MODEL, 2 * D_MODEL), lambda i: (0, 0))],
        out_specs=pl.BlockSpec((tm, 2 * D_MODEL), lambda i: (i, 0)),
        compiler_params=_cparams(("parallel",)),
        name="mem_kv",
    )(mem2d, g, b, w_kv)


def _store_row_tiles(ref, val):
    n = val.shape[0]
    for c in range(ROW_TILE):
        ref[pl.ds(c, n, stride=ROW_PITCH), :] = val[:, c * LANES:(c + 1) * LANES]
    ref[pl.ds(ROW_TILE, n, stride=ROW_PITCH), :] = jnp.zeros((n, LANES), val.dtype)


def _load_row_tiles(ref, n, pitch=ROW_PITCH):
    return jnp.concatenate([ref[pl.ds(c, n, stride=pitch), :] for c in range(ROW_TILE)], axis=1)


def _slab(ref, i):
    return ref.at[pl.ds(i * ROW_PITCH, ROW_PITCH)]


def _tile(ref, i, pitch):
    start = pl.multiple_of(i * ROW_TILE, ROW_TILE) if pitch == ROW_TILE else i * pitch
    return ref.at[pl.ds(start, ROW_TILE)]


_DONE = object()


def _interleave(stage_generators):
    live = list(stage_generators)
    while live:
        live = [g for g in live if next(g, _DONE) is not _DONE]


def _post_attn_kernel(x_ref, oa_ref, ob_ref, wout_ref, g1_ref, b1_ref, kv_ref, wq_ref, wo_ref,
                      g2_ref, b2_ref, wr_ref, br_ref, h2t_ref, idx_ref, wt_ref, cnt_ref, cnt_acc, *, tm, groups):
    @pl.when(pl.program_id(0) == 0)
    def _init_counts():
        cnt_acc[...] = jnp.zeros_like(cnt_acc)

    n = tm // groups
    lane = lax.broadcasted_iota(jnp.int32, (n, LANES), 1)
    lane_f = lane.astype(_F32)

    def row_group(g):
        rows = slice(g * n, (g + 1) * n)
        mix = _dot(oa_ref[rows, :], wout_ref[:DIFF_WIDTH, :]) + _dot(ob_ref[rows, :], wout_ref[DIFF_WIDTH:, :])
        yield
        h1 = _layer_norm(DEEPNORM_ALPHA * x_ref[rows, :] + mix, g1_ref[...], b1_ref[...])
        q = (_dot(h1.astype(_CDT), wq_ref[...]) * (MEM_HEAD_DIM ** -0.5)).astype(_CDT)
        yield
        heads = []
        for h in range(MEM_HEADS):
            c0 = h * MEM_HEAD_DIM
            kh = kv_ref[:, c0:c0 + MEM_HEAD_DIM]
            vh = kv_ref[:, D_MODEL + c0:D_MODEL + c0 + MEM_HEAD_DIM]
            s = _dot_nt(q[:, c0:c0 + MEM_HEAD_DIM], kh)
            p = jnp.exp(s - jnp.max(s, axis=1, keepdims=True))
            o = _dot(p.astype(_CDT), vh) / jnp.sum(p, axis=1, keepdims=True)
            heads.append(o.astype(_CDT))
        yield
        xatt = _dot(jnp.concatenate(heads, axis=1), wo_ref[...])
        yield
        h2 = _layer_norm(DEEPNORM_ALPHA * h1 + xatt, g2_ref[...], b2_ref[...])
        _store_row_tiles(h2t_ref.at[pl.ds(g * n * ROW_PITCH, n * ROW_PITCH)], h2)
        logits = _dot(h2.astype(_CDT), wr_ref[...]) + br_ref[...]
        yield
        vals, idxs = [], []
        work = logits
        for _ in range(TOP_K):
            m = jnp.max(work, axis=1, keepdims=True)
            ik = jnp.min(jnp.where(work == m, lane_f, float(LANES)), axis=1, keepdims=True)
            vals.append(m)
            idxs.append(ik)
            work = jnp.where(lane_f == ik, -jnp.inf, work)
        exps = [jnp.exp(v - vals[0]) for v in vals]
        denom = exps[0] + exps[1] + exps[2] + exps[3]
        idx_w = jnp.zeros((n, LANES), _F32)
        wt_w = jnp.zeros((n, LANES), _F32)
        for k in range(TOP_K):
            idx_w = jnp.where(lane == k, idxs[k], idx_w)
            wt_w = jnp.where(lane == k, exps[k] / denom, wt_w)
        idx_ref[rows, :] = idx_w[:, :TOP_K].astype(jnp.int32)
        wt_ref[rows, :] = wt_w[:, :TOP_K]
        chosen = functools.reduce(jnp.add, [jnp.where(lane_f == ik, 1.0, 0.0) for ik in idxs])
        cnt_acc[...] = cnt_acc[...] + jnp.sum(chosen, axis=0, keepdims=True)

    _interleave(row_group(g) for g in range(groups))
    cnt_ref[...] = cnt_acc[...].astype(jnp.int32)


def _post_attn(x2d, out_a, out_b, w_out, g1, b1, kv, w_q, w_o, g2, b2, w_r, b_r, seq, mem_len, *, tm):
    t = x2d.shape[0]
    per_b = seq // tm
    full = lambda shape: pl.BlockSpec(shape, lambda i: (0,) * len(shape))
    vec = full((1, D_MODEL))
    return pl.pallas_call(
        functools.partial(_post_attn_kernel, tm=tm, groups=max(1, tm // MOBA_BLOCK)),
        out_shape=(jax.ShapeDtypeStruct((t * ROW_PITCH, LANES), _F32),
                   jax.ShapeDtypeStruct((t, TOP_K), jnp.int32),
                   jax.ShapeDtypeStruct((t, TOP_K), _F32),
                   jax.ShapeDtypeStruct((1, LANES), jnp.int32)),
        grid=(t // tm,),
        in_specs=[pl.BlockSpec((tm, D_MODEL), lambda i: (i, 0)),
                  pl.BlockSpec((tm, DIFF_WIDTH), lambda i: (i, 0)),
                  pl.BlockSpec((tm, MOBA_WIDTH), lambda i: (i, 0)),
                  full((D_MODEL, D_MODEL)), vec, vec,
                  pl.BlockSpec((mem_len, 2 * D_MODEL), lambda i: (i // per_b, 0)),
                  full((D_MODEL, D_MODEL)), full((D_MODEL, D_MODEL)), vec, vec,
                  full((D_MODEL, LANES)), full((1, LANES))],
        out_specs=(pl.BlockSpec((tm * ROW_PITCH, LANES), lambda i: (i, 0)),
                   pl.BlockSpec((tm, TOP_K), lambda i: (i, 0)),
                   pl.BlockSpec((tm, TOP_K), lambda i: (i, 0)),
                   pl.BlockSpec((1, LANES), lambda i: (0, 0))),
        scratch_shapes=[pltpu.VMEM((1, LANES), _F32)],
        compiler_params=_cparams(("arbitrary",)),
        name="post_attn",
    )(x2d, out_a, out_b, w_out, g1, b1, kv, w_q, w_o, g2, b2, w_r, b_r)


def _rank_kernel(idx_ref, off_ref, pos_ref, next_ref, *, tm):
    @pl.when(pl.program_id(0) == 0)
    def _init():
        next_ref[...] = off_ref[...]

    idx = idx_ref[...]
    lane = lax.broadcasted_iota(jnp.int32, (tm, LANES), 1)
    hits = [lane == idx[:, k:k + 1] for k in range(TOP_K)]
    sel = jnp.zeros((tm, LANES), _F32)
    for hit in hits:
        sel = sel + jnp.where(hit, 1.0, 0.0)
    row = lax.broadcasted_iota(jnp.int32, (tm, tm), 0)
    col = lax.broadcasted_iota(jnp.int32, (tm, tm), 1)
    earlier = jnp.where(col < row, 1.0, 0.0).astype(jnp.bfloat16)
    dense = _dot(earlier, sel.astype(jnp.bfloat16)) + next_ref[...]
    pos_w = jnp.zeros((tm, LANES), _F32)
    for k, hit in enumerate(hits):
        pk = jnp.sum(jnp.where(hit, dense, 0.0), axis=1, keepdims=True)
        pos_w = jnp.where(lane == k, pk, pos_w)
    pos_ref[...] = pos_w[:, :TOP_K].astype(jnp.int32)
    next_ref[...] = next_ref[...] + jnp.sum(sel, axis=0, keepdims=True)


def _rank(idx, offsets, *, tm):
    t = idx.shape[0]
    return pl.pallas_call(
        functools.partial(_rank_kernel, tm=tm),
        out_shape=jax.ShapeDtypeStruct((t, TOP_K), jnp.int32),
        grid=(t // tm,),
        in_specs=[pl.BlockSpec((tm, TOP_K), lambda i: (i, 0)), pl.BlockSpec((1, LANES), lambda i: (0, 0))],
        out_specs=pl.BlockSpec((tm, TOP_K), lambda i: (i, 0)),
        scratch_shapes=[pltpu.VMEM((1, LANES), _F32)],
        compiler_params=_cparams(("arbitrary",)),
        name="rank",
    )(idx, offsets)


_DMA_UNROLL = 8
_DMA_THREADS = 2


def _for_each_row(n_rows, fn):
    tokens_per_group = _DMA_UNROLL // TOP_K

    def group(g, c):
        for u in range(_DMA_UNROLL):
            fn(g * _DMA_UNROLL + u, g * tokens_per_group + u // TOP_K, u % TOP_K)
        return c

    lax.fori_loop(0, n_rows // _DMA_UNROLL, group, 0)


def _dispatch_kernel(meta_ref, h_ref, pos_ref, xs_ref, zero_ref, sem, zsem, *, tm):
    n_rows = tm * TOP_K

    def zero_copy(r):
        return pltpu.make_async_copy(zero_ref, _tile(xs_ref, r, ROW_TILE), zsem)

    @pl.when(pl.program_id(0) == 0)
    def _zero_fill():
        zero_ref[...] = jnp.zeros_like(zero_ref)

        def pad_rows(e):
            return meta_ref[e] + meta_ref[N_EXPERTS + e], meta_ref[e] + meta_ref[2 * N_EXPERTS + e]

        for thread in range(_DMA_THREADS):
            def start_fills(j, c, thread=thread):
                lo, hi = pad_rows(j * _DMA_THREADS + thread)
                lax.fori_loop(lo, hi, lambda r, c2: (zero_copy(r).start(priority=thread), c2)[1], 0)
                return c

            lax.fori_loop(0, N_EXPERTS // _DMA_THREADS, start_fills, 0)

        def wait_fills(e, c):
            lo, hi = pad_rows(e)
            lax.fori_loop(lo, hi, lambda r, c2: (zero_copy(r).wait(), c2)[1], 0)
            return c

        lax.fori_loop(0, N_EXPERTS, wait_fills, 0)

    def row_copy(r, t, k):
        return pltpu.make_async_copy(_tile(h_ref, t, ROW_PITCH), _tile(xs_ref, pos_ref[r], ROW_TILE), sem)

    _for_each_row(n_rows, lambda r, t, k: row_copy(r, t, k).start(priority=k % _DMA_THREADS))
    _for_each_row(n_rows, lambda r, t, k: row_copy(r, t, k).wait())


def _dispatch(meta, h2t, pos_flat, n_sorted_rows, *, tm):
    t = h2t.shape[0] // ROW_PITCH
    return pl.pallas_call(
        functools.partial(_dispatch_kernel, tm=tm),
        out_shape=jax.ShapeDtypeStruct((n_sorted_rows * ROW_TILE, LANES), _F32),
        grid_spec=pltpu.PrefetchScalarGridSpec(
            num_scalar_prefetch=1,
            grid=(t // tm,),
            in_specs=[pl.BlockSpec((tm * ROW_PITCH, LANES), lambda i, meta: (i, 0)),
                      pl.BlockSpec((tm * TOP_K,), lambda i, meta: (i,), memory_space=pltpu.SMEM)],
            out_specs=pl.BlockSpec(memory_space=pl.ANY),
            scratch_shapes=[pltpu.VMEM((ROW_TILE, LANES), _F32),
                            pltpu.SemaphoreType.DMA, pltpu.SemaphoreType.DMA]),
        compiler_params=_cparams(("arbitrary",)),
        name="dispatch",
    )(meta, h2t, pos_flat)


_DEINT = 2 * LANES


def _deinterleave_matrix():
    k = jnp.arange(_DEINT)[:, None]
    n = jnp.arange(_DEINT)[None, :]
    src = jnp.where(n < LANES, 2 * n, 2 * (n - LANES) + 1)
    return (k == src).astype(_CDT)


def _experts_kernel(te_ref, nv_ref, xs_ref, perm_ref, w1_ref, b1g_ref, b1l_ref, w2_ref, b2_ref, y_ref,
                    wg_ref, wl_ref, w2c_ref, *, tm):
    i = pl.program_id(0)
    live = i < nv_ref[0]
    new_expert = (i == 0) | (te_ref[i] != te_ref[jnp.maximum(i - 1, 0)])

    @pl.when(live & new_expert)
    def _stage_weights():
        for c in range(2 * D_FF // _DEINT):
            both = _dot(w1_ref[0, :, c * _DEINT:(c + 1) * _DEINT].astype(_CDT), perm_ref[...])
            wg_ref[:, c * LANES:(c + 1) * LANES] = both[:, :LANES].astype(_CDT)
            wl_ref[:, c * LANES:(c + 1) * LANES] = both[:, LANES:].astype(_CDT)
        w2c_ref[...] = w2_ref[0].astype(_CDT)

    @pl.when(live)
    def _mlp():
        x = _load_row_tiles(xs_ref, tm, ROW_TILE).astype(_CDT)
        glu = jnp.minimum(_dot(x, wg_ref[...]) + b1g_ref[0], SWIGLU_LIMIT)
        lin = jnp.clip(_dot(x, wl_ref[...]) + b1l_ref[0], -SWIGLU_LIMIT, SWIGLU_LIMIT)
        act = glu * jax.nn.sigmoid(SWIGLU_ALPHA * glu) * (lin + 1.0)
        _store_row_tiles(y_ref, _dot(act.astype(_CDT), w2c_ref[...]) + b2_ref[0])

    @pl.when(jnp.logical_not(live))
    def _unused_tile():
        y_ref[...] = jnp.zeros_like(y_ref)


def _experts(tile_expert, n_valid, xs, w1, b1g, b1l, w2, b2, *, tm):
    n_tiles = xs.shape[0] // (tm * ROW_TILE)
    by_expert = lambda shape: pl.BlockSpec((1,) + shape, lambda i, te, nv: (te[i], 0, 0))
    return pl.pallas_call(
        functools.partial(_experts_kernel, tm=tm),
        out_shape=jax.ShapeDtypeStruct((n_tiles * tm * ROW_PITCH, LANES), _F32),
        grid_spec=pltpu.PrefetchScalarGridSpec(
            num_scalar_prefetch=2,
            grid=(n_tiles,),
            in_specs=[pl.BlockSpec((tm * ROW_TILE, LANES), lambda i, te, nv: (jnp.minimum(i, nv[0] - 1), 0)),
                      pl.BlockSpec((_DEINT, _DEINT), lambda i, te, nv: (0, 0)),
                      by_expert((D_MODEL, 2 * D_FF)),
                      by_expert((1, D_FF)), by_expert((1, D_FF)),
                      by_expert((D_FF, D_MODEL)), by_expert((1, D_MODEL))],
            out_specs=pl.BlockSpec((tm * ROW_PITCH, LANES), lambda i, te, nv: (i, 0)),
            scratch_shapes=[pltpu.VMEM((D_MODEL, D_FF), _CDT), pltpu.VMEM((D_MODEL, D_FF), _CDT),
                            pltpu.VMEM((D_FF, D_MODEL), _CDT)]),
        compiler_params=_cparams(("arbitrary",)),
        name="experts",
    )(tile_expert, n_valid, xs, _deinterleave_matrix(), w1, b1g, b1l, w2, b2)


def _combine_kernel(pos_ref, nxt_ref, wt_ref, h2t_ref, g_ref, b_ref, ys_ref, o_ref, buf_ref, sem, *, tm):
    n_rows = tm * TOP_K
    i = pl.program_id(0)
    slot = i & 1

    def row_copy(p_ref, s, r, t, k):
        return pltpu.make_async_copy(_slab(ys_ref, p_ref[r]), _slab(buf_ref.at[s, k], t), sem.at[s])

    @pl.when(i == 0)
    def _first():
        _for_each_row(n_rows, lambda r, t, k: row_copy(pos_ref, 0, r, t, k).start(priority=k % _DMA_THREADS))

    for r in range(n_rows):
        row_copy(pos_ref, slot, r, r // TOP_K, r % TOP_K).wait()
    for r in range(n_rows):
        row_copy(nxt_ref, 1 - slot, r, r // TOP_K, r % TOP_K).start(priority=r % _DMA_THREADS)
    wt = wt_ref[...]
    ffn = wt[:, 0:1] * _load_row_tiles(buf_ref.at[slot, 0], tm)
    for k in range(1, TOP_K):
        ffn = ffn + wt[:, k:k + 1] * _load_row_tiles(buf_ref.at[slot, k], tm)
    o_ref[...] = _layer_norm(DEEPNORM_ALPHA * _load_row_tiles(h2t_ref, tm) + ffn, g_ref[...], b_ref[...])

    @pl.when(i + 1 == pl.num_programs(0))
    def _drain():
        _for_each_row(n_rows, lambda r, t, k: row_copy(nxt_ref, 1 - slot, r, t, k).wait())


def _combine(pos_flat, wts, h2t, g3, b3, ys, *, tm):
    t = h2t.shape[0] // ROW_PITCH
    n_steps = t // tm
    vec = pl.BlockSpec((1, D_MODEL), lambda i: (0, 0))
    return pl.pallas_call(
        functools.partial(_combine_kernel, tm=tm),
        out_shape=jax.ShapeDtypeStruct((t, D_MODEL), _F32),
        grid=(n_steps,),
        in_specs=[pl.BlockSpec((tm * TOP_K,), lambda i: (i,), memory_space=pltpu.SMEM),
                  pl.BlockSpec((tm * TOP_K,), lambda i: (jnp.minimum(i + 1, n_steps - 1),), memory_space=pltpu.SMEM),
                  pl.BlockSpec((tm, TOP_K), lambda i: (i, 0)),
                  pl.BlockSpec((tm * ROW_PITCH, LANES), lambda i: (i, 0)), vec, vec,
                  pl.BlockSpec(memory_space=pl.ANY)],
        out_specs=pl.BlockSpec((tm, D_MODEL), lambda i: (i, 0)),
        scratch_shapes=[pltpu.VMEM((2, TOP_K, tm * ROW_PITCH, LANES), _F32), pltpu.SemaphoreType.DMA((2,))],
        compiler_params=_cparams(("arbitrary",)),
        name="combine",
    )(pos_flat, pos_flat, wts, h2t, g3, b3, ys)


def _tiles(seq):
    big = 512 if seq % 512 == 0 else MOBA_BLOCK
    post = 1024 if seq % 1024 == 0 else big
    return dict(inproj=big, attn_q=MOBA_BLOCK, post=post, rank=big, dispatch=big, expert=512, combine=big)


def kernel(x, mem, w_in, diff_lambda_q1, diff_lambda_k1, diff_lambda_q2, diff_lambda_k2, diff_subln_g,
           w_mix_out, ln1_g, ln1_b, mem_ln_g, mem_ln_b, w_mem_q, w_mem_kv, w_mem_o, ln2_g, ln2_b, w_router,
           b_router, w_mlp1, b_mlp1, w_mlp2, b_mlp2, ln3_g, ln3_b):
    batch, seq, d = x.shape
    mem_len = mem.shape[1]
    assert d == D_MODEL and seq % MOBA_BLOCK == 0 and w_in.shape[0] == 1
    t = batch * seq
    ts = _tiles(seq)
    row = lambda v: v.reshape(1, -1).astype(_F32)

    col = jnp.arange(PROJ_WIDTH)
    is_q = (col < DIFF_WIDTH) | ((col >= 3 * DIFF_WIDTH) & (col < 3 * DIFF_WIDTH + MOBA_WIDTH))
    w_in_c = (w_in[0] * jnp.where(is_q, HEAD_DIM ** -0.5 * math.log2(math.e), 1.0)[None, :]).astype(_CDT)
    lam_vecs = jnp.stack([diff_lambda_q1[0], diff_lambda_k1[0], diff_lambda_q2[0], diff_lambda_k2[0]]).astype(_F32)
    w_r = jnp.zeros((D_MODEL, LANES), _F32).at[:, :N_EXPERTS].set(w_router[0]).astype(_CDT)
    b_r = jnp.full((1, LANES), _NEG, _F32).at[0, :N_EXPERTS].set(b_router[0])
    b1g = b_mlp1[0][:, None, 0::2].astype(_F32)
    b1l = b_mlp1[0][:, None, 1::2].astype(_F32)
    b2 = b_mlp2[0][:, None, :].astype(_F32)

    x2d = x.reshape(t, d)
    proj, kmean = _inproj(x2d, w_in_c, seq, tm=ts["inproj"])
    kmean = kmean.reshape(batch, seq // MOBA_BLOCK, MOBA_WIDTH)
    out_a = _diff_attn(proj, lam_vecs, row(diff_subln_g[0]), batch, seq, tq=ts["attn_q"])
    out_b = _moba_attn(proj, kmean, batch, seq)
    kv = _mem_kv(mem.reshape(batch * mem_len, d), row(mem_ln_g), row(mem_ln_b), w_mem_kv[0].astype(_CDT),
                 tm=mem_len)
    h2t, top_idx, top_w, counts = _post_attn(
        x2d, out_a, out_b, w_mix_out[0].astype(_CDT), row(ln1_g[0]), row(ln1_b[0]), kv,
        w_mem_q[0].astype(_CDT), w_mem_o[0].astype(_CDT), row(ln2_g[0]), row(ln2_b[0]), w_r, b_r,
        seq, mem_len, tm=ts["post"])

    tmx = ts["expert"]
    cnt = counts[0, :N_EXPERTS]
    padded = (cnt + tmx - 1) // tmx * tmx
    ends = jnp.cumsum(padded)
    offsets = ends - padded
    meta = jnp.concatenate([offsets, cnt, padded]).astype(jnp.int32)
    n_sorted_rows = t * TOP_K + N_EXPERTS * tmx
    n_tiles = n_sorted_rows // tmx
    tile_start = jnp.arange(n_tiles, dtype=jnp.int32) * tmx
    tile_expert = jnp.minimum(jnp.sum(tile_start[:, None] >= ends[None, :], axis=1), N_EXPERTS - 1).astype(jnp.int32)
    n_valid = (ends[-1:] // tmx).astype(jnp.int32)
    offsets_row = jnp.zeros((1, LANES), _F32).at[0, :N_EXPERTS].set(offsets.astype(_F32))
    pos = _rank(top_idx, offsets_row, tm=ts["rank"]).reshape(-1)

    xs = _dispatch(meta, h2t, pos, n_sorted_rows, tm=ts["dispatch"])
    ys = _experts(tile_expert, n_valid, xs, w_mlp1[0], b1g, b1l, w_mlp2[0], b2, tm=tmx)
    out = _combine(pos, top_w, h2t, row(ln3_g[0]), row(ln3_b[0]), ys, tm=ts["combine"])
    return out.reshape(batch, seq, d)
```

```python
import functools
import math

import jax
import jax.numpy as jnp
from jax import lax
from jax.experimental import pallas as pl
from jax.experimental.pallas import tpu as pltpu

D_MODEL = 1024
DIFF_HEADS = 4
HEAD_DIM = 64
DIFF_WIDTH = DIFF_HEADS * 2 * HEAD_DIM
MOBA_HEADS = 8
MOBA_WIDTH = MOBA_HEADS * HEAD_DIM
MOBA_BLOCK = 256
MOBA_TOPK = 3
ROPE_THETA = 500000.0
ROT_DIM = HEAD_DIM // 4
MEM_HEADS = 4
MEM_HEAD_DIM = D_MODEL // MEM_HEADS
N_EXPERTS = 32
TOP_K = 4
D_FF = D_MODEL
SWIGLU_ALPHA = 1.702
SWIGLU_LIMIT = 7.0
LN_EPS = 1e-5
RMS_EPS = 1e-5
DEEPNORM_ALPHA = 2.0 ** 0.25
LAMBDA_INIT = 0.8 - 0.6 * math.exp(0.0)

LANES = 128
ROW_TILE = 8
ROW_PITCH = ROW_TILE + 1
V7X_VMEM_LIMIT = 56 * 1024 * 1024

_CDT = jnp.bfloat16
_F32 = jnp.float32
_NEG = -1e30

_CB_DQ, _CB_DK, _CB_DV = 0, 4, 8
_CB_MQ, _CB_MK, _CB_MV = 12, 16, 20
PROJ_WIDTH = 3 * (DIFF_WIDTH + MOBA_WIDTH)


def _cparams(sem, vmem=V7X_VMEM_LIMIT):
    return pltpu.CompilerParams(dimension_semantics=sem, vmem_limit_bytes=vmem)


def _dot(a, b):
    return jnp.dot(a, b, preferred_element_type=_F32)


def _dot_nt(a, b):
    return lax.dot_general(a, b, (((1,), (1,)), ((), ())), preferred_element_type=_F32)


def _layer_norm(x, g, b):
    mu = jnp.mean(x, axis=-1, keepdims=True)
    xc = x - mu
    var = jnp.mean(xc * xc, axis=-1, keepdims=True)
    return xc * lax.rsqrt(var + LN_EPS) * g + b


_ROPE_BLOCKS = tuple(range(_CB_DQ, _CB_DV)) + tuple(range(_CB_MQ, _CB_MV))


def _inproj_kernel(x_ref, w_ref, cos_ref, sa_ref, sb_ref, o_ref, km_ref, *, tm):
    x = x_ref[...].astype(_CDT)
    cos, sa, sb = cos_ref[...], sa_ref[...], sb_ref[...]
    seg_w = 4 * LANES
    for seg in range(PROJ_WIDTH // seg_w):
        pseg = _dot(x, w_ref[:, seg * seg_w:(seg + 1) * seg_w])
        for c in range(4):
            cb = seg * 4 + c
            cols = slice(cb * LANES, (cb + 1) * LANES)
            p = pseg[:, c * LANES:(c + 1) * LANES]
            if cb in _ROPE_BLOCKS:
                p = p * cos + pltpu.roll(p, 8, 1) * sa + pltpu.roll(p, LANES - 8, 1) * sb
            o_ref[:, cols] = p.astype(o_ref.dtype)
            if _CB_MK <= cb < _CB_MV:
                kc = slice((cb - _CB_MK) * LANES, (cb - _CB_MK + 1) * LANES)
                for blk in range(tm // MOBA_BLOCK):
                    rows = p[blk * MOBA_BLOCK:(blk + 1) * MOBA_BLOCK, :]
                    km_ref[blk, :, kc] = jnp.sum(rows, axis=0, keepdims=True) * (1.0 / MOBA_BLOCK)


def _rope_tables(seq):
    half = ROT_DIM // 2
    inv_freq = ROPE_THETA ** (-jnp.arange(0, ROT_DIM, 2, dtype=_F32) / ROT_DIM)
    ang = jnp.arange(seq, dtype=_F32)[:, None] * inv_freq[None, :]
    cos, sin = jnp.cos(ang), jnp.sin(ang)
    lane = jnp.arange(LANES) % HEAD_DIM
    first, second = lane < half, (lane >= half) & (lane < ROT_DIM)
    idx = jnp.where(first, lane, jnp.where(second, lane - half, 0))
    cos_t = jnp.where((first | second)[None, :], cos[:, idx], 1.0)
    sa_t = jnp.where(second[None, :], sin[:, idx], 0.0)
    sb_t = jnp.where(first[None, :], -sin[:, idx], 0.0)
    return cos_t, sa_t, sb_t


def _inproj(x2d, w_in, seq, *, tm):
    t = x2d.shape[0]
    cos_t, sa_t, sb_t = _rope_tables(seq)
    n_pos = seq // tm
    tab_spec = pl.BlockSpec((tm, LANES), lambda i: (i % n_pos, 0))
    return pl.pallas_call(
        functools.partial(_inproj_kernel, tm=tm),
        out_shape=(jax.ShapeDtypeStruct((t, PROJ_WIDTH), _CDT),
                   jax.ShapeDtypeStruct((t // MOBA_BLOCK, 1, MOBA_WIDTH), _F32)),
        grid=(t // tm,),
        in_specs=[pl.BlockSpec((tm, D_MODEL), lambda i: (i, 0)),
                  pl.BlockSpec((D_MODEL, PROJ_WIDTH), lambda i: (0, 0)),
                  tab_spec, tab_spec, tab_spec],
        out_specs=(pl.BlockSpec((tm, PROJ_WIDTH), lambda i: (i, 0)),
                   pl.BlockSpec((tm // MOBA_BLOCK, 1, MOBA_WIDTH), lambda i: (i, 0, 0))),
        compiler_params=_cparams(("parallel",)),
        name="inproj",
    )(x2d, w_in, cos_t, sa_t, sb_t)


def _exp2_parts(parts):
    chunks = [s[:, c:c + LANES] for s in parts for c in range(0, s.shape[1], LANES)]
    m = jnp.max(functools.reduce(jnp.maximum, chunks), axis=1, keepdims=True)
    return [jnp.exp2(s - m).astype(_CDT) for s in parts]


_SCORES_AHEAD = 1


def _causal_mask(tq):
    row = lax.broadcasted_iota(jnp.int32, (tq, tq), 0)
    col = lax.broadcasted_iota(jnp.int32, (tq, tq), 1)
    return col <= row


def _diff_attn_kernel(q_ref, k_ref, v_ref, lam_ref, g_ref, o_ref, *, seq, tq):
    lane = lax.broadcasted_iota(jnp.int32, (tq, LANES), 1)
    mask = _causal_mask(tq)
    lv = lam_ref[...]
    lam = (jnp.exp(jnp.sum(lv[0:1] * lv[1:2], axis=1, keepdims=True))
           - jnp.exp(jnp.sum(lv[2:3] * lv[3:4], axis=1, keepdims=True)) + LAMBDA_INIT)
    gain = g_ref[...] * (1.0 - LAMBDA_INIT)

    def with_ones(v):
        return jnp.concatenate([v, jnp.ones_like(v)], axis=1)

    def scores(i):
        rows = slice(i * tq, (i + 1) * tq)
        q = q_ref[rows, :]
        zero = jnp.zeros_like(q)
        q1 = jnp.where(lane < HEAD_DIM, q, zero)
        q2 = jnp.where(lane >= HEAD_DIM, q, zero)
        kd = k_ref[rows, :]
        s1 = [jnp.where(mask, _dot_nt(q1, kd), _NEG)]
        s2 = [jnp.where(mask, _dot_nt(q2, kd), _NEG)]
        if i:
            kp = k_ref[:i * tq, :]
            s1.append(_dot_nt(q1, kp))
            s2.append(_dot_nt(q2, kp))
        return s1, s2

    def finish(i, s1, s2):
        rows = slice(i * tq, (i + 1) * tq)
        v_all = with_ones(v_ref[:(i + 1) * tq, :])
        nums = []
        for parts in (s1, s2):
            es = _exp2_parts(parts)
            nl = _dot(jnp.concatenate(es[1:] + es[:1], axis=1), v_all)
            nums.append(nl[:, :LANES] / nl[:, LANES:LANES + 1])
        o = nums[0] - lam * nums[1]
        o = o * lax.rsqrt(jnp.mean(o * o, axis=1, keepdims=True) + RMS_EPS) * gain
        o_ref[rows, :] = o.astype(o_ref.dtype)

    n_tiles = seq // tq
    pending = [scores(i) for i in range(min(_SCORES_AHEAD, n_tiles))]
    for i in range(n_tiles):
        if i + _SCORES_AHEAD < n_tiles:
            pending.append(scores(i + _SCORES_AHEAD))
        finish(i, *pending.pop(0))


def _diff_attn(proj, lam_vecs, subln_g, batch, seq, *, tq):
    t = batch * seq
    blk = lambda cb: pl.BlockSpec((seq, LANES), lambda b, h: (b, cb + h))
    return pl.pallas_call(
        functools.partial(_diff_attn_kernel, seq=seq, tq=tq),
        out_shape=jax.ShapeDtypeStruct((t, DIFF_WIDTH), _CDT),
        grid=(batch, DIFF_HEADS),
        in_specs=[blk(_CB_DQ), blk(_CB_DK), blk(_CB_DV),
                  pl.BlockSpec((4, HEAD_DIM), lambda b, h: (0, 0)),
                  pl.BlockSpec((1, 2 * HEAD_DIM), lambda b, h: (0, 0))],
        out_specs=pl.BlockSpec((seq, LANES), lambda b, h: (b, h)),
        compiler_params=_cparams(("parallel", "parallel")),
        name="diff_attn",
    )(proj, proj, proj, lam_vecs, subln_g)


def _moba_selection(gate, n_past):
    blk = lax.broadcasted_iota(jnp.int32, gate.shape, 0)
    rank = jnp.zeros(gate.shape, _F32)
    for m_blk in range(n_past):
        gm = gate[m_blk:m_blk + 1, :]
        beats = (gm > gate) | ((gm == gate) & (m_blk < blk))
        rank = rank + jnp.where(beats, 1.0, 0.0)
    return jnp.where((blk < n_past) & (rank < MOBA_TOPK), 1.0, 0.0)


def _moba_kernel(q_ref, k_ref, v_ref, km_ref, o_ref, *, nb):
    tq = MOBA_BLOCK
    lane = lax.broadcasted_iota(jnp.int32, (tq, LANES), 1)
    mask = _causal_mask(tq)
    km = km_ref[...].astype(_CDT)
    in_head = [(lane >= hh * HEAD_DIM) & (lane < (hh + 1) * HEAD_DIM) for hh in range(2)]
    seq_lane = lax.broadcasted_iota(jnp.int32, (nb * tq, LANES), 1)
    v_ones = [jnp.where((seq_lane >= hh * HEAD_DIM) & (seq_lane < (hh + 1) * HEAD_DIM), v_ref[...],
                        jnp.ones((nb * tq, LANES), _CDT)) for hh in range(2)]

    def scores(i, hh):
        rows = slice(i * tq, (i + 1) * tq)
        q = q_ref[rows, :]
        qh = jnp.where(in_head[hh], q, jnp.zeros_like(q))
        parts = [jnp.where(mask, _dot_nt(qh, k_ref[rows, :]), _NEG)]
        gated = i > MOBA_TOPK
        if gated:
            sel = _moba_selection(_dot_nt(km, qh), i)
            sel_q = jnp.concatenate([sel, jnp.zeros((LANES - nb, tq), _F32)], axis=0).T
        for j in range(i):
            s = _dot_nt(qh, k_ref[j * tq:(j + 1) * tq, :])
            parts.append(jnp.where(sel_q[:, j:j + 1] > 0.5, s, _NEG) if gated else s)
        return parts

    def finish(i, hh, parts):
        es = _exp2_parts(parts)
        acc = _dot(jnp.concatenate(es[1:] + es[:1], axis=1), v_ones[hh][:(i + 1) * tq, :])
        sum_lane = (1 - hh) * HEAD_DIM
        return acc / acc[:, sum_lane:sum_lane + 1]

    units = [(i, hh) for i in range(nb) for hh in range(2)]
    pending = [scores(*u) for u in units[:_SCORES_AHEAD]]
    outs = {}
    for n, (i, hh) in enumerate(units):
        if n + _SCORES_AHEAD < len(units):
            pending.append(scores(*units[n + _SCORES_AHEAD]))
        outs[hh] = finish(i, hh, pending.pop(0))
        if hh == 1:
            o_ref[i * tq:(i + 1) * tq, :] = jnp.where(in_head[0], outs[0], outs[1]).astype(o_ref.dtype)


def _moba_attn(proj, kmean, batch, seq):
    nb = seq // MOBA_BLOCK
    t = batch * seq
    blk = lambda cb: pl.BlockSpec((seq, LANES), lambda b, p: (b, cb + p))
    return pl.pallas_call(
        functools.partial(_moba_kernel, nb=nb),
        out_shape=jax.ShapeDtypeStruct((t, MOBA_WIDTH), _CDT),
        grid=(batch, MOBA_HEADS // 2),
        in_specs=[blk(_CB_MQ), blk(_CB_MK), blk(_CB_MV),
                  pl.BlockSpec((None, nb, LANES), lambda b, p: (b, 0, p))],
        out_specs=pl.BlockSpec((seq, LANES), lambda b, p: (b, p)),
        compiler_params=_cparams(("parallel", "parallel")),
        name="moba_attn",
    )(proj, proj, proj, kmean)


def _mem_kv_kernel(mem_ref, g_ref, b_ref, w_ref, o_ref):
    mn = _layer_norm(mem_ref[...], g_ref[...], b_ref[...])
    o_ref[...] = _dot(mn.astype(_CDT), w_ref[...]).astype(o_ref.dtype)


def _mem_kv(mem2d, g, b, w_kv, *, tm):
    rows = mem2d.shape[0]
    vec = pl.BlockSpec((1, D_MODEL), lambda i: (0, 0))
    return pl.pallas_call(
        _mem_kv_kernel,
        out_shape=jax.ShapeDtypeStruct((rows, 2 * D_MODEL), _CDT),
        grid=(rows // tm,),
        in_specs=[pl.BlockSpec((tm, D_MODEL), lambda i: (i, 0)), vec, vec,
                  pl.BlockSpec((D_MODEL, 2 * D_MODEL), lambda i: (0, 0))],
        out_specs=pl.BlockSpec((tm, 2 * D_MODEL), lambda i: (i, 0)),
        compiler_params=_cparams(("parallel",)),
        name="mem_kv",
    )(mem2d, g, b, w_kv)


def _store_row_tiles(ref, val):
    n = val.shape[0]
    for c in range(ROW_TILE):
        ref[pl.ds(c, n, stride=ROW_PITCH), :] = val[:, c * LANES:(c + 1) * LANES]
    ref[pl.ds(ROW_TILE, n, stride=ROW_PITCH), :] = jnp.zeros((n, LANES), val.dtype)


def _load_row_tiles(ref, n, pitch=ROW_PITCH):
    return jnp.concatenate([ref[pl.ds(c, n, stride=pitch), :] for c in range(ROW_TILE)], axis=1)


def _slab(ref, i):
    return ref.at[pl.ds(i * ROW_PITCH, ROW_PITCH)]


def _tile(ref, i, pitch):
    start = pl.multiple_of(i * ROW_TILE, ROW_TILE) if pitch == ROW_TILE else i * pitch
    return ref.at[pl.ds(start, ROW_TILE)]


_DONE = object()


def _interleave(stage_generators):
    live = list(stage_generators)
    while live:
        live = [g for g in live if next(g, _DONE) is not _DONE]


def _post_attn_kernel(x_ref, oa_ref, ob_ref, wout_ref, g1_ref, b1_ref, mem_ref, gm_ref, bm_ref, wkv_ref, wq_ref,
                      wo_ref, g2_ref, b2_ref, wr_ref, br_ref, h2t_ref, idx_ref, wt_ref, cnt_ref, cnt_acc, kv_ref,
                      *, tm, groups, steps_per_batch):
    @pl.when(pl.program_id(0) == 0)
    def _init_counts():
        cnt_acc[...] = jnp.zeros_like(cnt_acc)

    @pl.when(pl.program_id(0) % steps_per_batch == 0)
    def _memory_kv():
        mn = _layer_norm(mem_ref[...], gm_ref[...], bm_ref[...])
        kv_ref[...] = _dot(mn.astype(_CDT), wkv_ref[...]).astype(kv_ref.dtype)

    n = tm // groups
    lane = lax.broadcasted_iota(jnp.int32, (n, LANES), 1)
    lane_f = lane.astype(_F32)

    def row_group(g):
        rows = slice(g * n, (g + 1) * n)
        mix = _dot(oa_ref[rows, :], wout_ref[:DIFF_WIDTH, :]) + _dot(ob_ref[rows, :], wout_ref[DIFF_WIDTH:, :])
        yield
        h1 = _layer_norm(DEEPNORM_ALPHA * x_ref[rows, :] + mix, g1_ref[...], b1_ref[...])
        q = (_dot(h1.astype(_CDT), wq_ref[...]) * (MEM_HEAD_DIM ** -0.5)).astype(_CDT)
        yield
        heads = []
        for h in range(MEM_HEADS):
            c0 = h * MEM_HEAD_DIM
            kh = kv_ref[:, c0:c0 + MEM_HEAD_DIM]
            vh = kv_ref[:, D_MODEL + c0:D_MODEL + c0 + MEM_HEAD_DIM]
            s = _dot_nt(q[:, c0:c0 + MEM_HEAD_DIM], kh)
            p = jnp.exp(s - jnp.max(s, axis=1, keepdims=True))
            o = _dot(p.astype(_CDT), vh) / jnp.sum(p, axis=1, keepdims=True)
            heads.append(o.astype(_CDT))
        yield
        xatt = _dot(jnp.concatenate(heads, axis=1), wo_ref[...])
        yield
        h2 = _layer_norm(DEEPNORM_ALPHA * h1 + xatt, g2_ref[...], b2_ref[...])
        _store_row_tiles(h2t_ref.at[pl.ds(g * n * ROW_PITCH, n * ROW_PITCH)], h2)
        logits = _dot(h2.astype(_CDT), wr_ref[...]) + br_ref[...]
        yield
        vals, idxs = [], []
        work = logits
        for _ in range(TOP_K):
            m = jnp.max(work, axis=1, keepdims=True)
            ik = jnp.min(jnp.where(work == m, lane_f, float(LANES)), axis=1, keepdims=True)
            vals.append(m)
            idxs.append(ik)
            work = jnp.where(lane_f == ik, -jnp.inf, work)
        exps = [jnp.exp(v - vals[0]) for v in vals]
        denom = exps[0] + exps[1] + exps[2] + exps[3]
        idx_w = jnp.zeros((n, LANES), _F32)
        wt_w = jnp.zeros((n, LANES), _F32)
        for k in range(TOP_K):
            idx_w = jnp.where(lane == k, idxs[k], idx_w)
            wt_w = jnp.where(lane == k, exps[k] / denom, wt_w)
        idx_ref[rows, :] = idx_w[:, :TOP_K].astype(jnp.int32)
        wt_ref[rows, :] = wt_w[:, :TOP_K]
        chosen = functools.reduce(jnp.add, [jnp.where(lane_f == ik, 1.0, 0.0) for ik in idxs])
        cnt_acc[...] = cnt_acc[...] + jnp.sum(chosen, axis=0, keepdims=True)

    _interleave(row_group(g) for g in range(groups))
    cnt_ref[...] = cnt_acc[...].astype(jnp.int32)


def _post_attn(x2d, out_a, out_b, w_out, g1, b1, mem2d, gm, bm, w_kv, w_q, w_o, g2, b2, w_r, b_r, seq, mem_len,
               *, tm):
    t = x2d.shape[0]
    per_b = seq // tm
    full = lambda shape: pl.BlockSpec(shape, lambda i: (0,) * len(shape))
    vec = full((1, D_MODEL))
    return pl.pallas_call(
        functools.partial(_post_attn_kernel, tm=tm, groups=max(1, tm // MOBA_BLOCK), steps_per_batch=per_b),
        out_shape=(jax.ShapeDtypeStruct((t * ROW_PITCH, LANES), _F32),
                   jax.ShapeDtypeStruct((t, TOP_K), jnp.int32),
                   jax.ShapeDtypeStruct((t, TOP_K), _F32),
                   jax.ShapeDtypeStruct((1, LANES), jnp.int32)),
        grid=(t // tm,),
        in_specs=[pl.BlockSpec((tm, D_MODEL), lambda i: (i, 0)),
                  pl.BlockSpec((tm, DIFF_WIDTH), lambda i: (i, 0)),
                  pl.BlockSpec((tm, MOBA_WIDTH), lambda i: (i, 0)),
                  full((D_MODEL, D_MODEL)), vec, vec,
                  pl.BlockSpec((mem_len, D_MODEL), lambda i: (i // per_b, 0)), vec, vec,
                  full((D_MODEL, 2 * D_MODEL)),
                  full((D_MODEL, D_MODEL)), full((D_MODEL, D_MODEL)), vec, vec,
                  full((D_MODEL, LANES)), full((1, LANES))],
        out_specs=(pl.BlockSpec((tm * ROW_PITCH, LANES), lambda i: (i, 0)),
                   pl.BlockSpec((tm, TOP_K), lambda i: (i, 0)),
                   pl.BlockSpec((tm, TOP_K), lambda i: (i, 0)),
                   pl.BlockSpec((1, LANES), lambda i: (0, 0))),
        scratch_shapes=[pltpu.VMEM((1, LANES), _F32), pltpu.VMEM((mem_len, 2 * D_MODEL), _CDT)],
        compiler_params=_cparams(("arbitrary",)),
        name="post_attn",
    )(x2d, out_a, out_b, w_out, g1, b1, mem2d, gm, bm, w_kv, w_q, w_o, g2, b2, w_r, b_r)


def _rank_kernel(idx_ref, off_ref, pos_ref, next_ref, *, tm):
    @pl.when(pl.program_id(0) == 0)
    def _init():
        next_ref[...] = off_ref[...]

    idx = idx_ref[...]
    lane = lax.broadcasted_iota(jnp.int32, (tm, LANES), 1)
    hits = [lane == idx[:, k:k + 1] for k in range(TOP_K)]
    sel = jnp.zeros((tm, LANES), _F32)
    for hit in hits:
        sel = sel + jnp.where(hit, 1.0, 0.0)
    row = lax.broadcasted_iota(jnp.int32, (tm, tm), 0)
    col = lax.broadcasted_iota(jnp.int32, (tm, tm), 1)
    earlier = jnp.where(col < row, 1.0, 0.0).astype(jnp.bfloat16)
    dense = _dot(earlier, sel.astype(jnp.bfloat16)) + next_ref[...]
    pos_w = jnp.zeros((tm, LANES), _F32)
    for k, hit in enumerate(hits):
        pk = jnp.sum(jnp.where(hit, dense, 0.0), axis=1, keepdims=True)
        pos_w = jnp.where(lane == k, pk, pos_w)
    pos_ref[...] = pos_w[:, :TOP_K].astype(jnp.int32)
    next_ref[...] = next_ref[...] + jnp.sum(sel, axis=0, keepdims=True)


def _rank(idx, offsets, *, tm):
    t = idx.shape[0]
    return pl.pallas_call(
        functools.partial(_rank_kernel, tm=tm),
        out_shape=jax.ShapeDtypeStruct((t, TOP_K), jnp.int32),
        grid=(t // tm,),
        in_specs=[pl.BlockSpec((tm, TOP_K), lambda i: (i, 0)), pl.BlockSpec((1, LANES), lambda i: (0, 0))],
        out_specs=pl.BlockSpec((tm, TOP_K), lambda i: (i, 0)),
        scratch_shapes=[pltpu.VMEM((1, LANES), _F32)],
        compiler_params=_cparams(("arbitrary",)),
        name="rank",
    )(idx, offsets)


_DMA_UNROLL = 8
_DMA_THREADS = 2


def _for_each_row(n_rows, fn):
    tokens_per_group = _DMA_UNROLL // TOP_K

    def group(g, c):
        for u in range(_DMA_UNROLL):
            fn(g * _DMA_UNROLL + u, g * tokens_per_group + u // TOP_K, u % TOP_K)
        return c

    lax.fori_loop(0, n_rows // _DMA_UNROLL, group, 0)


def _dispatch_kernel(meta_ref, h_ref, pos_ref, xs_ref, zero_ref, sem, zsem, *, tm):
    n_rows = tm * TOP_K

    def zero_copy(r):
        return pltpu.make_async_copy(zero_ref, _tile(xs_ref, r, ROW_TILE), zsem)

    @pl.when(pl.program_id(0) == 0)
    def _zero_fill():
        zero_ref[...] = jnp.zeros_like(zero_ref)

        def pad_rows(e):
            return meta_ref[e] + meta_ref[N_EXPERTS + e], meta_ref[e] + meta_ref[2 * N_EXPERTS + e]

        for thread in range(_DMA_THREADS):
            def start_fills(j, c, thread=thread):
                lo, hi = pad_rows(j * _DMA_THREADS + thread)
                lax.fori_loop(lo, hi, lambda r, c2: (zero_copy(r).start(priority=thread), c2)[1], 0)
                return c

            lax.fori_loop(0, N_EXPERTS // _DMA_THREADS, start_fills, 0)

        def wait_fills(e, c):
            lo, hi = pad_rows(e)
            lax.fori_loop(lo, hi, lambda r, c2: (zero_copy(r).wait(), c2)[1], 0)
            return c

        lax.fori_loop(0, N_EXPERTS, wait_fills, 0)

    def row_copy(r, t, k):
        return pltpu.make_async_copy(_tile(h_ref, t, ROW_PITCH), _tile(xs_ref, pos_ref[r], ROW_TILE), sem)

    _for_each_row(n_rows, lambda r, t, k: row_copy(r, t, k).start(priority=k % _DMA_THREADS))
    _for_each_row(n_rows, lambda r, t, k: row_copy(r, t, k).wait())


def _dispatch(meta, h2t, pos_flat, n_sorted_rows, *, tm):
    t = h2t.shape[0] // ROW_PITCH
    return pl.pallas_call(
        functools.partial(_dispatch_kernel, tm=tm),
        out_shape=jax.ShapeDtypeStruct((n_sorted_rows * ROW_TILE, LANES), _F32),
        grid_spec=pltpu.PrefetchScalarGridSpec(
            num_scalar_prefetch=1,
            grid=(t // tm,),
            in_specs=[pl.BlockSpec((tm * ROW_PITCH, LANES), lambda i, meta: (i, 0)),
                      pl.BlockSpec((tm * TOP_K,), lambda i, meta: (i,), memory_space=pltpu.SMEM)],
            out_specs=pl.BlockSpec(memory_space=pl.ANY),
            scratch_shapes=[pltpu.VMEM((ROW_TILE, LANES), _F32),
                            pltpu.SemaphoreType.DMA, pltpu.SemaphoreType.DMA]),
        compiler_params=_cparams(("arbitrary",)),
        name="dispatch",
    )(meta, h2t, pos_flat)


_DEINT = 2 * LANES


def _deinterleave_matrix():
    k = jnp.arange(_DEINT)[:, None]
    n = jnp.arange(_DEINT)[None, :]
    src = jnp.where(n < LANES, 2 * n, 2 * (n - LANES) + 1)
    return (k == src).astype(_CDT)


def _experts_kernel(te_ref, nv_ref, xs_ref, perm_ref, w1_ref, b1g_ref, b1l_ref, w2_ref, b2_ref, y_ref,
                    wg_ref, wl_ref, w2c_ref, *, tm):
    i = pl.program_id(0)
    live = i < nv_ref[0]
    new_expert = (i == 0) | (te_ref[i] != te_ref[jnp.maximum(i - 1, 0)])

    @pl.when(live & new_expert)
    def _stage_weights():
        for c in range(2 * D_FF // _DEINT):
            both = _dot(w1_ref[0, :, c * _DEINT:(c + 1) * _DEINT].astype(_CDT), perm_ref[...])
            wg_ref[:, c * LANES:(c + 1) * LANES] = both[:, :LANES].astype(_CDT)
            wl_ref[:, c * LANES:(c + 1) * LANES] = both[:, LANES:].astype(_CDT)
        w2c_ref[...] = w2_ref[0].astype(_CDT)

    @pl.when(live)
    def _mlp():
        x = _load_row_tiles(xs_ref, tm, ROW_TILE).astype(_CDT)
        glu = jnp.minimum(_dot(x, wg_ref[...]) + b1g_ref[0], SWIGLU_LIMIT)
        lin = jnp.clip(_dot(x, wl_ref[...]) + b1l_ref[0], -SWIGLU_LIMIT, SWIGLU_LIMIT)
        act = glu * jax.nn.sigmoid(SWIGLU_ALPHA * glu) * (lin + 1.0)
        _store_row_tiles(y_ref, _dot(act.astype(_CDT), w2c_ref[...]) + b2_ref[0])

    @pl.when(jnp.logical_not(live))
    def _unused_tile():
        y_ref[...] = jnp.zeros_like(y_ref)


def _experts(tile_expert, n_valid, xs, w1, b1g, b1l, w2, b2, *, tm):
    n_tiles = xs.shape[0] // (tm * ROW_TILE)
    by_expert = lambda shape: pl.BlockSpec((1,) + shape, lambda i, te, nv: (te[i], 0, 0))
    return pl.pallas_call(
        functools.partial(_experts_kernel, tm=tm),
        out_shape=jax.ShapeDtypeStruct((n_tiles * tm * ROW_PITCH, LANES), _F32),
        grid_spec=pltpu.PrefetchScalarGridSpec(
            num_scalar_prefetch=2,
            grid=(n_tiles,),
            in_specs=[pl.BlockSpec((tm * ROW_TILE, LANES), lambda i, te, nv: (jnp.minimum(i, nv[0] - 1), 0)),
                      pl.BlockSpec((_DEINT, _DEINT), lambda i, te, nv: (0, 0)),
                      by_expert((D_MODEL, 2 * D_FF)),
                      by_expert((1, D_FF)), by_expert((1, D_FF)),
                      by_expert((D_FF, D_MODEL)), by_expert((1, D_MODEL))],
            out_specs=pl.BlockSpec((tm * ROW_PITCH, LANES), lambda i, te, nv: (i, 0)),
            scratch_shapes=[pltpu.VMEM((D_MODEL, D_FF), _CDT), pltpu.VMEM((D_MODEL, D_FF), _CDT),
                            pltpu.VMEM((D_FF, D_MODEL), _CDT)]),
        compiler_params=_cparams(("arbitrary",)),
        name="experts",
    )(tile_expert, n_valid, xs, _deinterleave_matrix(), w1, b1g, b1l, w2, b2)


def _combine_kernel(pos_ref, nxt_ref, wt_ref, h2t_ref, g_ref, b_ref, ys_ref, o_ref, buf_ref, sem, *, tm):
    n_rows = tm * TOP_K
    i = pl.program_id(0)
    slot = i & 1

    def row_copy(p_ref, s, r, t, k):
        return pltpu.make_async_copy(_slab(ys_ref, p_ref[r]), _slab(buf_ref.at[s, k], t), sem.at[s])

    @pl.when(i == 0)
    def _first():
        _for_each_row(n_rows, lambda r, t, k: row_copy(pos_ref, 0, r, t, k).start(priority=k % _DMA_THREADS))

    for r in range(n_rows):
        row_copy(pos_ref, slot, r, r // TOP_K, r % TOP_K).wait()
    for r in range(n_rows):
        row_copy(nxt_ref, 1 - slot, r, r // TOP_K, r % TOP_K).start(priority=r % _DMA_THREADS)
    wt = wt_ref[...]
    ffn = wt[:, 0:1] * _load_row_tiles(buf_ref.at[slot, 0], tm)
    for k in range(1, TOP_K):
        ffn = ffn + wt[:, k:k + 1] * _load_row_tiles(buf_ref.at[slot, k], tm)
    o_ref[...] = _layer_norm(DEEPNORM_ALPHA * _load_row_tiles(h2t_ref, tm) + ffn, g_ref[...], b_ref[...])

    @pl.when(i + 1 == pl.num_programs(0))
    def _drain():
        _for_each_row(n_rows, lambda r, t, k: row_copy(nxt_ref, 1 - slot, r, t, k).wait())


def _combine(pos_flat, wts, h2t, g3, b3, ys, *, tm):
    t = h2t.shape[0] // ROW_PITCH
    n_steps = t // tm
    vec = pl.BlockSpec((1, D_MODEL), lambda i: (0, 0))
    return pl.pallas_call(
        functools.partial(_combine_kernel, tm=tm),
        out_shape=jax.ShapeDtypeStruct((t, D_MODEL), _F32),
        grid=(n_steps,),
        in_specs=[pl.BlockSpec((tm * TOP_K,), lambda i: (i,), memory_space=pltpu.SMEM),
                  pl.BlockSpec((tm * TOP_K,), lambda i: (jnp.minimum(i + 1, n_steps - 1),), memory_space=pltpu.SMEM),
                  pl.BlockSpec((tm, TOP_K), lambda i: (i, 0)),
                  pl.BlockSpec((tm * ROW_PITCH, LANES), lambda i: (i, 0)), vec, vec,
                  pl.BlockSpec(memory_space=pl.ANY)],
        out_specs=pl.BlockSpec((tm, D_MODEL), lambda i: (i, 0)),
        scratch_shapes=[pltpu.VMEM((2, TOP_K, tm * ROW_PITCH, LANES), _F32), pltpu.SemaphoreType.DMA((2,))],
        compiler_params=_cparams(("arbitrary",)),
        name="combine",
    )(pos_flat, pos_flat, wts, h2t, g3, b3, ys)


def _tiles(seq):
    big = 512 if seq % 512 == 0 else MOBA_BLOCK
    post = 1024 if seq % 1024 == 0 else big
    return dict(inproj=big, attn_q=MOBA_BLOCK, post=post, rank=big, dispatch=big, expert=512, combine=big)


def kernel(x, mem, w_in, diff_lambda_q1, diff_lambda_k1, diff_lambda_q2, diff_lambda_k2, diff_subln_g,
           w_mix_out, ln1_g, ln1_b, mem_ln_g, mem_ln_b, w_mem_q, w_mem_kv, w_mem_o, ln2_g, ln2_b, w_router,
           b_router, w_mlp1, b_mlp1, w_mlp2, b_mlp2, ln3_g, ln3_b):
    batch, seq, d = x.shape
    mem_len = mem.shape[1]
    assert d == D_MODEL and seq % MOBA_BLOCK == 0 and w_in.shape[0] == 1
    t = batch * seq
    ts = _tiles(seq)
    row = lambda v: v.reshape(1, -1).astype(_F32)

    col = jnp.arange(PROJ_WIDTH)
    is_q = (col < DIFF_WIDTH) | ((col >= 3 * DIFF_WIDTH) & (col < 3 * DIFF_WIDTH + MOBA_WIDTH))
    w_in_c = (w_in[0] * jnp.where(is_q, HEAD_DIM ** -0.5 * math.log2(math.e), 1.0)[None, :]).astype(_CDT)
    lam_vecs = jnp.stack([diff_lambda_q1[0], diff_lambda_k1[0], diff_lambda_q2[0], diff_lambda_k2[0]]).astype(_F32)
    w_r = jnp.zeros((D_MODEL, LANES), _F32).at[:, :N_EXPERTS].set(w_router[0]).astype(_CDT)
    b_r = jnp.full((1, LANES), _NEG, _F32).at[0, :N_EXPERTS].set(b_router[0])
    b1g = b_mlp1[0][:, None, 0::2].astype(_F32)
    b1l = b_mlp1[0][:, None, 1::2].astype(_F32)
    b2 = b_mlp2[0][:, None, :].astype(_F32)

    x2d = x.reshape(t, d)
    proj, kmean = _inproj(x2d, w_in_c, seq, tm=ts["inproj"])
    kmean = kmean.reshape(batch, seq // MOBA_BLOCK, MOBA_WIDTH)
    out_a = _diff_attn(proj, lam_vecs, row(diff_subln_g[0]), batch, seq, tq=ts["attn_q"])
    out_b = _moba_attn(proj, kmean, batch, seq)
    h2t, top_idx, top_w, counts = _post_attn(
        x2d, out_a, out_b, w_mix_out[0].astype(_CDT), row(ln1_g[0]), row(ln1_b[0]),
        mem.reshape(batch * mem_len, d), row(mem_ln_g), row(mem_ln_b), w_mem_kv[0].astype(_CDT),
        w_mem_q[0].astype(_CDT), w_mem_o[0].astype(_CDT), row(ln2_g[0]), row(ln2_b[0]), w_r, b_r,
        seq, mem_len, tm=ts["post"])

    tmx = ts["expert"]
    cnt = counts[0, :N_EXPERTS]
    padded = (cnt + tmx - 1) // tmx * tmx
    ends = jnp.cumsum(padded)
    offsets = ends - padded
    meta = jnp.concatenate([offsets, cnt, padded]).astype(jnp.int32)
    n_sorted_rows = t * TOP_K + N_EXPERTS * tmx
    n_tiles = n_sorted_rows // tmx
    tile_start = jnp.arange(n_tiles, dtype=jnp.int32) * tmx
    tile_expert = jnp.minimum(jnp.sum(tile_start[:, None] >= ends[None, :], axis=1), N_EXPERTS - 1).astype(jnp.int32)
    n_valid = (ends[-1:] // tmx).astype(jnp.int32)
    offsets_row = jnp.zeros((1, LANES), _F32).at[0, :N_EXPERTS].set(offsets.astype(_F32))
    pos = _rank(top_idx, offsets_row, tm=ts["rank"]).reshape(-1)

    xs = _dispatch(meta, h2t, pos, n_sorted_rows, tm=ts["dispatch"])
    ys = _experts(tile_expert, n_valid, xs, w_mlp1[0], b1g, b1l, w_mlp2[0], b2, tm=tmx)
    out = _combine(pos, top_w, h2t, row(ln3_g[0]), row(ln3_b[0]), ys, tm=ts["combine"])
    return out.reshape(batch, seq, d)
```
